```python
import functools
import jax, jax.numpy as jnp
from jax import lax
import numpy as np

D_MODEL = 1024
BATCH = 4
SEQ = 8192
DEPTH = 2
DEC_BATCH = 32
DEC_SEQ = 8
PAST_LEN = 16384
PAGE_SIZE = 128

NH_M = 4
DQK_M = 64
DV_M = 128
W_M = NH_M * DV_M
GATE_CAP = 15.0
NH_S = 8
P_S = 64
D_INNER = NH_S * P_S
N_GROUPS = 2
D_STATE = 128
CONV_W = 4
CONV_CH = D_INNER + 2 * N_GROUPS * D_STATE
NH_F = 8
DH_F = 64
W_F = NH_F * DH_F
Q_BLOCK = 128
CHUNK = 64
D_FF = 4 * D_MODEL
EPS = 1e-6
SPLIT_SIZES = (NH_M * DQK_M, NH_M * DQK_M, W_M, NH_M, NH_M, W_M,
               D_INNER, CONV_CH, NH_S,
               W_F, W_F, W_F, NH_F,
               3 * D_MODEL)
D_IN = sum(SPLIT_SIZES)

kernel_name = 'hybrid_mlstm_ssd_fox_decoder_step'


def rmsnorm(x, g):
    xf = x.astype(jnp.float32)
    y = xf * lax.rsqrt(jnp.mean(xf * xf, axis=-1, keepdims=True) + EPS)
    return (y * g).astype(x.dtype)


def chunked_scan(step, state, xs):
    bn, s = xs[0].shape[:2]
    ln = CHUNK if s % CHUNK == 0 else s
    nc = s // ln
    xs_c = tuple(jnp.moveaxis(t.reshape((bn, nc, ln) + t.shape[2:]), 1, 0) for t in xs)
    state, ys = lax.scan(step, state, xs_c)
    return state, jnp.moveaxis(ys, 0, 1).reshape((bn, s) + ys.shape[3:])


def mlstm_chunk(state, xs):
    c_st, n_st, m_st = state
    q, k, v, logi, logf = xs
    ln = q.shape[1]
    causal = jnp.tril(jnp.ones((ln, ln), bool))[None, :, :, None]
    b = jnp.cumsum(logf, axis=1)
    dmat = jnp.where(causal, b[:, :, None, :] - b[:, None, :, :] + logi[:, None, :, :], -jnp.inf)
    inter = b + m_st[:, None, :]
    m_t = jnp.maximum(inter, jnp.max(dmat, axis=2))
    s = jnp.einsum('bthd,bshd->btsh', q, k) * jnp.exp(dmat - m_t[:, :, None, :])
    a = jnp.exp(inter - m_t)
    num = jnp.einsum('btsh,bshv->bthv', s, v) + a[..., None] * jnp.einsum('bhvd,bthd->bthv', c_st, q)
    den = jnp.sum(s, axis=2) + a * jnp.einsum('bhd,bthd->bth', n_st, q)
    h = num / jnp.maximum(jnp.abs(den), jnp.exp(-m_t))[..., None]
    m_new = m_t[:, -1]
    decay = jnp.exp(b[:, -1] + m_st - m_new)
    wk = jnp.exp(b[:, -1:, :] - b + logi - m_new[:, None, :])
    c_new = decay[..., None, None] * c_st + jnp.einsum('bsh,bshv,bshd->bhvd', wk, v, k)
    n_new = decay[..., None] * n_st + jnp.einsum('bsh,bshd->bhd', wk, k)
    return (c_new, n_new, m_new), h


def ssd_chunk(h_st, xs):
    x, dt, da, bm, cm = xs
    ln = x.shape[1]
    causal = jnp.tril(jnp.ones((ln, ln), bool))[None, :, :, None]
    cum = jnp.cumsum(da, axis=1)
    ldec = jnp.exp(jnp.where(causal, cum[:, :, None, :] - cum[:, None, :, :], -jnp.inf))
    bh = jnp.repeat(bm, NH_S // N_GROUPS, axis=2)
    ch = jnp.repeat(cm, NH_S // N_GROUPS, axis=2)
    xdt = x * dt[..., None]
    cb = jnp.einsum('bthn,bshn->btsh', ch, bh) * ldec
    y = jnp.einsum('btsh,bshp->bthp', cb, xdt) + jnp.exp(cum)[..., None] * jnp.einsum('bthn,bhpn->bthp', ch, h_st)
    wk = jnp.exp(cum[:, -1:, :] - cum)
    h_new = jnp.exp(cum[:, -1])[..., None, None] * h_st + jnp.einsum('bsh,bshp,bshn->bhpn', wk, xdt, bh)
    return h_new, y


def causal_conv(xbc, buf, w, b):
    s = xbc.shape[1]
    full = jnp.concatenate([buf.astype(xbc.dtype), xbc], axis=1)
    out = b + sum(full[:, j:j + s] * w[j] for j in range(CONV_W))
    return out, full[:, s:]


def fox_prompt(q, k, v, logf):
    bn, s = q.shape[:2]
    scale = DH_F ** -0.5
    ft = jnp.moveaxis(jnp.cumsum(logf, axis=1), 2, 1)
    kpos = jnp.arange(s)

    def blk(i):
        start = i * Q_BLOCK
        qb = lax.dynamic_slice_in_dim(q, start, Q_BLOCK, axis=1)
        fq = lax.dynamic_slice_in_dim(ft, start, Q_BLOCK, axis=2)
        sc = jnp.einsum('bqhd,bkhd->bhqk', qb, k) * scale + (fq[..., :, None] - ft[..., None, :])
        mask = kpos[None, :] <= (start + jnp.arange(Q_BLOCK))[:, None]
        p = jax.nn.softmax(jnp.where(mask, sc, -jnp.inf), axis=-1)
        return jnp.einsum('bhqk,bkhd->bqhd', p, v)

    o = lax.map(blk, jnp.arange(s // Q_BLOCK))
    return jnp.moveaxis(o, 0, 1).reshape(bn, s, NH_F, DH_F)


def fox_sample(q, k, v, logf, k_past, v_past, logf_past):
    t = q.shape[1]
    p_len = k_past.shape[1]
    scale = DH_F ** -0.5
    fp = jnp.cumsum(logf_past.astype(jnp.float32), axis=1)
    fn = fp[:, -1:, :] + jnp.cumsum(logf, axis=1)
    fpt = jnp.moveaxis(fp, 2, 1)
    fnt = jnp.moveaxis(fn, 2, 1)
    s_past = jnp.einsum('bqhd,bkhd->bhqk', q, k_past) * scale + (fnt[..., :, None] - fpt[..., None, :])
    s_new = jnp.einsum('bqhd,bkhd->bhqk', q, k) * scale + (fnt[..., :, None] - fnt[..., None, :])
    s_new = jnp.where(jnp.tril(jnp.ones((t, t), bool)), s_new, -jnp.inf)
    p = jax.nn.softmax(jnp.concatenate([s_past, s_new], axis=-1), axis=-1)
    return (jnp.einsum('bhqk,bkhd->bqhd', p[..., :p_len], v_past)
            + jnp.einsum('bhqk,bkhd->bqhd', p[..., p_len:], v))


def trunk_layer(x, c, mlstm_state, conv_buf, ssm_state, attend,
                w_ada, b_ada, g_pre_mix, g_post_mix, w_in, b_mgate, b_ffox, g_mhead,
                conv_w, conv_b, dt_bias, a_log, d_skip, g_ssm,
                w_br_m, w_br_s, w_br_f, w_out, g_pre_mlp, g_post_mlp, w_up, w_down):
    bn, s = x.shape[:2]
    f32 = jnp.float32
    ada = jax.nn.silu(c) @ w_ada + b_ada
    sh1, sc1, gt1, sh2, sc2, gt2 = jnp.split(ada[:, None, :], 6, axis=-1)
    h = rmsnorm(x, g_pre_mix) * (1 + sc1) + sh1
    u = (h @ w_in).astype(f32)
    (mq, mk, mv, mi, mf, mo, sz, sxbc, sdt, fq, fk, fv, ff, gates) = jnp.split(
        u, np.cumsum(SPLIT_SIZES)[:-1].tolist(), axis=-1)
    q = mq.reshape(bn, s, NH_M, DQK_M) * (DQK_M ** -0.5)
    k = mk.reshape(bn, s, NH_M, DQK_M)
    v = mv.reshape(bn, s, NH_M, DV_M)
    gm = jnp.concatenate([mi, mf], axis=-1) + b_mgate
    gm = GATE_CAP * jnp.tanh(gm / GATE_CAP)
    logi = gm[..., :NH_M]
    logf_m = jax.nn.log_sigmoid(gm[..., NH_M:])
    mlstm_state = tuple(t.astype(f32) for t in mlstm_state)
    mlstm_state, hm = chunked_scan(mlstm_chunk, mlstm_state, (q, k, v, logi, logf_m))
    hm = rmsnorm(hm, g_mhead.reshape(NH_M, DV_M)).reshape(bn, s, W_M)
    y_m = hm * jax.nn.sigmoid(mo)
    xbc, conv_new = causal_conv(sxbc, conv_buf.astype(f32), conv_w, conv_b)
    xbc = jax.nn.silu(xbc)
    xs_, bm, cm = jnp.split(xbc, [D_INNER, D_INNER + N_GROUPS * D_STATE], axis=-1)
    xs_ = xs_.reshape(bn, s, NH_S, P_S)
    dt = jax.nn.softplus(sdt + dt_bias)
    a = -jnp.exp(a_log.astype(f32))
    ssm_new, ys = chunked_scan(ssd_chunk, ssm_state.astype(f32),
                               (xs_, dt, dt * a, bm.reshape(bn, s, N_GROUPS, D_STATE),
                                cm.reshape(bn, s, N_GROUPS, D_STATE)))
    ys = ys + d_skip[:, None] * xs_
    y_s = rmsnorm(ys.reshape(bn, s, D_INNER) * jax.nn.silu(sz), g_ssm)
    fqh = fq.reshape(bn, s, NH_F, DH_F)
    fkh = fk.reshape(bn, s, NH_F, DH_F)
    fvh = fv.reshape(bn, s, NH_F, DH_F)
    logf_f = jax.nn.log_sigmoid(ff + b_ffox)
    y_f = attend(fqh, fkh, fvh, logf_f).reshape(bn, s, W_F)
    g_m, g_s, g_f = jnp.split(jax.nn.sigmoid(gates), 3, axis=-1)
    merged = (g_m * (y_m.astype(x.dtype) @ w_br_m) + g_s * (y_s.astype(x.dtype) @ w_br_s)
              + g_f * (y_f.astype(x.dtype) @ w_br_f))
    x = x + gt1 * rmsnorm((merged @ w_out).astype(x.dtype), g_post_mix)
    h2 = rmsnorm(x, g_pre_mlp) * (1 + sc2) + sh2
    f = jnp.square(jax.nn.relu(h2 @ w_up)) @ w_down
    x = x + gt2 * rmsnorm(f, g_post_mlp)
    return x, mlstm_state, conv_new, ssm_new, (fkh, fvh, logf_f)


def setup_inputs(seed: int = 0) -> dict:
    key = jax.random.key(seed)
    ks = iter(jax.random.split(key, 48))
    nrm = lambda shape, sc=1.0: sc * jax.random.normal(next(ks), shape, jnp.float32)
    n_pages = PAST_LEN // PAGE_SIZE
    n_used = DEC_BATCH * n_pages
    n_phys = n_used + n_used // 4
    page_table = jax.random.permutation(next(ks), n_phys)[:n_used].reshape(DEC_BATCH, n_pages).astype(jnp.int32)
    dt0 = jnp.exp(jax.random.uniform(next(ks), (DEPTH, NH_S), jnp.float32, np.log(1e-3), np.log(1e-1)))
    return {
        'x_prompt': nrm((BATCH, SEQ, D_MODEL)),
        'x_sample': nrm((DEC_BATCH, DEC_SEQ, D_MODEL)),
        'cache_k': nrm((DEPTH, n_phys, PAGE_SIZE, NH_F, DH_F)),
        'cache_v': nrm((DEPTH, n_phys, PAGE_SIZE, NH_F, DH_F)),
        'cache_logf': jax.nn.log_sigmoid(3.0 + nrm((DEPTH, n_phys, PAGE_SIZE, NH_F))),
        'state_mlstm_C': nrm((DEPTH, DEC_BATCH, NH_M, DV_M, DQK_M)),
        'state_mlstm_n': nrm((DEPTH, DEC_BATCH, NH_M, DQK_M)),
        'state_mlstm_m': nrm((DEPTH, DEC_BATCH, NH_M)),
        'state_conv': nrm((DEPTH, DEC_BATCH, CONV_W - 1, CONV_CH)),
        'state_ssm': nrm((DEPTH, DEC_BATCH, NH_S, P_S, D_STATE), 0.5),
        'page_table': page_table,
        'c_prompt': nrm((BATCH, D_MODEL)),
        'c_sample': nrm((DEC_BATCH, D_MODEL)),
        'w_ada': nrm((DEPTH, D_MODEL, 6 * D_MODEL), D_MODEL ** -0.5),
        'b_ada': nrm((DEPTH, 6 * D_MODEL), 0.02),
        'g_pre_mix': 1.0 + nrm((DEPTH, D_MODEL), 0.1),
        'g_post_mix': 1.0 + nrm((DEPTH, D_MODEL), 0.1),
        'w_in': nrm((DEPTH, D_MODEL, D_IN), D_MODEL ** -0.5),
        'b_mgate': jnp.concatenate([nrm((DEPTH, NH_M), 0.1), 3.0 + nrm((DEPTH, NH_M), 0.5)], axis=-1),
        'b_ffox': 3.0 + nrm((DEPTH, NH_F), 0.5),
        'g_mhead': 1.0 + nrm((DEPTH, W_M), 0.1),
        'conv_w': nrm((DEPTH, CONV_W, CONV_CH), CONV_W ** -0.5),
        'conv_b': nrm((DEPTH, CONV_CH), 0.02),
        'dt_bias': dt0 + jnp.log(-jnp.expm1(-dt0)),
        'a_log': jnp.log(jax.random.uniform(next(ks), (DEPTH, NH_S), jnp.float32, 1.0, 16.0)),
        'd_skip': 1.0 + nrm((DEPTH, NH_S), 0.1),
        'g_ssm': 1.0 + nrm((DEPTH, D_INNER), 0.1),
        'w_br_m': nrm((DEPTH, W_M, D_MODEL), W_M ** -0.5),
        'w_br_s': nrm((DEPTH, D_INNER, D_MODEL), D_INNER ** -0.5),
        'w_br_f': nrm((DEPTH, W_F, D_MODEL), W_F ** -0.5),
        'w_out': nrm((DEPTH, D_MODEL, D_MODEL), D_MODEL ** -0.5),
        'g_pre_mlp': 1.0 + nrm((DEPTH, D_MODEL), 0.1),
        'g_post_mlp': 1.0 + nrm((DEPTH, D_MODEL), 0.1),
        'w_up': nrm((DEPTH, D_MODEL, D_FF), D_MODEL ** -0.5),
        'w_down': nrm((DEPTH, D_FF, D_MODEL), D_FF ** -0.5),
    }


def reference(x_prompt, x_sample, cache_k, cache_v, cache_logf, state_mlstm_C, state_mlstm_n,
              state_mlstm_m, state_conv, state_ssm, page_table, c_prompt, c_sample,
              w_ada, b_ada, g_pre_mix, g_post_mix, w_in, b_mgate, b_ffox, g_mhead,
              conv_w, conv_b, dt_bias, a_log, d_skip, g_ssm, w_br_m, w_br_s, w_br_f, w_out,
              g_pre_mlp, g_post_mlp, w_up, w_down):
    f32 = jnp.float32
    bp = x_prompt.shape[0]
    bs = x_sample.shape[0]
    past = page_table.shape[1] * cache_k.shape[2]
    yp = x_prompt
    ys = x_sample
    pk, pv, plf, pc, pn, pm, pconv, pssm = [], [], [], [], [], [], [], []
    sk, sv, slf, scs, sns, sms, sconv, sssm = [], [], [], [], [], [], [], []
    for l in range(DEPTH):
        lw = (w_ada[l], b_ada[l], g_pre_mix[l], g_post_mix[l], w_in[l], b_mgate[l], b_ffox[l], g_mhead[l],
              conv_w[l], conv_b[l], dt_bias[l], a_log[l], d_skip[l], g_ssm[l],
              w_br_m[l], w_br_s[l], w_br_f[l], w_out[l], g_pre_mlp[l], g_post_mlp[l], w_up[l], w_down[l])
        st0 = (jnp.zeros((bp, NH_M, DV_M, DQK_M), f32), jnp.zeros((bp, NH_M, DQK_M), f32),
               jnp.zeros((bp, NH_M), f32))
        yp, mst, cbuf, sst, rows = trunk_layer(
            yp, c_prompt, st0, jnp.zeros((bp, CONV_W - 1, CONV_CH), f32),
            jnp.zeros((bp, NH_S, P_S, D_STATE), f32), fox_prompt, *lw)
        pk.append(rows[0]); pv.append(rows[1]); plf.append(rows[2])
        pc.append(mst[0]); pn.append(mst[1]); pm.append(mst[2])
        pconv.append(cbuf); pssm.append(sst)
        k_past = cache_k[l][page_table].reshape(bs, past, NH_F, DH_F)
        v_past = cache_v[l][page_table].reshape(bs, past, NH_F, DH_F)
        lf_past = cache_logf[l][page_table].reshape(bs, past, NH_F)
        attend = functools.partial(fox_sample, k_past=k_past, v_past=v_past, logf_past=lf_past)
        ys, mst, cbuf, sst, rows = trunk_layer(
            ys, c_sample, (state_mlstm_C[l], state_mlstm_n[l], state_mlstm_m[l]),
            state_conv[l], state_ssm[l], attend, *lw)
        sk.append(rows[0]); sv.append(rows[1]); slf.append(rows[2])
        scs.append(mst[0]); sns.append(mst[1]); sms.append(mst[2])
        sconv.append(cbuf); sssm.append(sst)
    return (yp, ys,
            jnp.stack(pk), jnp.stack(pv), jnp.stack(plf), jnp.stack(pc), jnp.stack(pn), jnp.stack(pm),
            jnp.stack(pconv), jnp.stack(pssm),
            jnp.stack(sk), jnp.stack(sv), jnp.stack(slf), jnp.stack(scs), jnp.stack(sns), jnp.stack(sms),
            jnp.stack(sconv), jnp.stack(sssm))
```

```python
import functools

import jax
import jax.numpy as jnp
from jax import lax
from jax.experimental import pallas as pl
from jax.experimental.pallas import tpu as pltpu

F32 = jnp.float32
BF16 = jnp.bfloat16

NH_M, DQK_M, DV_M = 4, 64, 128
W_M = NH_M * DV_M
GATE_CAP = 15.0
NH_S, P_S, N_GROUPS, D_STATE, CONV_W = 8, 64, 2, 128, 4
D_INNER = NH_S * P_S
CONV_CH = D_INNER + 2 * N_GROUPS * D_STATE
NH_F, DH_F = 8, 64
W_F = NH_F * DH_F
CHUNK = 64
EPS = 1e-6
LANES = 128
SUBLANES = 8

LN_LOGI, LN_LOGF, LN_DT, LN_FF, LN_BM, LN_CUM, LN_FT = 0, 4, 8, 16, 24, 32, 40

VMEM_LIMIT = 56 * 1024 * 1024


def _cparams(sem):
    return pltpu.CompilerParams(dimension_semantics=sem, vmem_limit_bytes=VMEM_LIMIT)


def _sigmoid(x):
    return 1.0 / (1.0 + jnp.exp(-x))


def _softplus(x):
    return jnp.maximum(x, 0.0) + jnp.log(1.0 + jnp.exp(-jnp.abs(x)))


def _rms(x, g):
    return x * lax.rsqrt(jnp.mean(x * x, axis=-1, keepdims=True) + EPS) * g


def _dot(a, b):
    return jnp.dot(a, b, preferred_element_type=F32)


def _dot_nt(a, b):
    return lax.dot_general(a, b, (((1,), (1,)), ((), ())), preferred_element_type=F32)


def _dot_tn(a, b):
    return lax.dot_general(a, b, (((0,), (0,)), ((), ())), preferred_element_type=F32)


def _const_spec(shape):
    nd = len(shape)
    return pl.BlockSpec(shape, lambda *_: (0,) * nd)


def _ada_kernel(c_ref, w_ref, b_ref, o_ref):
    c = c_ref[...]
    s = (c * _sigmoid(c)).astype(BF16)
    o_ref[0] = _dot(s, w_ref[0].astype(BF16)) + b_ref[0]


def _ada(c_all, w_ada, b_ada):
    depth, d, n = w_ada.shape
    rows = c_all.shape[0]
    tn = 512
    return pl.pallas_call(
        _ada_kernel,
        grid=(depth, n // tn),
        in_specs=[pl.BlockSpec((rows, d), lambda l, j: (0, 0)),
                  pl.BlockSpec((1, d, tn), lambda l, j: (l, 0, j)),
                  pl.BlockSpec((1, 1, tn), lambda l, j: (l, 0, j))],
        out_specs=pl.BlockSpec((1, rows, tn), lambda l, j: (l, 0, j)),
        out_shape=jax.ShapeDtypeStruct((depth, rows, n), F32),
        compiler_params=_cparams(("arbitrary", "arbitrary")),
        name="ada",
    )(c_all, w_ada, b_ada.reshape(depth, 1, n))


SEGS = (("mq", NH_M * DQK_M, DQK_M ** -0.5), ("mk", NH_M * DQK_M, 1.0), ("mv", W_M, 1.0), ("mo", W_M, 1.0),
        ("sz", D_INNER, 1.0), ("sxbc", CONV_CH, 1.0), ("fq", W_F, 1.0), ("fk", W_F, 1.0), ("fv", W_F, 1.0),
        ("small", LANES, 1.0))
W_PROJ = sum(s[1] for s in SEGS)


def _inproj_kernel(x_ref, sh_ref, sc_ref, g_ref, w_ref, *outs):
    x = x_ref[...]
    tb, ts, d = x.shape
    h = _rms(x, g_ref[...]) * (1.0 + sc_ref[...]) + sh_ref[...]
    hb = h.reshape(tb * ts, d).astype(BF16)
    off = 0
    for (_, wd, scale), o in zip(SEGS, outs[:-1]):
        r = _dot(hb, w_ref[:, off:off + wd])
        if scale != 1.0:
            r = r * scale
        o[...] = r
        off += wd
    outs[-1][...] = outs[-2][...].T


def _x_tiles(b, s, ts_max):
    if s >= ts_max:
        return 1, ts_max
    tb = max(1, min(b, ts_max // s))
    return tb, s


def _ada_spec(tb, d, col):
    return pl.BlockSpec((tb, 1, d), lambda i, j: (i, 0, col))


def _inproj(x, ada, g_pre, w):
    b, s, d = x.shape
    tb, ts = _x_tiles(b, s, 256)
    tm = tb * ts
    nj = s // ts
    m = b * s
    row = lambda i, j: (i * nj + j, 0)
    out_shape = [jax.ShapeDtypeStruct((m, wd), F32) for _, wd, _ in SEGS]
    out_specs = [pl.BlockSpec((tm, wd), row) for _, wd, _ in SEGS]
    out_shape.append(jax.ShapeDtypeStruct((LANES, m), F32))
    out_specs.append(pl.BlockSpec((LANES, tm), lambda i, j: (0, i * nj + j)))
    outs = pl.pallas_call(
        _inproj_kernel,
        grid=(b // tb, nj),
        in_specs=[pl.BlockSpec((tb, ts, d), lambda i, j: (i, j, 0)),
                  _ada_spec(tb, d, 0), _ada_spec(tb, d, 1),
                  _const_spec((1, d)),
                  pl.BlockSpec((d, W_PROJ), lambda i, j: (0, 0), pipeline_mode=pl.Buffered(1))],
        out_specs=out_specs,
        out_shape=out_shape,
        compiler_params=_cparams(("arbitrary", "arbitrary")),
        name="inproj",
    )(x, ada, ada, g_pre.reshape(1, d), w)
    names = [sg[0] for sg in SEGS] + ["small_t"]
    return dict(zip(names, outs))


def _prep_kernel(small_ref, bias_ref, alog_ref, g_ref, gt_ref, carry_ref, *, chunk, seg, tiles_per_seq):
    i = pl.program_id(0)

    @pl.when(i % tiles_per_seq == 0)
    def _():
        carry_ref[...] = jnp.zeros_like(carry_ref)

    v = small_ref[...] + bias_ref[...]
    ts = v.shape[0]
    lane = lax.broadcasted_iota(jnp.int32, v.shape, 1)
    row = lax.broadcasted_iota(jnp.int32, v.shape, 0)
    capped = GATE_CAP * jnp.tanh(v * (1.0 / GATE_CAP))
    is_m = (lane < LN_DT) | ((lane >= LN_BM) & (lane < LN_CUM))
    vv = jnp.where(is_m, capped, v)
    sp = _softplus(vv)
    lsig = -_softplus(-vv)
    a = -jnp.exp(alog_ref[...])
    val = jnp.where(lane < LN_LOGF, vv,
          jnp.where(lane < LN_DT, lsig,
          jnp.where(lane < LN_FF, sp,
          jnp.where(lane < LN_CUM, lsig,
          jnp.where(lane < LN_FT, sp * a, lsig)))))
    val = jnp.where(lane < LN_FT + NH_F, val, 0.0)
    ridx = jnp.where(lane < LN_FT, row % chunk, row % seg)
    ridx = jnp.where(lane >= LN_BM, ridx, -1)
    x = val
    k = 1
    while k < min(ts, max(chunk, seg)):
        x = x + jnp.where(ridx >= k, pltpu.roll(x, k, 0), 0.0)
        k *= 2
    x = x + jnp.where(lane >= LN_FT, carry_ref[...], 0.0)
    carry_ref[...] = x[ts - 1:ts, :]
    g_ref[...] = x
    gt_ref[...] = x.T


def _prep(small, bias_lanes, alog_lanes, s, chunk):
    m = small.shape[0]
    ts = min(512, m)
    if s >= ts:
        seg, tiles_per_seq = ts, s // ts
    else:
        seg, tiles_per_seq = s, 1
    return pl.pallas_call(
        functools.partial(_prep_kernel, chunk=chunk, seg=seg, tiles_per_seq=tiles_per_seq),
        grid=(m // ts,),
        in_specs=[pl.BlockSpec((ts, LANES), lambda i: (i, 0)), _const_spec((1, LANES)), _const_spec((1, LANES))],
        out_specs=[pl.BlockSpec((ts, LANES), lambda i: (i, 0)), pl.BlockSpec((LANES, ts), lambda i: (0, i))],
        out_shape=[jax.ShapeDtypeStruct((m, LANES), F32), jax.ShapeDtypeStruct((LANES, m), F32)],
        scratch_shapes=[pltpu.VMEM((1, LANES), F32)],
        compiler_params=_cparams(("arbitrary",)),
        name="prep",
    )(small, bias_lanes, alog_lanes)


def _mlstm_kernel(q_ref, k_ref, v_ref, mo_ref, g_ref, gt_ref, gh_ref, c0_ref, n0_ref, m0_ref,
                  y_ref, c_ref, n_ref, m_ref, *, chunk, gt3d, mm):
    j = pl.program_id(1)

    @pl.when(j == 0)
    def _():
        c_ref[...] = c0_ref[...]
        n_ref[...] = n0_ref[...]
        m_ref[...] = m0_ref[...]

    ts = q_ref.shape[0]
    ln = chunk
    gt = gt_ref[0] if gt3d else gt_ref[...]
    rr = lax.broadcasted_iota(jnp.int32, (ln, ln), 0)
    cc = lax.broadcasted_iota(jnp.int32, (ln, ln), 1)
    causal = cc <= rr
    for c in range(ts // ln):
        lo, hi = c * ln, (c + 1) * ln
        g = g_ref[lo:hi, :]
        for h in range(NH_M):
            qf = q_ref[lo:hi, h * DQK_M:(h + 1) * DQK_M]
            kf = k_ref[lo:hi, h * DQK_M:(h + 1) * DQK_M]
            vf = v_ref[lo:hi, h * DV_M:(h + 1) * DV_M]
            qb, kb = qf.astype(mm), kf.astype(mm)
            logi_r = gt[LN_LOGI + h:LN_LOGI + h + 1, lo:hi]
            logi_c = g[:, LN_LOGI + h:LN_LOGI + h + 1]
            b_r = gt[LN_BM + h:LN_BM + h + 1, lo:hi]
            b_c = g[:, LN_BM + h:LN_BM + h + 1]
            m_st = m_ref[0, h][:, 0:1]
            ct = c_ref[0, h]
            nv = n_ref[0, h]
            dm = jnp.where(causal, b_c - b_r + logi_r, -jnp.inf)
            inter = b_c + m_st
            m_t = jnp.maximum(inter, jnp.max(dm, axis=1, keepdims=True))
            s = _dot_nt(qb, kb) * jnp.exp(dm - m_t)
            a = jnp.exp(inter - m_t)
            num = _dot(s.astype(mm), vf.astype(mm)) + a * _dot(qb, ct.astype(mm))
            den = jnp.sum(s, axis=1, keepdims=True) + a * jnp.sum(qf * nv, axis=1, keepdims=True)
            hv = num / jnp.maximum(jnp.abs(den), jnp.exp(-m_t))
            m_new = m_t[ln - 1:ln, :]
            b_last = b_c[ln - 1:ln, :]
            decay = jnp.exp(b_last + m_st - m_new)
            wk = jnp.exp(b_last - b_c + logi_c - m_new)
            c_ref[0, h] = decay * ct + _dot_tn(kb, (wk * vf).astype(mm))
            n_ref[0, h] = decay * nv + jnp.sum(wk * kf, axis=0, keepdims=True)
            m_ref[0, h] = jnp.broadcast_to(m_new, (1, LANES))
            hm = _rms(hv, gh_ref[:, h * DV_M:(h + 1) * DV_M])
            y = hm * _sigmoid(mo_ref[lo:hi, h * DV_M:(h + 1) * DV_M])
            y_ref[lo:hi, h * DV_M:(h + 1) * DV_M] = y.astype(y_ref.dtype)


def _gt_spec(b, s, ts, nj, gt3d):
    if gt3d:
        return pl.BlockSpec((1, LANES, s), lambda i, j: (i, 0, 0))
    return pl.BlockSpec((LANES, ts), lambda i, j: (0, i * nj + j))


def _mlstm(u, g, gt, g_mhead, c0t, n0, m0, b, s, chunk, mm):
    ts = min(256, s)
    nj = s // ts
    m = b * s
    gt3d = gt.ndim == 3
    row = lambda i, j: (i * nj + j, 0)
    st = lambda *shape: pl.BlockSpec((1,) + shape, lambda i, j: (i,) + (0,) * len(shape))
    return pl.pallas_call(
        functools.partial(_mlstm_kernel, chunk=chunk, gt3d=gt3d, mm=mm),
        grid=(b, nj),
        in_specs=[pl.BlockSpec((ts, NH_M * DQK_M), row), pl.BlockSpec((ts, NH_M * DQK_M), row),
                  pl.BlockSpec((ts, W_M), row), pl.BlockSpec((ts, W_M), row),
                  pl.BlockSpec((ts, LANES), row), _gt_spec(b, s, ts, nj, gt3d),
                  _const_spec((1, W_M)),
                  st(NH_M, DQK_M, DV_M), st(NH_M, 1, DQK_M), st(NH_M, 1, LANES)],
        out_specs=[pl.BlockSpec((ts, W_M), row),
                   st(NH_M, DQK_M, DV_M), st(NH_M, 1, DQK_M), st(NH_M, 1, LANES)],
        out_shape=[jax.ShapeDtypeStruct((m, W_M), BF16),
                   jax.ShapeDtypeStruct((b, NH_M, DQK_M, DV_M), F32),
                   jax.ShapeDtypeStruct((b, NH_M, 1, DQK_M), F32),
                   jax.ShapeDtypeStruct((b, NH_M, 1, LANES), F32)],
        compiler_params=_cparams(("arbitrary", "arbitrary")),
        name="mlstm",
    )(u["mq"], u["mk"], u["mv"], u["mo"], g, gt, g_mhead.reshape(1, W_M), c0t, n0, m0)


def _ssd_kernel(xbc_ref, sz_ref, g_ref, gt_ref, cw_ref, cb_ref, dsk_ref, gs_ref, conv0_ref, h0_ref,
                y_ref, hst_ref, tail_ref, *, chunk, gt3d, mm):
    j = pl.program_id(1)

    @pl.when(j == 0)
    def _():
        hst_ref[...] = h0_ref[...]
        tail_ref[...] = conv0_ref[0]

    x = xbc_ref[...]
    ts = x.shape[0]
    ln = chunk
    prev = tail_ref[...]
    row8 = lax.broadcasted_iota(jnp.int32, prev.shape, 0)
    acc = cb_ref[...] + cw_ref[CONV_W - 1:CONV_W, :] * x
    for k in range(1, CONV_W):
        rolled = pltpu.roll(x, k, 0)
        first = jnp.where(row8 < k, pltpu.roll(prev, k, 0), rolled[:SUBLANES])
        shifted = first if ts == SUBLANES else jnp.concatenate([first, rolled[SUBLANES:]], axis=0)
        acc = acc + cw_ref[CONV_W - 1 - k:CONV_W - k, :] * shifted
    tail_ref[...] = x[ts - SUBLANES:, :]
    xa = acc * _sigmoid(acc)

    gt = gt_ref[0] if gt3d else gt_ref[...]
    rr = lax.broadcasted_iota(jnp.int32, (ln, ln), 0)
    cc = lax.broadcasted_iota(jnp.int32, (ln, ln), 1)
    causal = cc <= rr
    hpg = NH_S // N_GROUPS
    for c in range(ts // ln):
        lo, hi = c * ln, (c + 1) * ln
        g = g_ref[lo:hi, :]
        ys = []
        for grp in range(N_GROUPS):
            bo = D_INNER + grp * D_STATE
            co = D_INNER + N_GROUPS * D_STATE + grp * D_STATE
            bmg = xa[lo:hi, bo:bo + D_STATE].astype(mm)
            cmg = xa[lo:hi, co:co + D_STATE].astype(mm)
            cbm = _dot_nt(cmg, bmg)
            for hh in range(hpg):
                h = grp * hpg + hh
                cum_c = g[:, LN_CUM + h:LN_CUM + h + 1]
                cum_r = gt[LN_CUM + h:LN_CUM + h + 1, lo:hi]
                dt_c = g[:, LN_DT + h:LN_DT + h + 1]
                xh = xa[lo:hi, h * P_S:(h + 1) * P_S]
                xdt = xh * dt_c
                ldec = jnp.exp(jnp.where(causal, cum_c - cum_r, -jnp.inf))
                hst = hst_ref[0, h]
                y = _dot((cbm * ldec).astype(mm), xdt.astype(mm))
                y = y + jnp.exp(cum_c) * _dot_nt(cmg, hst.astype(mm))
                c_last = cum_c[ln - 1:ln, :]
                wk = jnp.exp(c_last - cum_c)
                hst_ref[0, h] = jnp.exp(c_last) * hst + _dot_tn((wk * xdt).astype(mm), bmg)
                ys.append(y + dsk_ref[:, h * P_S:(h + 1) * P_S] * xh)
        yy = jnp.concatenate(ys, axis=1)
        z = sz_ref[lo:hi, :]
        y_ref[lo:hi, :] = _rms(yy * (z * _sigmoid(z)), gs_ref[...]).astype(y_ref.dtype)


def _ssd(u, g, gt, conv_w, conv_b, d_skip, g_ssm, conv0, h0, b, s, chunk, mm):
    ts = min(256, s)
    nj = s // ts
    m = b * s
    gt3d = gt.ndim == 3
    row = lambda i, j: (i * nj + j, 0)
    st = lambda *shape: pl.BlockSpec((1,) + shape, lambda i, j: (i,) + (0,) * len(shape))
    return pl.pallas_call(
        functools.partial(_ssd_kernel, chunk=chunk, gt3d=gt3d, mm=mm),
        grid=(b, nj),
        in_specs=[pl.BlockSpec((ts, CONV_CH), row), pl.BlockSpec((ts, D_INNER), row),
                  pl.BlockSpec((ts, LANES), row), _gt_spec(b, s, ts, nj, gt3d),
                  _const_spec((CONV_W, CONV_CH)), _const_spec((1, CONV_CH)),
                  _const_spec((1, D_INNER)), _const_spec((1, D_INNER)),
                  st(SUBLANES, CONV_CH), st(NH_S, P_S, D_STATE)],
        out_specs=[pl.BlockSpec((ts, D_INNER), row), st(NH_S, P_S, D_STATE)],
        out_shape=[jax.ShapeDtypeStruct((m, D_INNER), BF16),
                   jax.ShapeDtypeStruct((b, NH_S, P_S, D_STATE), F32)],
        scratch_shapes=[pltpu.VMEM((SUBLANES, CONV_CH), F32)],
        compiler_params=_cparams(("arbitrary", "arbitrary")),
        name="ssd",
    )(u["sxbc"], u["sz"], g, gt, conv_w, conv_b.reshape(1, CONV_CH),
      jnp.repeat(d_skip, P_S).reshape(1, D_INNER), g_ssm.reshape(1, D_INNER), conv0, h0)


def _foxp_kernel(q_ref, k_ref, v_ref, g_ref, ftr_ref, o_ref, *, tq):
    hp = pl.program_id(1)
    i = pl.program_id(2)
    g = g_ref[...]
    lane = lax.broadcasted_iota(jnp.int32, g.shape, 1)
    rr = lax.broadcasted_iota(jnp.int32, (tq, tq), 0)
    cc = lax.broadcasted_iota(jnp.int32, (tq, tq), 1)
    causal = cc <= rr
    outs = []
    for hh in range(2):
        h = hp * 2 + hh
        sl = slice(hh * DH_F, (hh + 1) * DH_F)
        qb = (q_ref[:, sl] * (DH_F ** -0.5)).astype(BF16)
        ft_c = jnp.sum(jnp.where(lane == LN_FT + h, g, 0.0), axis=1, keepdims=True)

        def scores(jb):
            ks = pl.multiple_of(jb * tq, tq)
            kb = k_ref[pl.ds(ks, tq), sl].astype(BF16)
            vb = v_ref[pl.ds(ks, tq), sl].astype(BF16)
            ft_r = ftr_ref[pl.ds(h, 1), pl.ds(ks, tq)]
            return _dot_nt(qb, kb) + (ft_c - ft_r), vb

        def update(carry, s, vb):
            m, l, acc = carry
            m_new = jnp.maximum(m, jnp.max(s, axis=1, keepdims=True))
            p = jnp.exp(s - m_new)
            alpha = jnp.exp(m - m_new)
            l = alpha * l + jnp.sum(p, axis=1, keepdims=True)
            acc = alpha * acc + _dot(p.astype(BF16), vb)
            return m_new, l, acc

        def body(jb, carry):
            s, vb = scores(jb)
            return update(carry, s, vb)

        init = (jnp.full((tq, 1), -jnp.inf, F32), jnp.zeros((tq, 1), F32), jnp.zeros((tq, DH_F), F32))
        carry = lax.fori_loop(0, i, body, init)
        s, vb = scores(i)
        m, l, acc = update(carry, jnp.where(causal, s, -jnp.inf), vb)
        outs.append(acc / l)
    o_ref[...] = jnp.concatenate(outs, axis=1).astype(o_ref.dtype)


def _fox_prompt(u, g, gt, b, s):
    tq = min(256, s)
    nq = s // tq
    m = b * s
    hw = 2 * DH_F
    return pl.pallas_call(
        functools.partial(_foxp_kernel, tq=tq),
        grid=(b, NH_F // 2, nq),
        in_specs=[pl.BlockSpec((tq, hw), lambda bb, hp, i: (bb * nq + i, hp)),
                  pl.BlockSpec((s, hw), lambda bb, hp, i: (bb, hp)),
                  pl.BlockSpec((s, hw), lambda bb, hp, i: (bb, hp)),
                  pl.BlockSpec((tq, LANES), lambda bb, hp, i: (bb * nq + i, 0)),
                  pl.BlockSpec((SUBLANES, s), lambda bb, hp, i: (LN_FT // SUBLANES, bb))],
        out_specs=pl.BlockSpec((tq, hw), lambda bb, hp, i: (bb * nq + i, hp)),
        out_shape=jax.ShapeDtypeStruct((m, W_F), BF16),
        compiler_params=_cparams(("arbitrary", "arbitrary", "arbitrary")),
        name="fox_prompt",
    )(u["fq"], u["fk"], u["fv"], g, gt)


def _lane_cumsum(x):
    lane = lax.broadcasted_iota(jnp.int32, x.shape, 1)
    k = 1
    while k < x.shape[1]:
        x = x + jnp.where(lane >= k, pltpu.roll(x, k, 1), 0.0)
        k *= 2
    return x


def _foxs_kernel(pt_ref, q_ref, kn_ref, vn_ref, gt_ref, *rest, pages):
    del pt_ref
    k_refs, v_refs, lf_refs = rest[:pages], rest[pages:2 * pages], rest[2 * pages:3 * pages]
    o_ref, qbd_ref, m_ref, l_ref, acc_ref, carry_ref = rest[3 * pages:]
    gi = pl.program_id(1)
    t = q_ref.shape[0]
    ht = NH_F * t
    page = k_refs[0].shape[1]

    def expand_heads(x):
        return jnp.broadcast_to(x[:, None, :], (NH_F, t, x.shape[1])).reshape(ht, x.shape[1])

    @pl.when(gi == 0)
    def _():
        q = q_ref[...] * (DH_F ** -0.5)
        qt = jnp.broadcast_to(q[None], (NH_F, t, W_F)).reshape(ht, W_F)
        rh = lax.broadcasted_iota(jnp.int32, (ht, W_F), 0) // t
        lh = lax.broadcasted_iota(jnp.int32, (ht, W_F), 1) // DH_F
        qbd_ref[...] = jnp.where(rh == lh, qt, 0.0)
        m_ref[...] = jnp.full_like(m_ref, -jnp.inf)
        l_ref[...] = jnp.zeros_like(l_ref)
        acc_ref[...] = jnp.zeros_like(acc_ref)
        carry_ref[...] = jnp.zeros_like(carry_ref)

    cn_rows = expand_heads(gt_ref[0][LN_FT:LN_FT + NH_F, :])
    tq_idx = lax.broadcasted_iota(jnp.int32, (ht, t), 0) % t
    tk_idx = lax.broadcasted_iota(jnp.int32, (ht, t), 1)
    cn_col = jnp.sum(jnp.where(tq_idx == tk_idx, cn_rows, 0.0), axis=1, keepdims=True)

    qbd = qbd_ref[...]
    qbd_b = qbd.astype(BF16)

    def update(s, pv):
        m = m_ref[...]
        m_new = jnp.maximum(m, jnp.max(s, axis=1, keepdims=True))
        p = jnp.exp(s - m_new)
        alpha = jnp.exp(m - m_new)
        l_ref[...] = alpha * l_ref[...] + jnp.sum(p, axis=1, keepdims=True)
        acc_ref[...] = alpha * acc_ref[...] + pv(p)
        m_ref[...] = m_new

    carry = carry_ref[...]
    ss = [None] * pages
    for pi in reversed(range(pages)):
        pre = _lane_cumsum(lf_refs[pi][0])
        tot = pre[:, page - 1:page]
        r = carry + tot - pre
        carry = carry + tot
        ss[pi] = _dot_nt(qbd_b, k_refs[pi][0].astype(BF16)) + expand_heads(r)
    carry_ref[...] = carry
    s_all = jnp.concatenate(ss, axis=1) + cn_col

    def pv_past(p):
        out = None
        for pi in range(pages):
            d = _dot(p[:, pi * page:(pi + 1) * page].astype(BF16), v_refs[pi][0].astype(BF16))
            out = d if out is None else out + d
        return out

    update(s_all, pv_past)

    @pl.when(gi == pl.num_programs(1) - 1)
    def _():
        s_new = _dot_nt(qbd, kn_ref[...]) + (cn_col - cn_rows)
        s_new = jnp.where(tk_idx <= tq_idx, s_new, -jnp.inf)
        update(s_new, lambda p: _dot(p, vn_ref[...]))
        o = acc_ref[...] / l_ref[...]
        lh = lax.broadcasted_iota(jnp.int32, (t, W_F), 1) // DH_F
        y = jnp.zeros((t, W_F), F32)
        for h in range(NH_F):
            y = y + jnp.where(lh == h, o[h * t:(h + 1) * t, :], 0.0)
        o_ref[...] = y.astype(o_ref.dtype)


def _fox_sample(u, gt3, page_table, ck, cv, clf_t, b, t):
    n_pages = page_table.shape[1]
    page = ck.shape[1]
    pages = 8
    while n_pages % pages:
        pages //= 2
    ng = n_pages // pages
    ht = NH_F * t

    def page_spec(width_shape, pi):
        return pl.BlockSpec((1,) + width_shape,
                            lambda bb, gi, pt: (pt[bb, (ng - 1 - gi) * pages + pi], 0, 0))

    tok = pl.BlockSpec((t, W_F), lambda bb, gi, pt: (bb, 0))
    in_specs = [tok, tok, tok, pl.BlockSpec((1, LANES, t), lambda bb, gi, pt: (bb, 0, 0))]
    in_specs += [page_spec((page, W_F), pi) for pi in range(pages)]
    in_specs += [page_spec((page, W_F), pi) for pi in range(pages)]
    in_specs += [page_spec((NH_F, page), pi) for pi in range(pages)]
    grid_spec = pltpu.PrefetchScalarGridSpec(
        num_scalar_prefetch=1,
        grid=(b, ng),
        in_specs=in_specs,
        out_specs=pl.BlockSpec((t, W_F), lambda bb, gi, pt: (bb, 0)),
        scratch_shapes=[pltpu.VMEM((ht, W_F), F32), pltpu.VMEM((ht, 1), F32), pltpu.VMEM((ht, 1), F32),
                        pltpu.VMEM((ht, W_F), F32), pltpu.VMEM((NH_F, 1), F32)],
    )
    return pl.pallas_call(
        functools.partial(_foxs_kernel, pages=pages),
        grid_spec=grid_spec,
        out_shape=jax.ShapeDtypeStruct((b * t, W_F), BF16),
        compiler_params=_cparams(("arbitrary", "arbitrary")),
        name="fox_sample",
    )(page_table, u["fq"], u["fk"], u["fv"], gt3, *([ck] * pages), *([cv] * pages), *([clf_t] * pages))


def _merge_kernel(x_ref, sh_ref, sc_ref, gt_ref, gpre_ref, gpost_ref, ym_ref, ys_ref, yf_ref,
                  wg_ref, wm_ref, ws_ref, wf_ref, wo_ref, o_ref):
    x = x_ref[...]
    tb, ts, d = x.shape
    h = _rms(x, gpre_ref[...]) * (1.0 + sc_ref[...]) + sh_ref[...]
    hb = h.reshape(tb * ts, d).astype(BF16)
    merged = None
    for bi, (y_ref, w_ref) in enumerate(((ym_ref, wm_ref), (ys_ref, ws_ref), (yf_ref, wf_ref))):
        gate = _sigmoid(_dot(hb, wg_ref[:, bi * d:(bi + 1) * d]))
        term = gate * _dot(y_ref[...], w_ref[...])
        merged = term if merged is None else merged + term
    out = _dot(merged.astype(BF16), wo_ref[...]).reshape(tb, ts, d)
    o_ref[...] = x + gt_ref[...] * _rms(out, gpost_ref[...])


def _merge(x, ada, g_pre, g_post, ym, ys, yf, wg, wm, ws, wf, wo):
    b, s, d = x.shape
    tb, ts = _x_tiles(b, s, 256)
    tm = tb * ts
    nj = s // ts
    row = lambda i, j: (i * nj + j, 0)
    wspec = lambda shape: pl.BlockSpec(shape, lambda i, j: (0, 0), pipeline_mode=pl.Buffered(1))
    return pl.pallas_call(
        _merge_kernel,
        grid=(b // tb, nj),
        in_specs=[pl.BlockSpec((tb, ts, d), lambda i, j: (i, j, 0)),
                  _ada_spec(tb, d, 0), _ada_spec(tb, d, 1), _ada_spec(tb, d, 2),
                  _const_spec((1, d)), _const_spec((1, d)),
                  pl.BlockSpec((tm, W_M), row), pl.BlockSpec((tm, D_INNER), row), pl.BlockSpec((tm, W_F), row),
                  wspec((d, 3 * d)), wspec((W_M, d)), wspec((D_INNER, d)), wspec((W_F, d)), wspec((d, d))],
        out_specs=pl.BlockSpec((tb, ts, d), lambda i, j: (i, j, 0)),
        out_shape=jax.ShapeDtypeStruct((b, s, d), F32),
        compiler_params=_cparams(("arbitrary", "arbitrary")),
        name="merge",
    )(x, ada, ada, ada, g_pre.reshape(1, d), g_post.reshape(1, d), ym, ys, yf, wg, wm, ws, wf, wo)


def _mlp_kernel(x_ref, sh_ref, sc_ref, gt_ref, gpre_ref, gpost_ref, wu_ref, wd_ref, o_ref):
    x = x_ref[...]
    tb, ts, d = x.shape
    h = _rms(x, gpre_ref[...]) * (1.0 + sc_ref[...]) + sh_ref[...]
    hb = h.reshape(tb * ts, d).astype(BF16)
    up = jnp.maximum(_dot(hb, wu_ref[...]), 0.0)
    f = _dot((up * up).astype(BF16), wd_ref[...]).reshape(tb, ts, d)
    o_ref[...] = x + gt_ref[...] * _rms(f, gpost_ref[...])


def _mlp(x, ada, g_pre, g_post, wu, wd):
    b, s, d = x.shape
    tb, ts = _x_tiles(b, s, 256)
    nj = s // ts
    dff = wu.shape[1]
    wspec = lambda shape: pl.BlockSpec(shape, lambda i, j: (0, 0), pipeline_mode=pl.Buffered(1))
    return pl.pallas_call(
        _mlp_kernel,
        grid=(b // tb, nj),
        in_specs=[pl.BlockSpec((tb, ts, d), lambda i, j: (i, j, 0)),
                  _ada_spec(tb, d, 3), _ada_spec(tb, d, 4), _ada_spec(tb, d, 5),
                  _const_spec((1, d)), _const_spec((1, d)),
                  wspec((d, dff)), wspec((dff, d))],
        out_specs=pl.BlockSpec((tb, ts, d), lambda i, j: (i, j, 0)),
        out_shape=jax.ShapeDtypeStruct((b, s, d), F32),
        compiler_params=_cparams(("arbitrary", "arbitrary")),
        name="mlp",
    )(x, ada, ada, ada, g_pre.reshape(1, d), g_post.reshape(1, d), wu, wd)


def _split_w_in(w):
    sizes = (NH_M * DQK_M, NH_M * DQK_M, W_M, NH_M, NH_M, W_M, D_INNER, CONV_CH, NH_S, W_F, W_F, W_F, NH_F)
    names = ("mq", "mk", "mv", "mi", "mf", "mo", "sz", "sxbc", "sdt", "fq", "fk", "fv", "ff")
    cols, off = {}, 0
    for nm, sz in zip(names, sizes):
        cols[nm] = w[:, off:off + sz]
        off += sz
    gates = w[:, off:]
    d = w.shape[0]
    zeros = lambda n: jnp.zeros((d, n), w.dtype)
    small = jnp.concatenate(
        [cols["mi"], cols["mf"], cols["sdt"], cols["ff"], cols["mf"], zeros(LN_CUM - LN_BM - NH_M),
         cols["sdt"], cols["ff"], zeros(LANES - LN_FT - NH_F)], axis=1)
    proj = jnp.concatenate([cols[nm] for nm, _, _ in SEGS[:-1]] + [small], axis=1)
    return proj.astype(BF16), gates.astype(BF16)


def _gate_lanes(b_mgate, dt_bias, b_ffox, a_log):
    z = lambda n: jnp.zeros((n,), F32)
    bias = jnp.concatenate([b_mgate, dt_bias, b_ffox, b_mgate[NH_M:], z(LN_CUM - LN_BM - NH_M),
                            dt_bias, b_ffox, z(LANES - LN_FT - NH_F)])
    alog = jnp.concatenate([z(LN_CUM), a_log, z(LANES - LN_CUM - NH_S)])
    return bias.reshape(1, LANES), alog.reshape(1, LANES)


def _mixer(x, ada, lw, state, attend, chunk, mm):
    b, s, d = x.shape
    u = _inproj(x, ada, lw["g_pre_mix"], lw["w_proj"])
    g, gt = _prep(u["small"], lw["bias_lanes"], lw["alog_lanes"], s, chunk)
    if s % LANES:
        gt = gt.reshape(LANES, b, s).transpose(1, 0, 2)
    c0t, n0, m0, conv0, h0 = state
    ym, c_t, n_new, m_new = _mlstm(u, g, gt, lw["g_mhead"], c0t, n0, m0, b, s, chunk, mm)
    ys, h_new = _ssd(u, g, gt, lw["conv_w"], lw["conv_b"], lw["d_skip"], lw["g_ssm"], conv0, h0, b, s, chunk, mm)
    yf = attend(u, g, gt)
    x1 = _merge(x, ada, lw["g_pre_mix"], lw["g_post_mix"], ym, ys, yf,
                lw["w_gates"], lw["w_br_m"], lw["w_br_s"], lw["w_br_f"], lw["w_out"])
    x2 = _mlp(x1, ada, lw["g_pre_mlp"], lw["g_post_mlp"], lw["w_up"], lw["w_down"])
    outs = (u["fk"].reshape(b, s, NH_F, DH_F), u["fv"].reshape(b, s, NH_F, DH_F),
            g[:, LN_FF:LN_FF + NH_F].reshape(b, s, NH_F),
            jnp.swapaxes(c_t, -1, -2), n_new[:, :, 0, :], m_new[:, :, 0, 0],
            u["sxbc"].reshape(b, s, CONV_CH)[:, s - (CONV_W - 1):, :], h_new)
    return x2, outs


def kernel(x_prompt, x_sample, cache_k, cache_v, cache_logf, state_mlstm_C, state_mlstm_n, state_mlstm_m,
           state_conv, state_ssm, page_table, c_prompt, c_sample, w_ada, b_ada, g_pre_mix, g_post_mix, w_in,
           b_mgate, b_ffox, g_mhead, conv_w, conv_b, dt_bias, a_log, d_skip, g_ssm, w_br_m, w_br_s, w_br_f,
           w_out, g_pre_mlp, g_post_mlp, w_up, w_down):
    depth = w_in.shape[0]
    bp, sp, d = x_prompt.shape
    bs, ss, _ = x_sample.shape
    n_phys, page = cache_k.shape[1], cache_k.shape[2]

    pad = (-(bp + bs)) % SUBLANES
    c_all = jnp.concatenate([c_prompt, c_sample, jnp.zeros((pad, d), F32)], axis=0)
    ada_all = _ada(c_all, w_ada, b_ada)

    chunk_p = CHUNK if sp % CHUNK == 0 else sp
    chunk_s = CHUNK if ss % CHUNK == 0 else ss
    yp, ys = x_prompt, x_sample
    res_p, res_s = [], []
    for l in range(depth):
        w_proj, w_gates = _split_w_in(w_in[l])
        bias_lanes, alog_lanes = _gate_lanes(b_mgate[l], dt_bias[l], b_ffox[l], a_log[l])
        lw = dict(w_proj=w_proj, w_gates=w_gates, bias_lanes=bias_lanes, alog_lanes=alog_lanes,
                  g_pre_mix=g_pre_mix[l], g_post_mix=g_post_mix[l], g_mhead=g_mhead[l],
                  conv_w=conv_w[l], conv_b=conv_b[l], d_skip=d_skip[l], g_ssm=g_ssm[l],
                  w_br_m=w_br_m[l].astype(BF16), w_br_s=w_br_s[l].astype(BF16), w_br_f=w_br_f[l].astype(BF16),
                  w_out=w_out[l].astype(BF16), g_pre_mlp=g_pre_mlp[l], g_post_mlp=g_post_mlp[l],
                  w_up=w_up[l].astype(BF16), w_down=w_down[l].astype(BF16))
        ada_p = ada_all[l, :bp][:, None, :]
        ada_s = ada_all[l, bp:bp + bs][:, None, :]

        zero_state = (jnp.zeros((bp, NH_M, DQK_M, DV_M), F32), jnp.zeros((bp, NH_M, 1, DQK_M), F32),
                      jnp.zeros((bp, NH_M, 1, LANES), F32), jnp.zeros((bp, SUBLANES, CONV_CH), F32),
                      jnp.zeros((bp, NH_S, P_S, D_STATE), F32))
        yp, outs = _mixer(yp, ada_p, lw, zero_state,
                          lambda u, g, gt: _fox_prompt(u, g, gt, bp, sp), chunk_p, BF16)
        res_p.append(outs)

        state = (jnp.swapaxes(state_mlstm_C[l], -1, -2), state_mlstm_n[l][:, :, None, :],
                 jnp.broadcast_to(state_mlstm_m[l][:, :, None, None], (bs, NH_M, 1, LANES)),
                 jnp.pad(state_conv[l], ((0, 0), (SUBLANES - (CONV_W - 1), 0), (0, 0))),
                 state_ssm[l])
        ck = cache_k[l].reshape(n_phys, page, W_F)
        cv = cache_v[l].reshape(n_phys, page, W_F)
        clf_t = jnp.swapaxes(cache_logf[l], 1, 2)
        ys, outs = _mixer(ys, ada_s, lw, state,
                          lambda u, g, gt: _fox_sample(u, gt, page_table, ck, cv, clf_t, bs, ss), chunk_s, F32)
        res_s.append(outs)

    stack = lambda res, i: jnp.stack([r[i] for r in res])
    return (yp, ys) + tuple(stack(res_p, i) for i in range(8)) + tuple(stack(res_s, i) for i in range(8))
```

```python
import functools

import jax
import jax.numpy as jnp
from jax import lax
from jax.experimental import pallas as pl
from jax.experimental.pallas import tpu as pltpu

F32 = jnp.float32
BF16 = jnp.bfloat16

NH_M, DQK_M, DV_M = 4, 64, 128
W_M = NH_M * DV_M
GATE_CAP = 15.0
NH_S, P_S, N_GROUPS, D_STATE, CONV_W = 8, 64, 2, 128, 4
D_INNER = NH_S * P_S
CONV_CH = D_INNER + 2 * N_GROUPS * D_STATE
NH_F, DH_F = 8, 64
W_F = NH_F * DH_F
CHUNK = 64
SCAN_CHUNK = 256
EPS = 1e-6
LANES = 128
SUBLANES = 8

LN_LOGI, LN_LOGF, LN_DT, LN_FF, LN_BM, LN_CUM, LN_FT = 0, 4, 8, 16, 24, 32, 40

VMEM_LIMIT = 56 * 1024 * 1024


def _cparams(sem):
    return pltpu.CompilerParams(dimension_semantics=sem, vmem_limit_bytes=VMEM_LIMIT)


def _sigmoid(x):
    return 1.0 / (1.0 + jnp.exp(-x))


def _softplus(x):
    return jnp.maximum(x, 0.0) + jnp.log(1.0 + jnp.exp(-jnp.abs(x)))


def _rms(x, g):
    return x * lax.rsqrt(jnp.mean(x * x, axis=-1, keepdims=True) + EPS) * g


def _dot(a, b):
    return jnp.dot(a, b, preferred_element_type=F32)


def _dot_nt(a, b):
    return lax.dot_general(a, b, (((1,), (1,)), ((), ())), preferred_element_type=F32)


def _dot_tn(a, b):
    return lax.dot_general(a, b, (((0,), (0,)), ((), ())), preferred_element_type=F32)


def _const_spec(shape):
    nd = len(shape)
    return pl.BlockSpec(shape, lambda *_: (0,) * nd)


def _ada_kernel(c_ref, w_ref, b_ref, o_ref):
    c = c_ref[...]
    s = (c * _sigmoid(c)).astype(BF16)
    o_ref[0] = _dot(s, w_ref[0].astype(BF16)) + b_ref[0]


def _ada(c_all, w_ada, b_ada):
    depth, d, n = w_ada.shape
    rows = c_all.shape[0]
    tn = 512
    return pl.pallas_call(
        _ada_kernel,
        grid=(depth, n // tn),
        in_specs=[pl.BlockSpec((rows, d), lambda l, j: (0, 0)),
                  pl.BlockSpec((1, d, tn), lambda l, j: (l, 0, j)),
                  pl.BlockSpec((1, 1, tn), lambda l, j: (l, 0, j))],
        out_specs=pl.BlockSpec((1, rows, tn), lambda l, j: (l, 0, j)),
        out_shape=jax.ShapeDtypeStruct((depth, rows, n), F32),
        compiler_params=_cparams(("arbitrary", "arbitrary")),
        name="ada",
    )(c_all, w_ada, b_ada.reshape(depth, 1, n))


SEGS = (("mq", NH_M * DQK_M, DQK_M ** -0.5, F32), ("mk", NH_M * DQK_M, 1.0, F32), ("mv", W_M, 1.0, F32),
        ("mo", W_M, 1.0, F32), ("sz", D_INNER, 1.0, F32), ("sxbc", CONV_CH, 1.0, F32),
        ("fq", W_F, DH_F ** -0.5, BF16), ("small", LANES, 1.0, F32))
W_PROJ = sum(sg[1] for sg in SEGS)


def _inproj_kernel(x_ref, sh_ref, sc_ref, g_ref, w_ref, wkv_ref, *outs, kv_t):
    x = x_ref[...]
    tb, ts, d = x.shape
    h = _rms(x, g_ref[...]) * (1.0 + sc_ref[...]) + sh_ref[...]
    hb = h.reshape(tb * ts, d).astype(BF16)
    off = 0
    for (_, wd, scale, _), o in zip(SEGS, outs):
        r = _dot(hb, w_ref[:, off:off + wd])
        if scale != 1.0:
            r = r * scale
        o[...] = r.astype(o.dtype)
        off += wd
    small_ref, smallt_ref, fk_ref, fv_ref = outs[len(SEGS) - 1:]
    smallt_ref[...] = small_ref[...].T
    if kv_t:
        kv = _dot_nt(wkv_ref[...], hb)
        fk_ref[0] = kv[:W_F]
        fv_ref[0] = kv[W_F:]
    else:
        kv = _dot(hb, wkv_ref[...])
        fk_ref[...] = kv[:, :W_F]
        fv_ref[...] = kv[:, W_F:]


def _x_tiles(b, s, ts_max):
    if s >= ts_max:
        return 1, ts_max
    tb = max(1, min(b, ts_max // s))
    return tb, s


def _ada_spec(tb, d, col):
    return pl.BlockSpec((tb, 1, d), lambda i, j: (i, 0, col))


def _inproj(x, ada, g_pre, w, wkv, kv_t):
    b, s, d = x.shape
    tb, ts = _x_tiles(b, s, 256)
    tm = tb * ts
    nj = s // ts
    m = b * s
    row = lambda i, j: (i * nj + j, 0)
    out_shape = [jax.ShapeDtypeStruct((m, wd), dt) for _, wd, _, dt in SEGS]
    out_specs = [pl.BlockSpec((tm, wd), row) for _, wd, _, _ in SEGS]
    out_shape.append(jax.ShapeDtypeStruct((LANES, m), F32))
    out_specs.append(pl.BlockSpec((LANES, tm), lambda i, j: (0, i * nj + j)))
    if kv_t:
        assert tb == 1
        out_shape += [jax.ShapeDtypeStruct((b, W_F, s), F32)] * 2
        out_specs += [pl.BlockSpec((1, W_F, ts), lambda i, j: (i, 0, j))] * 2
    else:
        out_shape += [jax.ShapeDtypeStruct((m, W_F), F32)] * 2
        out_specs += [pl.BlockSpec((tm, W_F), row)] * 2
    outs = pl.pallas_call(
        functools.partial(_inproj_kernel, kv_t=kv_t),
        grid=(b // tb, nj),
        in_specs=[pl.BlockSpec((tb, ts, d), lambda i, j: (i, j, 0)),
                  _ada_spec(tb, d, 0), _ada_spec(tb, d, 1),
                  _const_spec((1, d)),
                  pl.BlockSpec((d, W_PROJ), lambda i, j: (0, 0), pipeline_mode=pl.Buffered(1)),
                  pl.BlockSpec(wkv.shape, lambda i, j: (0, 0), pipeline_mode=pl.Buffered(1))],
        out_specs=out_specs,
        out_shape=out_shape,
        compiler_params=_cparams(("arbitrary", "arbitrary")),
        name="inproj",
    )(x, ada, ada, g_pre.reshape(1, d), w, wkv)
    names = [sg[0] for sg in SEGS] + ["small_t", "fk", "fv"]
    return dict(zip(names, outs))


def _prep_kernel(small_ref, bias_ref, alog_ref, g_ref, gt_ref, carry_ref, *, chunk, seg, tiles_per_seq):
    i = pl.program_id(0)

    @pl.when(i % tiles_per_seq == 0)
    def _():
        carry_ref[...] = jnp.zeros_like(carry_ref)

    v = small_ref[...] + bias_ref[...]
    ts = v.shape[0]
    lane = lax.broadcasted_iota(jnp.int32, v.shape, 1)
    row = lax.broadcasted_iota(jnp.int32, v.shape, 0)
    capped = GATE_CAP * jnp.tanh(v * (1.0 / GATE_CAP))
    is_m = (lane < LN_DT) | ((lane >= LN_BM) & (lane < LN_CUM))
    vv = jnp.where(is_m, capped, v)
    sp = _softplus(vv)
    lsig = -_softplus(-vv)
    a = -jnp.exp(alog_ref[...])
    val = jnp.where(lane < LN_LOGF, vv,
          jnp.where(lane < LN_DT, lsig,
          jnp.where(lane < LN_FF, sp,
          jnp.where(lane < LN_CUM, lsig,
          jnp.where(lane < LN_FT, sp * a, lsig)))))
    val = jnp.where(lane < LN_FT + NH_F, val, 0.0)
    ridx = jnp.where(lane < LN_FT, row % chunk, row % seg)
    ridx = jnp.where(lane >= LN_BM, ridx, -1)
    x = val
    k = 1
    while k < min(ts, max(chunk, seg)):
        x = x + jnp.where(ridx >= k, pltpu.roll(x, k, 0), 0.0)
        k *= 2
    x = x + jnp.where(lane >= LN_FT, carry_ref[...], 0.0)
    carry_ref[...] = x[ts - 1:ts, :]
    g_ref[...] = x
    gt_ref[...] = x.T


def _prep(small, bias_lanes, alog_lanes, s, chunk):
    m = small.shape[0]
    ts = min(512, m)
    if s >= ts:
        seg, tiles_per_seq = ts, s // ts
    else:
        seg, tiles_per_seq = s, 1
    return pl.pallas_call(
        functools.partial(_prep_kernel, chunk=chunk, seg=seg, tiles_per_seq=tiles_per_seq),
        grid=(m // ts,),
        in_specs=[pl.BlockSpec((ts, LANES), lambda i: (i, 0)), _const_spec((1, LANES)), _const_spec((1, LANES))],
        out_specs=[pl.BlockSpec((ts, LANES), lambda i: (i, 0)), pl.BlockSpec((LANES, ts), lambda i: (0, i))],
        out_shape=[jax.ShapeDtypeStruct((m, LANES), F32), jax.ShapeDtypeStruct((LANES, m), F32)],
        scratch_shapes=[pltpu.VMEM((1, LANES), F32)],
        compiler_params=_cparams(("arbitrary",)),
        name="prep",
    )(small, bias_lanes, alog_lanes)


def _mlstm_kernel(q_ref, k_ref, v_ref, mo_ref, g_ref, gt_ref, gh_ref, c0_ref, n0_ref, m0_ref,
                  y_ref, c_ref, n_ref, m_ref, *, chunk, gt3d, mm):
    j = pl.program_id(1)

    @pl.when(j == 0)
    def _():
        c_ref[...] = c0_ref[...]
        n_ref[...] = n0_ref[...]
        m_ref[...] = m0_ref[...]

    ts = q_ref.shape[0]
    ln = chunk
    gt = gt_ref[0] if gt3d else gt_ref[...]
    rr = lax.broadcasted_iota(jnp.int32, (ln, ln), 0)
    cc = lax.broadcasted_iota(jnp.int32, (ln, ln), 1)
    causal = cc <= rr
    for c in range(ts // ln):
        lo, hi = c * ln, (c + 1) * ln
        g = g_ref[lo:hi, :]
        for h in range(NH_M):
            qf = q_ref[lo:hi, h * DQK_M:(h + 1) * DQK_M]
            kf = k_ref[lo:hi, h * DQK_M:(h + 1) * DQK_M]
            vf = v_ref[lo:hi, h * DV_M:(h + 1) * DV_M]
            qb, kb = qf.astype(mm), kf.astype(mm)
            logi_r = gt[LN_LOGI + h:LN_LOGI + h + 1, lo:hi]
            logi_c = g[:, LN_LOGI + h:LN_LOGI + h + 1]
            b_r = gt[LN_BM + h:LN_BM + h + 1, lo:hi]
            b_c = g[:, LN_BM + h:LN_BM + h + 1]
            m_st = m_ref[0, h][:, 0:1]
            ct = c_ref[0, h]
            nv = n_ref[0, h]
            dm = jnp.where(causal, b_c - b_r + logi_r, -jnp.inf)
            inter = b_c + m_st
            m_t = jnp.maximum(inter, jnp.max(dm, axis=1, keepdims=True))
            s = _dot_nt(qb, kb) * jnp.exp(dm - m_t)
            a = jnp.exp(inter - m_t)
            num = _dot(s.astype(mm), vf.astype(mm)) + a * _dot(qb, ct.astype(mm))
            den = jnp.sum(s, axis=1, keepdims=True) + a * jnp.sum(qf * nv, axis=1, keepdims=True)
            hv = num / jnp.maximum(jnp.abs(den), jnp.exp(-m_t))
            m_new = m_t[ln - 1:ln, :]
            b_last = b_c[ln - 1:ln, :]
            decay = jnp.exp(b_last + m_st - m_new)
            wk = jnp.exp(b_last - b_c + logi_c - m_new)
            c_ref[0, h] = decay * ct + _dot_tn(kb, (wk * vf).astype(mm))
            n_ref[0, h] = decay * nv + jnp.sum(wk * kf, axis=0, keepdims=True)
            m_ref[0, h] = jnp.broadcast_to(m_new, (1, LANES))
            hm = _rms(hv, gh_ref[:, h * DV_M:(h + 1) * DV_M])
            y = hm * _sigmoid(mo_ref[lo:hi, h * DV_M:(h + 1) * DV_M])
            y_ref[lo:hi, h * DV_M:(h + 1) * DV_M] = y.astype(y_ref.dtype)


def _gt_spec(b, s, ts, nj, gt3d):
    if gt3d:
        return pl.BlockSpec((1, LANES, s), lambda i, j: (i, 0, 0))
    return pl.BlockSpec((LANES, ts), lambda i, j: (0, i * nj + j))


def _mlstm(u, g, gt, g_mhead, c0t, n0, m0, b, s, chunk, mm):
    ts = min(256, s)
    nj = s // ts
    m = b * s
    gt3d = gt.ndim == 3
    row = lambda i, j: (i * nj + j, 0)
    st = lambda *shape: pl.BlockSpec((1,) + shape, lambda i, j: (i,) + (0,) * len(shape))
    return pl.pallas_call(
        functools.partial(_mlstm_kernel, chunk=chunk, gt3d=gt3d, mm=mm),
        grid=(b, nj),
        in_specs=[pl.BlockSpec((ts, NH_M * DQK_M), row), pl.BlockSpec((ts, NH_M * DQK_M), row),
                  pl.BlockSpec((ts, W_M), row), pl.BlockSpec((ts, W_M), row),
                  pl.BlockSpec((ts, LANES), row), _gt_spec(b, s, ts, nj, gt3d),
                  _const_spec((1, W_M)),
                  st(NH_M, DQK_M, DV_M), st(NH_M, 1, DQK_M), st(NH_M, 1, LANES)],
        out_specs=[pl.BlockSpec((ts, W_M), row),
                   st(NH_M, DQK_M, DV_M), st(NH_M, 1, DQK_M), st(NH_M, 1, LANES)],
        out_shape=[jax.ShapeDtypeStruct((m, W_M), BF16),
                   jax.ShapeDtypeStruct((b, NH_M, DQK_M, DV_M), F32),
                   jax.ShapeDtypeStruct((b, NH_M, 1, DQK_M), F32),
                   jax.ShapeDtypeStruct((b, NH_M, 1, LANES), F32)],
        compiler_params=_cparams(("arbitrary", "arbitrary")),
        name="mlstm",
    )(u["mq"], u["mk"], u["mv"], u["mo"], g, gt, g_mhead.reshape(1, W_M), c0t, n0, m0)


def _ssd_kernel(xbc_ref, sz_ref, g_ref, gt_ref, cw_ref, cb_ref, dsk_ref, gs_ref, conv0_ref, h0_ref,
                y_ref, hst_ref, tail_ref, *, chunk, gt3d, mm):
    j = pl.program_id(1)

    @pl.when(j == 0)
    def _():
        hst_ref[...] = h0_ref[...]
        tail_ref[...] = conv0_ref[0]

    x = xbc_ref[...]
    ts = x.shape[0]
    ln = chunk
    prev = tail_ref[...]
    row8 = lax.broadcasted_iota(jnp.int32, prev.shape, 0)
    acc = cb_ref[...] + cw_ref[CONV_W - 1:CONV_W, :] * x
    for k in range(1, CONV_W):
        rolled = pltpu.roll(x, k, 0)
        first = jnp.where(row8 < k, pltpu.roll(prev, k, 0), rolled[:SUBLANES])
        shifted = first if ts == SUBLANES else jnp.concatenate([first, rolled[SUBLANES:]], axis=0)
        acc = acc + cw_ref[CONV_W - 1 - k:CONV_W - k, :] * shifted
    tail_ref[...] = x[ts - SUBLANES:, :]
    xa = acc * _sigmoid(acc)

    gt = gt_ref[0] if gt3d else gt_ref[...]
    rr = lax.broadcasted_iota(jnp.int32, (ln, ln), 0)
    cc = lax.broadcasted_iota(jnp.int32, (ln, ln), 1)
    causal = cc <= rr
    hpg = NH_S // N_GROUPS
    for c in range(ts // ln):
        lo, hi = c * ln, (c + 1) * ln
        g = g_ref[lo:hi, :]
        ys = []
        for grp in range(N_GROUPS):
            bo = D_INNER + grp * D_STATE
            co = D_INNER + N_GROUPS * D_STATE + grp * D_STATE
            bmg = xa[lo:hi, bo:bo + D_STATE].astype(mm)
            cmg = xa[lo:hi, co:co + D_STATE].astype(mm)
            cbm = _dot_nt(cmg, bmg)
            for hh in range(hpg):
                h = grp * hpg + hh
                cum_c = g[:, LN_CUM + h:LN_CUM + h + 1]
                cum_r = gt[LN_CUM + h:LN_CUM + h + 1, lo:hi]
                dt_c = g[:, LN_DT + h:LN_DT + h + 1]
                xh = xa[lo:hi, h * P_S:(h + 1) * P_S]
                xdt = xh * dt_c
                ldec = jnp.exp(jnp.where(causal, cum_c - cum_r, -jnp.inf))
                hst = hst_ref[0, h]
                y = _dot((cbm * ldec).astype(mm), xdt.astype(mm))
                y = y + jnp.exp(cum_c) * _dot_nt(cmg, hst.astype(mm))
                c_last = cum_c[ln - 1:ln, :]
                wk = jnp.exp(c_last - cum_c)
                hst_ref[0, h] = jnp.exp(c_last) * hst + _dot_tn((wk * xdt).astype(mm), bmg)
                ys.append(y + dsk_ref[:, h * P_S:(h + 1) * P_S] * xh)
        yy = jnp.concatenate(ys, axis=1)
        z = sz_ref[lo:hi, :]
        y_ref[lo:hi, :] = _rms(yy * (z * _sigmoid(z)), gs_ref[...]).astype(y_ref.dtype)


def _ssd(u, g, gt, conv_w, conv_b, d_skip, g_ssm, conv0, h0, b, s, chunk, mm):
    ts = min(256, s)
    nj = s // ts
    m = b * s
    gt3d = gt.ndim == 3
    row = lambda i, j: (i * nj + j, 0)
    st = lambda *shape: pl.BlockSpec((1,) + shape, lambda i, j: (i,) + (0,) * len(shape))
    return pl.pallas_call(
        functools.partial(_ssd_kernel, chunk=chunk, gt3d=gt3d, mm=mm),
        grid=(b, nj),
        in_specs=[pl.BlockSpec((ts, CONV_CH), row), pl.BlockSpec((ts, D_INNER), row),
                  pl.BlockSpec((ts, LANES), row), _gt_spec(b, s, ts, nj, gt3d),
                  _const_spec((CONV_W, CONV_CH)), _const_spec((1, CONV_CH)),
                  _const_spec((1, D_INNER)), _const_spec((1, D_INNER)),
                  st(SUBLANES, CONV_CH), st(NH_S, P_S, D_STATE)],
        out_specs=[pl.BlockSpec((ts, D_INNER), row), st(NH_S, P_S, D_STATE)],
        out_shape=[jax.ShapeDtypeStruct((m, D_INNER), BF16),
                   jax.ShapeDtypeStruct((b, NH_S, P_S, D_STATE), F32)],
        scratch_shapes=[pltpu.VMEM((SUBLANES, CONV_CH), F32)],
        compiler_params=_cparams(("arbitrary", "arbitrary")),
        name="ssd",
    )(u["sxbc"], u["sz"], g, gt, conv_w, conv_b.reshape(1, CONV_CH),
      jnp.repeat(d_skip, P_S).reshape(1, D_INNER), g_ssm.reshape(1, D_INNER), conv0, h0)


def _split3(x):
    hi = x.astype(BF16).astype(F32)
    r = x - hi
    mid = r.astype(BF16).astype(F32)
    return hi, mid, r - mid


FOX_EXT = 16
FOX_VROWS = 2 * DH_F + FOX_EXT


def _foxp_kernel(q_ref, kt_ref, vt_ref, g_ref, ftr_ref, o_ref, kaug_ref, vaug_ref, *, tq):
    hp = pl.program_id(1)
    i = pl.program_id(2)
    hw = 2 * DH_F
    s_len = kaug_ref.shape[1]

    @pl.when(i == 0)
    def _():
        kaug_ref[0:hw, :] = kt_ref[0].astype(BF16)
        vaug_ref[0:hw, :] = vt_ref[0].astype(BF16)
        r = lax.broadcasted_iota(jnp.int32, (FOX_EXT, s_len), 0)
        ext = jnp.where(r < 3, 1.0, 0.0)
        for hh in range(2):
            parts = _split3(ftr_ref[pl.ds(hp * 2 + hh, 1), :])
            for pi, part in enumerate(parts):
                ext = jnp.where(r == 3 + 3 * hh + pi, -part, ext)
        kaug_ref[hw:hw + FOX_EXT, :] = ext.astype(BF16)
        kaug_ref[hw + FOX_EXT:, :] = jnp.zeros((kaug_ref.shape[0] - hw - FOX_EXT, s_len), BF16)
        vaug_ref[hw:, :] = jnp.where(r == 0, 1.0, 0.0).astype(BF16)

    g = g_ref[...]
    q = q_ref[...]
    lane = lax.broadcasted_iota(jnp.int32, (tq, LANES), 1)
    low = lane < DH_F
    qaug = []
    for hh in range(2):
        ft_c = jnp.sum(jnp.where(lane == LN_FT + hp * 2 + hh, g, 0.0), axis=1, keepdims=True)
        hi, mid, lo = _split3(ft_c)
        ext = jnp.where(lane == 0, hi, jnp.where(lane == 1, mid, jnp.where(lane == 2, lo, 0.0)))
        ext = jnp.where((lane >= 3 + 3 * hh) & (lane < 6 + 3 * hh), 1.0, ext)
        qm = jnp.where(low if hh == 0 else ~low, q, jnp.zeros_like(q))
        qaug.append(jnp.concatenate([qm, ext.astype(BF16)], axis=1))

    rr = lax.broadcasted_iota(jnp.int32, (tq, tq), 0)
    cc = lax.broadcasted_iota(jnp.int32, (tq, tq), 1)
    causal = cc <= rr

    def block(carry, start, width, diag):
        ks = pl.multiple_of(start, width)
        kb = kaug_ref[:, pl.ds(ks, width)]
        vb = vaug_ref[:, pl.ds(ks, width)]
        ms, ls, acc = carry
        new_m, new_l, alphas, pvs = [], [], [], []
        for hh in range(2):
            s = _dot(qaug[hh], kb)
            if diag:
                s = jnp.where(causal, s, -jnp.inf)
            m_new = jnp.maximum(ms[hh], jnp.max(s, axis=1, keepdims=True))
            p = jnp.exp(s - m_new)
            alpha = jnp.exp(ms[hh] - m_new)
            pv = _dot_nt(p.astype(BF16), vb)
            new_m.append(m_new)
            new_l.append(alpha * ls[hh] + pv[:, hw:hw + 1])
            alphas.append(alpha)
            pvs.append(pv[:, :hw])
        acc = jnp.where(low, alphas[0], alphas[1]) * acc + jnp.where(low, pvs[0], pvs[1])
        return tuple(new_m), tuple(new_l), acc

    neg = jnp.full((tq, 1), -jnp.inf, F32)
    zero = jnp.zeros((tq, 1), F32)
    init = ((neg, neg), (zero, zero), jnp.zeros((tq, hw), F32))
    carry = lax.fori_loop(0, i // 2, lambda jb, c: block(c, jb * (2 * tq), 2 * tq, False), init)
    carry = lax.cond(i % 2 == 1, lambda c: block(c, (i - 1) * tq, tq, False), lambda c: c, carry)
    _, ls, acc = block(carry, i * tq, tq, True)
    o_ref[...] = (acc / jnp.where(low, ls[0], ls[1])).astype(o_ref.dtype)


def _fox_prompt(u, g, gt, b, s):
    tq = min(512, s)
    nq = s // tq
    m = b * s
    hw = 2 * DH_F
    return pl.pallas_call(
        functools.partial(_foxp_kernel, tq=tq),
        grid=(b, NH_F // 2, nq),
        in_specs=[pl.BlockSpec((tq, hw), lambda bb, hp, i: (bb * nq + i, hp)),
                  pl.BlockSpec((1, hw, s), lambda bb, hp, i: (bb, hp, 0)),
                  pl.BlockSpec((1, hw, s), lambda bb, hp, i: (bb, hp, 0)),
                  pl.BlockSpec((tq, LANES), lambda bb, hp, i: (bb * nq + i, 0)),
                  pl.BlockSpec((SUBLANES, s), lambda bb, hp, i: (LN_FT // SUBLANES, bb))],
        out_specs=pl.BlockSpec((tq, hw), lambda bb, hp, i: (bb * nq + i, hp)),
        out_shape=jax.ShapeDtypeStruct((m, W_F), BF16),
        scratch_shapes=[pltpu.VMEM((2 * hw, s), BF16), pltpu.VMEM((FOX_VROWS, s), BF16)],
        compiler_params=_cparams(("arbitrary", "arbitrary", "arbitrary")),
        name="fox_prompt",
    )(u["fq"], u["fk"], u["fv"], g, gt)


def _foxs_kernel(pt_ref, q_ref, kn_ref, vn_ref, gt_ref, *rest, pages):
    del pt_ref
    k_refs, v_refs, lf_refs = rest[:pages], rest[pages:2 * pages], rest[2 * pages:3 * pages]
    o_ref, qbd_ref, m_ref, l_ref, acc_ref, carry_ref, kcat_ref, vcat_ref = rest[3 * pages:]
    gi = pl.program_id(1)
    t = q_ref.shape[0]
    ht = NH_F * t
    page = k_refs[0].shape[2]

    def expand_heads(x):
        return jnp.broadcast_to(x[:, None, :], (NH_F, t, x.shape[1])).reshape(ht, x.shape[1])

    @pl.when(gi == 0)
    def _():
        q = q_ref[...].astype(F32)
        qt = jnp.broadcast_to(q[None], (NH_F, t, W_F)).reshape(ht, W_F)
        rh = lax.broadcasted_iota(jnp.int32, (ht, W_F), 0) // t
        lh = lax.broadcasted_iota(jnp.int32, (ht, W_F), 1) // DH_F
        qbd_ref[...] = jnp.where(rh == lh, qt, 0.0)
        m_ref[...] = jnp.full_like(m_ref, -jnp.inf)
        l_ref[...] = jnp.zeros_like(l_ref)
        acc_ref[...] = jnp.zeros_like(acc_ref)
        carry_ref[...] = jnp.zeros_like(carry_ref)

    cn_rows = expand_heads(gt_ref[0][LN_FT:LN_FT + NH_F, :])
    tq_idx = lax.broadcasted_iota(jnp.int32, (ht, t), 0) % t
    tk_idx = lax.broadcasted_iota(jnp.int32, (ht, t), 1)
    cn_col = jnp.sum(jnp.where(tq_idx == tk_idx, cn_rows, 0.0), axis=1, keepdims=True)

    qbd = qbd_ref[...]
    qbd_b = qbd.astype(BF16)

    def update(s, pv):
        m = m_ref[...]
        m_new = jnp.maximum(m, jnp.max(s, axis=1, keepdims=True))
        p = jnp.exp(s - m_new)
        alpha = jnp.exp(m - m_new)
        l_ref[...] = alpha * l_ref[...] + jnp.sum(p, axis=1, keepdims=True)
        acc_ref[...] = alpha * acc_ref[...] + pv(p)
        m_ref[...] = m_new

    lf_all = jnp.concatenate([lf_refs[pi][0] for pi in range(pages)], axis=0)
    tri = jnp.where(lax.broadcasted_iota(jnp.int32, (page, page), 0)
                    <= lax.broadcasted_iota(jnp.int32, (page, page), 1), 1.0, 0.0).astype(BF16)
    pre_all = sum(_dot(part.astype(BF16), tri) for part in _split3(lf_all))
    carry = carry_ref[...]
    rs = [None] * pages
    for pi in reversed(range(pages)):
        pre = pre_all[pi * NH_F:(pi + 1) * NH_F, :]
        tot = pre[:, page - 1:page]
        rs[pi] = expand_heads(carry + tot - pre)
        carry = carry + tot
        kcat_ref[:, pi * page:(pi + 1) * page] = k_refs[pi][0].astype(BF16)
        vcat_ref[:, pi * page:(pi + 1) * page] = v_refs[pi][0].astype(BF16)
    carry_ref[...] = carry
    s_all = _dot(qbd_b, kcat_ref[...]) + jnp.concatenate(rs, axis=1) + cn_col
    update(s_all, lambda p: _dot_nt(p.astype(BF16), vcat_ref[...]))

    @pl.when(gi == pl.num_programs(1) - 1)
    def _():
        s_new = _dot_nt(qbd, kn_ref[...]) + (cn_col - cn_rows)
        s_new = jnp.where(tk_idx <= tq_idx, s_new, -jnp.inf)
        update(s_new, lambda p: _dot(p, vn_ref[...]))
        o = acc_ref[...] / l_ref[...]
        lh = lax.broadcasted_iota(jnp.int32, (t, W_F), 1) // DH_F
        y = jnp.zeros((t, W_F), F32)
        for h in range(NH_F):
            y = y + jnp.where(lh == h, o[h * t:(h + 1) * t, :], 0.0)
        o_ref[...] = y.astype(o_ref.dtype)


def _fox_sample(u, gt3, page_table, ck_t, cv_t, clf_t, page_base, b, t):
    n_pages = page_table.shape[1]
    page = ck_t.shape[2]
    pages = 8
    while n_pages % pages:
        pages //= 2
    ng = n_pages // pages
    ht = NH_F * t

    def page_spec(width_shape, pi):
        return pl.BlockSpec((1,) + width_shape,
                            lambda bb, gi, pt: (page_base + pt[bb, (ng - 1 - gi) * pages + pi], 0, 0))

    tok = pl.BlockSpec((t, W_F), lambda bb, gi, pt: (bb, 0))
    in_specs = [tok, tok, tok, pl.BlockSpec((1, LANES, t), lambda bb, gi, pt: (bb, 0, 0))]
    in_specs += [page_spec((W_F, page), pi) for pi in range(pages)]
    in_specs += [page_spec((W_F, page), pi) for pi in range(pages)]
    in_specs += [page_spec((NH_F, page), pi) for pi in range(pages)]
    grid_spec = pltpu.PrefetchScalarGridSpec(
        num_scalar_prefetch=1,
        grid=(b, ng),
        in_specs=in_specs,
        out_specs=pl.BlockSpec((t, W_F), lambda bb, gi, pt: (bb, 0)),
        scratch_shapes=[pltpu.VMEM((ht, W_F), F32), pltpu.VMEM((ht, 1), F32), pltpu.VMEM((ht, 1), F32),
                        pltpu.VMEM((ht, W_F), F32), pltpu.VMEM((NH_F, 1), F32),
                        pltpu.VMEM((W_F, pages * page), BF16), pltpu.VMEM((W_F, pages * page), BF16)],
    )
    return pl.pallas_call(
        functools.partial(_foxs_kernel, pages=pages),
        grid_spec=grid_spec,
        out_shape=jax.ShapeDtypeStruct((b * t, W_F), BF16),
        compiler_params=_cparams(("arbitrary", "arbitrary")),
        name="fox_sample",
    )(page_table, u["fq"], u["fk"], u["fv"], gt3, *([ck_t] * pages), *([cv_t] * pages), *([clf_t] * pages))


def _merge_kernel(x_ref, sh_ref, sc_ref, gt_ref, gpre_ref, gpost_ref, ym_ref, ys_ref, yf_ref,
                  wg_ref, wm_ref, ws_ref, wf_ref, wo_ref, o_ref):
    x = x_ref[...]
    tb, ts, d = x.shape
    h = _rms(x, gpre_ref[...]) * (1.0 + sc_ref[...]) + sh_ref[...]
    hb = h.reshape(tb * ts, d).astype(BF16)
    merged = None
    for bi, (y_ref, w_ref) in enumerate(((ym_ref, wm_ref), (ys_ref, ws_ref), (yf_ref, wf_ref))):
        gate = _sigmoid(_dot(hb, wg_ref[:, bi * d:(bi + 1) * d]))
        term = gate * _dot(y_ref[...], w_ref[...])
        merged = term if merged is None else merged + term
    out = _dot(merged.astype(BF16), wo_ref[...]).reshape(tb, ts, d)
    o_ref[...] = x + gt_ref[...] * _rms(out, gpost_ref[...])


def _merge(x, ada, g_pre, g_post, ym, ys, yf, wg, wm, ws, wf, wo):
    b, s, d = x.shape
    tb, ts = _x_tiles(b, s, 256)
    tm = tb * ts
    nj = s // ts
    row = lambda i, j: (i * nj + j, 0)
    wspec = lambda shape: pl.BlockSpec(shape, lambda i, j: (0, 0), pipeline_mode=pl.Buffered(1))
    return pl.pallas_call(
        _merge_kernel,
        grid=(b // tb, nj),
        in_specs=[pl.BlockSpec((tb, ts, d), lambda i, j: (i, j, 0)),
                  _ada_spec(tb, d, 0), _ada_spec(tb, d, 1), _ada_spec(tb, d, 2),
                  _const_spec((1, d)), _const_spec((1, d)),
                  pl.BlockSpec((tm, W_M), row), pl.BlockSpec((tm, D_INNER), row), pl.BlockSpec((tm, W_F), row),
                  wspec((d, 3 * d)), wspec((W_M, d)), wspec((D_INNER, d)), wspec((W_F, d)), wspec((d, d))],
        out_specs=pl.BlockSpec((tb, ts, d), lambda i, j: (i, j, 0)),
        out_shape=jax.ShapeDtypeStruct((b, s, d), F32),
        compiler_params=_cparams(("arbitrary", "arbitrary")),
        name="merge",
    )(x, ada, ada, ada, g_pre.reshape(1, d), g_post.reshape(1, d), ym, ys, yf, wg, wm, ws, wf, wo)


def _mlp_kernel(x_ref, sh_ref, sc_ref, gt_ref, gpre_ref, gpost_ref, wu_ref, wd_ref, o_ref):
    x = x_ref[...]
    tb, ts, d = x.shape
    h = _rms(x, gpre_ref[...]) * (1.0 + sc_ref[...]) + sh_ref[...]
    hb = h.reshape(tb * ts, d).astype(BF16)
    up = jnp.maximum(_dot(hb, wu_ref[...]), 0.0)
    f = _dot((up * up).astype(BF16), wd_ref[...]).reshape(tb, ts, d)
    o_ref[...] = x + gt_ref[...] * _rms(f, gpost_ref[...])


def _mlp(x, ada, g_pre, g_post, wu, wd):
    b, s, d = x.shape
    tb, ts = _x_tiles(b, s, 256)
    nj = s // ts
    dff = wu.shape[1]
    wspec = lambda shape: pl.BlockSpec(shape, lambda i, j: (0, 0), pipeline_mode=pl.Buffered(1))
    return pl.pallas_call(
        _mlp_kernel,
        grid=(b // tb, nj),
        in_specs=[pl.BlockSpec((tb, ts, d), lambda i, j: (i, j, 0)),
                  _ada_spec(tb, d, 3), _ada_spec(tb, d, 4), _ada_spec(tb, d, 5),
                  _const_spec((1, d)), _const_spec((1, d)),
                  wspec((d, dff)), wspec((dff, d))],
        out_specs=pl.BlockSpec((tb, ts, d), lambda i, j: (i, j, 0)),
        out_shape=jax.ShapeDtypeStruct((b, s, d), F32),
        compiler_params=_cparams(("arbitrary", "arbitrary")),
        name="mlp",
    )(x, ada, ada, ada, g_pre.reshape(1, d), g_post.reshape(1, d), wu, wd)


def _split_w_in(w):
    sizes = (NH_M * DQK_M, NH_M * DQK_M, W_M, NH_M, NH_M, W_M, D_INNER, CONV_CH, NH_S, W_F, W_F, W_F, NH_F)
    names = ("mq", "mk", "mv", "mi", "mf", "mo", "sz", "sxbc", "sdt", "fq", "fk", "fv", "ff")
    cols, off = {}, 0
    for nm, sz in zip(names, sizes):
        cols[nm] = w[:, off:off + sz]
        off += sz
    gates = w[:, off:]
    d = w.shape[0]
    zeros = lambda n: jnp.zeros((d, n), w.dtype)
    small = jnp.concatenate(
        [cols["mi"], cols["mf"], cols["sdt"], cols["ff"], cols["mf"], zeros(LN_CUM - LN_BM - NH_M),
         cols["sdt"], cols["ff"], zeros(LANES - LN_FT - NH_F)], axis=1)
    proj = jnp.concatenate([cols[sg[0]] for sg in SEGS[:-1]] + [small], axis=1)
    kv = jnp.concatenate([cols["fk"], cols["fv"]], axis=1)
    return proj.astype(BF16), kv.astype(BF16), gates.astype(BF16)


def _gate_lanes(b_mgate, dt_bias, b_ffox, a_log):
    z = lambda n: jnp.zeros((n,), F32)
    bias = jnp.concatenate([b_mgate, dt_bias, b_ffox, b_mgate[NH_M:], z(LN_CUM - LN_BM - NH_M),
                            dt_bias, b_ffox, z(LANES - LN_FT - NH_F)])
    alog = jnp.concatenate([z(LN_CUM), a_log, z(LANES - LN_CUM - NH_S)])
    return bias.reshape(1, LANES), alog.reshape(1, LANES)


def _mixer(x, ada, lw, state, attend, chunk, mm, kv_t):
    b, s, d = x.shape
    u = _inproj(x, ada, lw["g_pre_mix"], lw["w_proj"], lw["w_kv"].T if kv_t else lw["w_kv"], kv_t)
    g, gt = _prep(u["small"], lw["bias_lanes"], lw["alog_lanes"], s, chunk)
    if s % LANES:
        gt = gt.reshape(LANES, b, s).transpose(1, 0, 2)
    c0t, n0, m0, conv0, h0 = state
    ym, c_t, n_new, m_new = _mlstm(u, g, gt, lw["g_mhead"], c0t, n0, m0, b, s, chunk, mm)
    ys, h_new = _ssd(u, g, gt, lw["conv_w"], lw["conv_b"], lw["d_skip"], lw["g_ssm"], conv0, h0, b, s, chunk, mm)
    yf = attend(u, g, gt)
    x1 = _merge(x, ada, lw["g_pre_mix"], lw["g_post_mix"], ym, ys, yf,
                lw["w_gates"], lw["w_br_m"], lw["w_br_s"], lw["w_br_f"], lw["w_out"])
    x2 = _mlp(x1, ada, lw["g_pre_mlp"], lw["g_post_mlp"], lw["w_up"], lw["w_down"])
    if kv_t:
        rows = lambda a: jnp.transpose(a.reshape(b, NH_F, DH_F, s), (0, 3, 1, 2))
    else:
        rows = lambda a: a.reshape(b, s, NH_F, DH_F)
    outs = (rows(u["fk"]), rows(u["fv"]),
            g[:, LN_FF:LN_FF + NH_F].reshape(b, s, NH_F),
            jnp.swapaxes(c_t, -1, -2), n_new[:, :, 0, :], m_new[:, :, 0, 0],
            u["sxbc"].reshape(b, s, CONV_CH)[:, s - (CONV_W - 1):, :], h_new)
    return x2, outs


def kernel(x_prompt, x_sample, cache_k, cache_v, cache_logf, state_mlstm_C, state_mlstm_n, state_mlstm_m,
           state_conv, state_ssm, page_table, c_prompt, c_sample, w_ada, b_ada, g_pre_mix, g_post_mix, w_in,
           b_mgate, b_ffox, g_mhead, conv_w, conv_b, dt_bias, a_log, d_skip, g_ssm, w_br_m, w_br_s, w_br_f,
           w_out, g_pre_mlp, g_post_mlp, w_up, w_down):
    depth = w_in.shape[0]
    bp, sp, d = x_prompt.shape
    bs, ss, _ = x_sample.shape
    n_phys, page = cache_k.shape[1], cache_k.shape[2]

    pad = (-(bp + bs)) % SUBLANES
    c_all = jnp.concatenate([c_prompt, c_sample, jnp.zeros((pad, d), F32)], axis=0)
    ada_all = _ada(c_all, w_ada, b_ada)

    ck_t = jnp.transpose(cache_k, (0, 1, 3, 4, 2)).reshape(depth * n_phys, W_F, page)
    cv_t = jnp.transpose(cache_v, (0, 1, 3, 4, 2)).reshape(depth * n_phys, W_F, page)
    clf_t = jnp.transpose(cache_logf, (0, 1, 3, 2)).reshape(depth * n_phys, NH_F, page)

    chunk_p = SCAN_CHUNK if sp % SCAN_CHUNK == 0 else (CHUNK if sp % CHUNK == 0 else sp)
    chunk_s = SCAN_CHUNK if ss % SCAN_CHUNK == 0 else (CHUNK if ss % CHUNK == 0 else ss)
    yp, ys = x_prompt, x_sample
    res_p, res_s = [], []
    for l in range(depth):
        w_proj, w_kv, w_gates = _split_w_in(w_in[l])
        bias_lanes, alog_lanes = _gate_lanes(b_mgate[l], dt_bias[l], b_ffox[l], a_log[l])
        lw = dict(w_proj=w_proj, w_kv=w_kv, w_gates=w_gates, bias_lanes=bias_lanes, alog_lanes=alog_lanes,
                  g_pre_mix=g_pre_mix[l], g_post_mix=g_post_mix[l], g_mhead=g_mhead[l],
                  conv_w=conv_w[l], conv_b=conv_b[l], d_skip=d_skip[l], g_ssm=g_ssm[l],
                  w_br_m=w_br_m[l].astype(BF16), w_br_s=w_br_s[l].astype(BF16), w_br_f=w_br_f[l].astype(BF16),
                  w_out=w_out[l].astype(BF16), g_pre_mlp=g_pre_mlp[l], g_post_mlp=g_post_mlp[l],
                  w_up=w_up[l].astype(BF16), w_down=w_down[l].astype(BF16))
        ada_p = ada_all[l, :bp][:, None, :]
        ada_s = ada_all[l, bp:bp + bs][:, None, :]

        zero_state = (jnp.zeros((bp, NH_M, DQK_M, DV_M), F32), jnp.zeros((bp, NH_M, 1, DQK_M), F32),
                      jnp.zeros((bp, NH_M, 1, LANES), F32), jnp.zeros((bp, SUBLANES, CONV_CH), F32),
                      jnp.zeros((bp, NH_S, P_S, D_STATE), F32))
        yp, outs = _mixer(yp, ada_p, lw, zero_state,
                          lambda u, g, gt: _fox_prompt(u, g, gt, bp, sp), chunk_p, BF16, True)
        res_p.append(outs)

        state = (jnp.swapaxes(state_mlstm_C[l], -1, -2), state_mlstm_n[l][:, :, None, :],
                 jnp.broadcast_to(state_mlstm_m[l][:, :, None, None], (bs, NH_M, 1, LANES)),
                 jnp.pad(state_conv[l], ((0, 0), (SUBLANES - (CONV_W - 1), 0), (0, 0))),
                 state_ssm[l])
        ys, outs = _mixer(ys, ada_s, lw, state,
                          lambda u, g, gt, l=l: _fox_sample(u, gt, page_table, ck_t, cv_t, clf_t, l * n_phys, bs, ss),
                          chunk_s, F32, False)
        res_s.append(outs)

    stack = lambda res, i: jnp.stack([r[i] for r in res])
    return (yp, ys) + tuple(stack(res_p, i) for i in range(8)) + tuple(stack(res_s, i) for i in range(8))
```

```python
import functools

import jax
import jax.numpy as jnp
from jax import lax
from jax.experimental import pallas as pl
from jax.experimental.pallas import tpu as pltpu

F32 = jnp.float32
BF16 = jnp.bfloat16

NH_M, DQK_M, DV_M = 4, 64, 128
W_M = NH_M * DV_M
GATE_CAP = 15.0
NH_S, P_S, N_GROUPS, D_STATE, CONV_W = 8, 64, 2, 128, 4
D_INNER = NH_S * P_S
CONV_CH = D_INNER + 2 * N_GROUPS * D_STATE
NH_F, DH_F = 8, 64
W_F = NH_F * DH_F
CHUNK = 64
SCAN_CHUNK = 256
EPS = 1e-6
LANES = 128
SUBLANES = 8

LN_LOGI, LN_LOGF, LN_DT, LN_FF, LN_BM, LN_CUM, LN_FT = 0, 4, 8, 16, 24, 32, 40

VMEM_LIMIT = 56 * 1024 * 1024


def _cparams(sem):
    return pltpu.CompilerParams(dimension_semantics=sem, vmem_limit_bytes=VMEM_LIMIT)


def _sigmoid(x):
    return 1.0 / (1.0 + jnp.exp(-x))


def _softplus(x):
    return jnp.maximum(x, 0.0) + jnp.log(1.0 + jnp.exp(-jnp.abs(x)))


def _rms(x, g):
    return x * lax.rsqrt(jnp.mean(x * x, axis=-1, keepdims=True) + EPS) * g


def _dot(a, b):
    return jnp.dot(a, b, preferred_element_type=F32)


def _dot_nt(a, b):
    return lax.dot_general(a, b, (((1,), (1,)), ((), ())), preferred_element_type=F32)


def _dot_tn(a, b):
    return lax.dot_general(a, b, (((0,), (0,)), ((), ())), preferred_element_type=F32)


def _const_spec(shape):
    nd = len(shape)
    return pl.BlockSpec(shape, lambda *_: (0,) * nd)


def _ada_kernel(c_ref, w_ref, b_ref, o_ref):
    c = c_ref[...]
    s = (c * _sigmoid(c)).astype(BF16)
    o_ref[0] = _dot(s, w_ref[0].astype(BF16)) + b_ref[0]


def _ada(c_all, w_ada, b_ada):
    depth, d, n = w_ada.shape
    rows = c_all.shape[0]
    tn = 512
    return pl.pallas_call(
        _ada_kernel,
        grid=(depth, n // tn),
        in_specs=[pl.BlockSpec((rows, d), lambda l, j: (0, 0)),
                  pl.BlockSpec((1, d, tn), lambda l, j: (l, 0, j)),
                  pl.BlockSpec((1, 1, tn), lambda l, j: (l, 0, j))],
        out_specs=pl.BlockSpec((1, rows, tn), lambda l, j: (l, 0, j)),
        out_shape=jax.ShapeDtypeStruct((depth, rows, n), F32),
        compiler_params=_cparams(("arbitrary", "arbitrary")),
        name="ada",
    )(c_all, w_ada, b_ada.reshape(depth, 1, n))


SEGS = (("mq", NH_M * DQK_M, DQK_M ** -0.5, F32), ("mk", NH_M * DQK_M, 1.0, F32), ("mv", W_M, 1.0, F32),
        ("mo", W_M, 1.0, F32), ("sz", D_INNER, 1.0, F32), ("sxbc", CONV_CH, 1.0, F32),
        ("fq", W_F, DH_F ** -0.5, BF16), ("small", LANES, 1.0, F32))
W_PROJ = sum(sg[1] for sg in SEGS)


def _inproj_kernel(x_ref, sh_ref, sc_ref, g_ref, w_ref, wkv_ref, *outs, kv_t):
    x = x_ref[...]
    tb, ts, d = x.shape
    h = _rms(x, g_ref[...]) * (1.0 + sc_ref[...]) + sh_ref[...]
    hb = h.reshape(tb * ts, d).astype(BF16)
    off = 0
    for (_, wd, scale, _), o in zip(SEGS, outs):
        r = _dot(hb, w_ref[:, off:off + wd])
        if scale != 1.0:
            r = r * scale
        o[...] = r.astype(o.dtype)
        off += wd
    small_ref, smallt_ref, fk_ref, fv_ref = outs[len(SEGS) - 1:]
    smallt_ref[...] = small_ref[...].T
    if kv_t:
        kv = _dot_nt(wkv_ref[...], hb)
        fk_ref[0] = kv[:W_F]
        fv_ref[0] = kv[W_F:]
    else:
        kv = _dot(hb, wkv_ref[...])
        fk_ref[...] = kv[:, :W_F]
        fv_ref[...] = kv[:, W_F:]


def _x_tiles(b, s, ts_max):
    if s >= ts_max:
        return 1, ts_max
    tb = max(1, min(b, ts_max // s))
    return tb, s


def _ada_spec(tb, d, col):
    return pl.BlockSpec((tb, 1, d), lambda i, j: (i, 0, col))


def _inproj(x, ada, g_pre, w, wkv, kv_t):
    b, s, d = x.shape
    tb, ts = _x_tiles(b, s, 256)
    tm = tb * ts
    nj = s // ts
    m = b * s
    row = lambda i, j: (i * nj + j, 0)
    out_shape = [jax.ShapeDtypeStruct((m, wd), dt) for _, wd, _, dt in SEGS]
    out_specs = [pl.BlockSpec((tm, wd), row) for _, wd, _, _ in SEGS]
    out_shape.append(jax.ShapeDtypeStruct((LANES, m), F32))
    out_specs.append(pl.BlockSpec((LANES, tm), lambda i, j: (0, i * nj + j)))
    if kv_t:
        assert tb == 1
        out_shape += [jax.ShapeDtypeStruct((b, W_F, s), F32)] * 2
        out_specs += [pl.BlockSpec((1, W_F, ts), lambda i, j: (i, 0, j))] * 2
    else:
        out_shape += [jax.ShapeDtypeStruct((m, W_F), F32)] * 2
        out_specs += [pl.BlockSpec((tm, W_F), row)] * 2
    outs = pl.pallas_call(
        functools.partial(_inproj_kernel, kv_t=kv_t),
        grid=(b // tb, nj),
        in_specs=[pl.BlockSpec((tb, ts, d), lambda i, j: (i, j, 0)),
                  _ada_spec(tb, d, 0), _ada_spec(tb, d, 1),
                  _const_spec((1, d)),
                  pl.BlockSpec((d, W_PROJ), lambda i, j: (0, 0), pipeline_mode=pl.Buffered(1)),
                  pl.BlockSpec(wkv.shape, lambda i, j: (0, 0), pipeline_mode=pl.Buffered(1))],
        out_specs=out_specs,
        out_shape=out_shape,
        compiler_params=_cparams(("arbitrary", "arbitrary")),
        name="inproj",
    )(x, ada, ada, g_pre.reshape(1, d), w, wkv)
    names = [sg[0] for sg in SEGS] + ["small_t", "fk", "fv"]
    return dict(zip(names, outs))


def _prep_kernel(small_ref, bias_ref, alog_ref, g_ref, gt_ref, carry_ref, *, chunk, seg, tiles_per_seq):
    i = pl.program_id(0)

    @pl.when(i % tiles_per_seq == 0)
    def _():
        carry_ref[...] = jnp.zeros_like(carry_ref)

    v = small_ref[...] + bias_ref[...]
    ts = v.shape[0]
    lane = lax.broadcasted_iota(jnp.int32, v.shape, 1)
    row = lax.broadcasted_iota(jnp.int32, v.shape, 0)
    capped = GATE_CAP * jnp.tanh(v * (1.0 / GATE_CAP))
    is_m = (lane < LN_DT) | ((lane >= LN_BM) & (lane < LN_CUM))
    vv = jnp.where(is_m, capped, v)
    sp = _softplus(vv)
    lsig = -_softplus(-vv)
    a = -jnp.exp(alog_ref[...])
    val = jnp.where(lane < LN_LOGF, vv,
          jnp.where(lane < LN_DT, lsig,
          jnp.where(lane < LN_FF, sp,
          jnp.where(lane < LN_CUM, lsig,
          jnp.where(lane < LN_FT, sp * a, lsig)))))
    val = jnp.where(lane < LN_FT + NH_F, val, 0.0)
    ridx = jnp.where(lane < LN_FT, row % chunk, row % seg)
    ridx = jnp.where(lane >= LN_BM, ridx, -1)
    x = val
    k = 1
    while k < min(ts, max(chunk, seg)):
        x = x + jnp.where(ridx >= k, pltpu.roll(x, k, 0), 0.0)
        k *= 2
    x = x + jnp.where(lane >= LN_FT, carry_ref[...], 0.0)
    carry_ref[...] = x[ts - 1:ts, :]
    g_ref[...] = x
    gt_ref[...] = x.T


def _prep(small, bias_lanes, alog_lanes, s, chunk):
    m = small.shape[0]
    ts = min(512, m)
    if s >= ts:
        seg, tiles_per_seq = ts, s // ts
    else:
        seg, tiles_per_seq = s, 1
    return pl.pallas_call(
        functools.partial(_prep_kernel, chunk=chunk, seg=seg, tiles_per_seq=tiles_per_seq),
        grid=(m // ts,),
        in_specs=[pl.BlockSpec((ts, LANES), lambda i: (i, 0)), _const_spec((1, LANES)), _const_spec((1, LANES))],
        out_specs=[pl.BlockSpec((ts, LANES), lambda i: (i, 0)), pl.BlockSpec((LANES, ts), lambda i: (0, i))],
        out_shape=[jax.ShapeDtypeStruct((m, LANES), F32), jax.ShapeDtypeStruct((LANES, m), F32)],
        scratch_shapes=[pltpu.VMEM((1, LANES), F32)],
        compiler_params=_cparams(("arbitrary",)),
        name="prep",
    )(small, bias_lanes, alog_lanes)


def _mlstm_kernel(q_ref, k_ref, v_ref, mo_ref, g_ref, gt_ref, gh_ref, c0_ref, m0_ref,
                  y_ref, c_ref, m_ref, *, chunk, gt3d, mm):
    j = pl.program_id(1)

    @pl.when(j == 0)
    def _():
        c_ref[...] = c0_ref[...]
        m_ref[...] = m0_ref[...]

    ts = q_ref.shape[0]
    ln = chunk
    gt = gt_ref[0] if gt3d else gt_ref[...]
    rr = lax.broadcasted_iota(jnp.int32, (ln, ln), 0)
    cc = lax.broadcasted_iota(jnp.int32, (ln, ln), 1)
    causal = cc <= rr
    ones = jnp.ones((ln, DV_M), F32)
    heads = range(NH_M)
    for c in range(ts // ln):
        lo, hi = c * ln, (c + 1) * ln
        g = g_ref[lo:hi, :]
        qb = [q_ref[lo:hi, h * DQK_M:(h + 1) * DQK_M].astype(mm) for h in heads]
        kb = [k_ref[lo:hi, h * DQK_M:(h + 1) * DQK_M].astype(mm) for h in heads]
        vf = [v_ref[lo:hi, h * DV_M:(h + 1) * DV_M] for h in heads]
        b_c = [g[:, LN_BM + h:LN_BM + h + 1] for h in heads]
        m_st = [m_ref[0, h][:, 0:1] for h in heads]
        st = [c_ref[0, h] for h in heads]
        qk = [_dot_nt(qb[h], kb[h]) for h in heads]
        qc = [_dot(qb[h], st[h].astype(mm)) for h in heads]
        dm = [jnp.where(causal, b_c[h] - gt[LN_BM + h:LN_BM + h + 1, lo:hi]
                        + gt[LN_LOGI + h:LN_LOGI + h + 1, lo:hi], -jnp.inf) for h in heads]
        inter = [b_c[h] + m_st[h] for h in heads]
        m_t = [jnp.maximum(inter[h], jnp.max(dm[h], axis=1, keepdims=True)) for h in heads]
        s = [(qk[h] * jnp.exp(dm[h] - m_t[h])).astype(mm) for h in heads]
        sv = [_dot(s[h], jnp.concatenate([vf[h], ones], axis=1).astype(mm)) for h in heads]
        m_new = [m_t[h][ln - 1:ln, :] for h in heads]
        wk = [jnp.exp(b_c[h][ln - 1:ln, :] - b_c[h] + g[:, LN_LOGI + h:LN_LOGI + h + 1] - m_new[h]) for h in heads]
        wv = [jnp.concatenate([wk[h] * vf[h], jnp.broadcast_to(wk[h], (ln, DV_M))], axis=1).astype(mm) for h in heads]
        upd = [_dot_tn(kb[h], wv[h]) for h in heads]
        for h in heads:
            decay = jnp.exp(b_c[h][ln - 1:ln, :] + m_st[h] - m_new[h])
            c_ref[0, h] = decay * st[h] + upd[h]
            m_ref[0, h] = jnp.broadcast_to(m_new[h], (1, LANES))
        for h in heads:
            both = sv[h] + jnp.exp(inter[h] - m_t[h]) * qc[h]
            num, den = both[:, :DV_M], both[:, DV_M:]
            hv = num / jnp.maximum(jnp.abs(den), jnp.exp(-m_t[h]))
            hm = _rms(hv, gh_ref[:, h * DV_M:(h + 1) * DV_M])
            y = hm * _sigmoid(mo_ref[lo:hi, h * DV_M:(h + 1) * DV_M])
            y_ref[lo:hi, h * DV_M:(h + 1) * DV_M] = y.astype(y_ref.dtype)


def _gt_spec(b, s, ts, nj, gt3d):
    if gt3d:
        return pl.BlockSpec((1, LANES, s), lambda i, j: (i, 0, 0))
    return pl.BlockSpec((LANES, ts), lambda i, j: (0, i * nj + j))


def _mlstm(u, g, gt, g_mhead, c0n, m0, b, s, chunk, mm):
    ts = min(256, s)
    nj = s // ts
    m = b * s
    gt3d = gt.ndim == 3
    row = lambda i, j: (i * nj + j, 0)
    st = lambda *shape: pl.BlockSpec((1,) + shape, lambda i, j: (i,) + (0,) * len(shape))
    return pl.pallas_call(
        functools.partial(_mlstm_kernel, chunk=chunk, gt3d=gt3d, mm=mm),
        grid=(b, nj),
        in_specs=[pl.BlockSpec((ts, NH_M * DQK_M), row), pl.BlockSpec((ts, NH_M * DQK_M), row),
                  pl.BlockSpec((ts, W_M), row), pl.BlockSpec((ts, W_M), row),
                  pl.BlockSpec((ts, LANES), row), _gt_spec(b, s, ts, nj, gt3d),
                  _const_spec((1, W_M)),
                  st(NH_M, DQK_M, 2 * DV_M), st(NH_M, 1, LANES)],
        out_specs=[pl.BlockSpec((ts, W_M), row),
                   st(NH_M, DQK_M, 2 * DV_M), st(NH_M, 1, LANES)],
        out_shape=[jax.ShapeDtypeStruct((m, W_M), BF16),
                   jax.ShapeDtypeStruct((b, NH_M, DQK_M, 2 * DV_M), F32),
                   jax.ShapeDtypeStruct((b, NH_M, 1, LANES), F32)],
        compiler_params=_cparams(("arbitrary", "arbitrary")),
        name="mlstm",
    )(u["mq"], u["mk"], u["mv"], u["mo"], g, gt, g_mhead.reshape(1, W_M), c0n, m0)


def _ssd_kernel(xbc_ref, sz_ref, g_ref, gt_ref, cw_ref, cb_ref, dsk_ref, gs_ref, conv0_ref, h0_ref,
                y_ref, hst_ref, tail_ref, *, chunk, gt3d, mm):
    j = pl.program_id(1)

    @pl.when(j == 0)
    def _():
        hst_ref[...] = h0_ref[...]
        tail_ref[...] = conv0_ref[0]

    x = xbc_ref[...]
    ts = x.shape[0]
    ln = chunk
    prev = tail_ref[...]
    row8 = lax.broadcasted_iota(jnp.int32, prev.shape, 0)
    acc = cb_ref[...] + cw_ref[CONV_W - 1:CONV_W, :] * x
    for k in range(1, CONV_W):
        rolled = pltpu.roll(x, k, 0)
        first = jnp.where(row8 < k, pltpu.roll(prev, k, 0), rolled[:SUBLANES])
        shifted = first if ts == SUBLANES else jnp.concatenate([first, rolled[SUBLANES:]], axis=0)
        acc = acc + cw_ref[CONV_W - 1 - k:CONV_W - k, :] * shifted
    tail_ref[...] = x[ts - SUBLANES:, :]
    xa = acc * _sigmoid(acc)

    gt = gt_ref[0] if gt3d else gt_ref[...]
    rr = lax.broadcasted_iota(jnp.int32, (ln, ln), 0)
    cc = lax.broadcasted_iota(jnp.int32, (ln, ln), 1)
    causal = cc <= rr
    hpg = NH_S // N_GROUPS
    for c in range(ts // ln):
        lo, hi = c * ln, (c + 1) * ln
        g = g_ref[lo:hi, :]
        heads = range(NH_S)
        bmg = [xa[lo:hi, D_INNER + grp * D_STATE:D_INNER + (grp + 1) * D_STATE].astype(mm) for grp in range(N_GROUPS)]
        co = D_INNER + N_GROUPS * D_STATE
        cmg = [xa[lo:hi, co + grp * D_STATE:co + (grp + 1) * D_STATE].astype(mm) for grp in range(N_GROUPS)]
        cbm = [_dot_nt(cmg[grp], bmg[grp]) for grp in range(N_GROUPS)]
        cum_c = [g[:, LN_CUM + h:LN_CUM + h + 1] for h in heads]
        xh = [xa[lo:hi, h * P_S:(h + 1) * P_S] for h in heads]
        xdt = [xh[h] * g[:, LN_DT + h:LN_DT + h + 1] for h in heads]
        hst = [hst_ref[0, h] for h in heads]
        ych = [_dot_nt(cmg[h // hpg], hst[h].astype(mm)) for h in heads]
        ldec = [jnp.exp(jnp.where(causal, cum_c[h] - gt[LN_CUM + h:LN_CUM + h + 1, lo:hi], -jnp.inf)) for h in heads]
        ycb = [_dot((cbm[h // hpg] * ldec[h]).astype(mm), xdt[h].astype(mm)) for h in heads]
        c_last = [cum_c[h][ln - 1:ln, :] for h in heads]
        upd = [_dot_tn((jnp.exp(c_last[h] - cum_c[h]) * xdt[h]).astype(mm), bmg[h // hpg]) for h in heads]
        for h in heads:
            hst_ref[0, h] = jnp.exp(c_last[h]) * hst[h] + upd[h]
        ys = [ycb[h] + jnp.exp(cum_c[h]) * ych[h] + dsk_ref[:, h * P_S:(h + 1) * P_S] * xh[h] for h in heads]
        yy = jnp.concatenate(ys, axis=1)
        z = sz_ref[lo:hi, :]
        y_ref[lo:hi, :] = _rms(yy * (z * _sigmoid(z)), gs_ref[...]).astype(y_ref.dtype)


def _ssd(u, g, gt, conv_w, conv_b, d_skip, g_ssm, conv0, h0, b, s, chunk, mm):
    ts = min(256, s)
    nj = s // ts
    m = b * s
    gt3d = gt.ndim == 3
    row = lambda i, j: (i * nj + j, 0)
    st = lambda *shape: pl.BlockSpec((1,) + shape, lambda i, j: (i,) + (0,) * len(shape))
    return pl.pallas_call(
        functools.partial(_ssd_kernel, chunk=chunk, gt3d=gt3d, mm=mm),
        grid=(b, nj),
        in_specs=[pl.BlockSpec((ts, CONV_CH), row), pl.BlockSpec((ts, D_INNER), row),
                  pl.BlockSpec((ts, LANES), row), _gt_spec(b, s, ts, nj, gt3d),
                  _const_spec((CONV_W, CONV_CH)), _const_spec((1, CONV_CH)),
                  _const_spec((1, D_INNER)), _const_spec((1, D_INNER)),
                  st(SUBLANES, CONV_CH), st(NH_S, P_S, D_STATE)],
        out_specs=[pl.BlockSpec((ts, D_INNER), row), st(NH_S, P_S, D_STATE)],
        out_shape=[jax.ShapeDtypeStruct((m, D_INNER), BF16),
                   jax.ShapeDtypeStruct((b, NH_S, P_S, D_STATE), F32)],
        scratch_shapes=[pltpu.VMEM((SUBLANES, CONV_CH), F32)],
        compiler_params=_cparams(("arbitrary", "arbitrary")),
        name="ssd",
    )(u["sxbc"], u["sz"], g, gt, conv_w, conv_b.reshape(1, CONV_CH),
      jnp.repeat(d_skip, P_S).reshape(1, D_INNER), g_ssm.reshape(1, D_INNER), conv0, h0)


def _split3(x):
    hi = x.astype(BF16).astype(F32)
    r = x - hi
    mid = r.astype(BF16).astype(F32)
    return hi, mid, r - mid


FOX_EXT = 16
FOX_VROWS = 2 * DH_F + FOX_EXT
FOX_SUB = 256
FOX_WIDE = 4


def _foxp_kernel(q_ref, kt_ref, vt_ref, g_ref, ftr_ref, o_ref, kaug_ref, vaug_ref, *, tq):
    hp = pl.program_id(1)
    i = pl.program_id(2)
    hw = 2 * DH_F
    s_len = kaug_ref.shape[1]

    @pl.when(i == 0)
    def _():
        kaug_ref[0:hw, :] = kt_ref[0].astype(BF16)
        vaug_ref[0:hw, :] = vt_ref[0].astype(BF16)
        r = lax.broadcasted_iota(jnp.int32, (FOX_EXT, s_len), 0)
        ext = jnp.where(r < 3, 1.0, 0.0)
        for hh in range(2):
            parts = _split3(ftr_ref[pl.ds(hp * 2 + hh, 1), :])
            for pi, part in enumerate(parts):
                ext = jnp.where(r == 3 + 3 * hh + pi, -part, ext)
        kaug_ref[hw:hw + FOX_EXT, :] = ext.astype(BF16)
        kaug_ref[hw + FOX_EXT:, :] = jnp.zeros((kaug_ref.shape[0] - hw - FOX_EXT, s_len), BF16)
        vaug_ref[hw:, :] = jnp.where(r == 0, 1.0, 0.0).astype(BF16)

    g = g_ref[...]
    q = q_ref[...]
    lane = lax.broadcasted_iota(jnp.int32, (tq, LANES), 1)
    low = lane < DH_F
    qaug = []
    for hh in range(2):
        ft_c = jnp.sum(jnp.where(lane == LN_FT + hp * 2 + hh, g, 0.0), axis=1, keepdims=True)
        hi, mid, lo = _split3(ft_c)
        ext = jnp.where(lane == 0, hi, jnp.where(lane == 1, mid, jnp.where(lane == 2, lo, 0.0)))
        ext = jnp.where((lane >= 3 + 3 * hh) & (lane < 6 + 3 * hh), 1.0, ext)
        qm = jnp.where(low if hh == 0 else ~low, q, jnp.zeros_like(q))
        qaug.append(jnp.concatenate([qm, ext.astype(BF16)], axis=1))

    rr = lax.broadcasted_iota(jnp.int32, (tq, tq), 0)
    cc = lax.broadcasted_iota(jnp.int32, (tq, tq), 1)
    causal = cc <= rr

    def block(carry, start, width, diag):
        subs = [pl.ds(pl.multiple_of(start + c * FOX_SUB, FOX_SUB), FOX_SUB) for c in range(width // FOX_SUB)]
        ms, ls, acc = carry
        new_m, new_l, alphas, pvs = [], [], [], []
        scores = []
        for hh in range(2):
            ss = []
            for c, sub in enumerate(subs):
                s = _dot(qaug[hh], kaug_ref[:, sub])
                dc = c * FOX_SUB - (width - tq)
                if diag and dc >= 0:
                    s = jnp.where(causal[:, dc:dc + FOX_SUB], s, -jnp.inf)
                ss.append(s)
            scores.append(ss)
        for hh in range(2):
            m_blk = jnp.max(functools.reduce(jnp.maximum, scores[hh]), axis=1, keepdims=True)
            new_m.append(jnp.maximum(ms[hh], m_blk))
            alphas.append(jnp.exp(ms[hh] - new_m[hh]))
        pv = [None, None]
        for c in range(len(subs)):
            for hh in range(2):
                p = jnp.exp(scores[hh][c] - new_m[hh]).astype(BF16)
                d = _dot_nt(p, vaug_ref[:, subs[c]])
                pv[hh] = d if pv[hh] is None else pv[hh] + d
        for hh in range(2):
            new_l.append(alphas[hh] * ls[hh] + pv[hh][:, hw:hw + 1])
            pvs.append(pv[hh][:, :hw])
        acc = jnp.where(low, alphas[0], alphas[1]) * acc + jnp.where(low, pvs[0], pvs[1])
        return tuple(new_m), tuple(new_l), acc

    neg = jnp.full((tq, 1), -jnp.inf, F32)
    zero = jnp.zeros((tq, 1), F32)
    init = ((neg, neg), (zero, zero), jnp.zeros((tq, hw), F32))
    wide = FOX_WIDE * tq
    carry = lax.fori_loop(0, i // FOX_WIDE, lambda jb, c: block(c, jb * wide, wide, False), init)
    tail_start = (i // FOX_WIDE) * wide

    def tail(r):
        return lambda c: block(c, tail_start, (r + 1) * tq, True)

    def pick(lo, hi):
        if hi - lo == 1:
            return tail(lo)
        mid = (lo + hi) // 2
        return lambda c: lax.cond(i % FOX_WIDE < mid, pick(lo, mid), pick(mid, hi), c)

    _, ls, acc = pick(0, FOX_WIDE)(carry)
    o_ref[...] = (acc / jnp.where(low, ls[0], ls[1])).astype(o_ref.dtype)


def _fox_prompt(u, g, gt, b, s):
    tq = min(512, s)
    nq = s // tq
    m = b * s
    hw = 2 * DH_F
    return pl.pallas_call(
        functools.partial(_foxp_kernel, tq=tq),
        grid=(b, NH_F // 2, nq),
        in_specs=[pl.BlockSpec((tq, hw), lambda bb, hp, i: (bb * nq + i, hp)),
                  pl.BlockSpec((1, hw, s), lambda bb, hp, i: (bb, hp, 0)),
                  pl.BlockSpec((1, hw, s), lambda bb, hp, i: (bb, hp, 0)),
                  pl.BlockSpec((tq, LANES), lambda bb, hp, i: (bb * nq + i, 0)),
                  pl.BlockSpec((SUBLANES, s), lambda bb, hp, i: (LN_FT // SUBLANES, bb))],
        out_specs=pl.BlockSpec((tq, hw), lambda bb, hp, i: (bb * nq + i, hp)),
        out_shape=jax.ShapeDtypeStruct((m, W_F), BF16),
        scratch_shapes=[pltpu.VMEM((2 * hw, s), BF16), pltpu.VMEM((FOX_VROWS, s), BF16)],
        compiler_params=_cparams(("arbitrary", "arbitrary", "arbitrary")),
        name="fox_prompt",
    )(u["fq"], u["fk"], u["fv"], g, gt)


def _foxs_kernel(pt_ref, q_ref, kn_ref, vn_ref, gt_ref, *rest, pages):
    del pt_ref
    k_refs, v_refs, lf_refs = rest[:pages], rest[pages:2 * pages], rest[2 * pages:3 * pages]
    o_ref, qbd_ref, m_ref, l_ref, acc_ref, carry_ref, kcat_ref, vcat_ref = rest[3 * pages:]
    gi = pl.program_id(1)
    t = q_ref.shape[0]
    ht = NH_F * t
    page = k_refs[0].shape[2]

    def expand_heads(x):
        return jnp.broadcast_to(x[:, None, :], (NH_F, t, x.shape[1])).reshape(ht, x.shape[1])

    @pl.when(gi == 0)
    def _():
        q = q_ref[...].astype(F32)
        qt = jnp.broadcast_to(q[None], (NH_F, t, W_F)).reshape(ht, W_F)
        rh = lax.broadcasted_iota(jnp.int32, (ht, W_F), 0) // t
        lh = lax.broadcasted_iota(jnp.int32, (ht, W_F), 1) // DH_F
        qbd_ref[...] = jnp.where(rh == lh, qt, 0.0)
        m_ref[...] = jnp.full_like(m_ref, -jnp.inf)
        l_ref[...] = jnp.zeros_like(l_ref)
        acc_ref[...] = jnp.zeros_like(acc_ref)
        carry_ref[...] = jnp.zeros_like(carry_ref)

    cn_rows = expand_heads(gt_ref[0][LN_FT:LN_FT + NH_F, :])
    tq_idx = lax.broadcasted_iota(jnp.int32, (ht, t), 0) % t
    tk_idx = lax.broadcasted_iota(jnp.int32, (ht, t), 1)
    cn_col = jnp.sum(jnp.where(tq_idx == tk_idx, cn_rows, 0.0), axis=1, keepdims=True)

    qbd = qbd_ref[...]
    qbd_b = qbd.astype(BF16)

    def update(s, pv):
        m = m_ref[...]
        m_new = jnp.maximum(m, jnp.max(s, axis=1, keepdims=True))
        p = jnp.exp(s - m_new)
        alpha = jnp.exp(m - m_new)
        l_ref[...] = alpha * l_ref[...] + jnp.sum(p, axis=1, keepdims=True)
        acc_ref[...] = alpha * acc_ref[...] + pv(p)
        m_ref[...] = m_new

    lf_all = jnp.concatenate([lf_refs[pi][0] for pi in range(pages)], axis=0)
    tri = jnp.where(lax.broadcasted_iota(jnp.int32, (page, page), 0)
                    <= lax.broadcasted_iota(jnp.int32, (page, page), 1), 1.0, 0.0).astype(BF16)
    pre_all = sum(_dot(part.astype(BF16), tri) for part in _split3(lf_all))
    carry = carry_ref[...]
    rs = [None] * pages
    for pi in reversed(range(pages)):
        pre = pre_all[pi * NH_F:(pi + 1) * NH_F, :]
        tot = pre[:, page - 1:page]
        rs[pi] = expand_heads(carry + tot - pre)
        carry = carry + tot
        kcat_ref[:, pi * page:(pi + 1) * page] = k_refs[pi][0].astype(BF16)
        vcat_ref[:, pi * page:(pi + 1) * page] = v_refs[pi][0].astype(BF16)
    carry_ref[...] = carry
    s_all = _dot(qbd_b, kcat_ref[...]) + jnp.concatenate(rs, axis=1) + cn_col
    update(s_all, lambda p: _dot_nt(p.astype(BF16), vcat_ref[...]))

    @pl.when(gi == pl.num_programs(1) - 1)
    def _():
        s_new = _dot_nt(qbd, kn_ref[...]) + (cn_col - cn_rows)
        s_new = jnp.where(tk_idx <= tq_idx, s_new, -jnp.inf)
        update(s_new, lambda p: _dot(p, vn_ref[...]))
        o = acc_ref[...] / l_ref[...]
        lh = lax.broadcasted_iota(jnp.int32, (t, W_F), 1) // DH_F
        y = jnp.zeros((t, W_F), F32)
        for h in range(NH_F):
            y = y + jnp.where(lh == h, o[h * t:(h + 1) * t, :], 0.0)
        o_ref[...] = y.astype(o_ref.dtype)


def _fox_sample(u, gt3, page_table, ck_t, cv_t, clf_t, page_base, b, t):
    n_pages = page_table.shape[1]
    page = ck_t.shape[2]
    pages = 8
    while n_pages % pages:
        pages //= 2
    ng = n_pages // pages
    ht = NH_F * t

    def page_spec(width_shape, pi):
        return pl.BlockSpec((1,) + width_shape,
                            lambda bb, gi, pt: (page_base + pt[bb, (ng - 1 - gi) * pages + pi], 0, 0))

    tok = pl.BlockSpec((t, W_F), lambda bb, gi, pt: (bb, 0))
    in_specs = [tok, tok, tok, pl.BlockSpec((1, LANES, t), lambda bb, gi, pt: (bb, 0, 0))]
    in_specs += [page_spec((W_F, page), pi) for pi in range(pages)]
    in_specs += [page_spec((W_F, page), pi) for pi in range(pages)]
    in_specs += [page_spec((NH_F, page), pi) for pi in range(pages)]
    grid_spec = pltpu.PrefetchScalarGridSpec(
        num_scalar_prefetch=1,
        grid=(b, ng),
        in_specs=in_specs,
        out_specs=pl.BlockSpec((t, W_F), lambda bb, gi, pt: (bb, 0)),
        scratch_shapes=[pltpu.VMEM((ht, W_F), F32), pltpu.VMEM((ht, 1), F32), pltpu.VMEM((ht, 1), F32),
                        pltpu.VMEM((ht, W_F), F32), pltpu.VMEM((NH_F, 1), F32),
                        pltpu.VMEM((W_F, pages * page), BF16), pltpu.VMEM((W_F, pages * page), BF16)],
    )
    return pl.pallas_call(
        functools.partial(_foxs_kernel, pages=pages),
        grid_spec=grid_spec,
        out_shape=jax.ShapeDtypeStruct((b * t, W_F), BF16),
        compiler_params=_cparams(("arbitrary", "arbitrary")),
        name="fox_sample",
    )(page_table, u["fq"], u["fk"], u["fv"], gt3, *([ck_t] * pages), *([cv_t] * pages), *([clf_t] * pages))


def _merge_kernel(x_ref, sh_ref, sc_ref, gt_ref, gpre_ref, gpost_ref, ym_ref, ys_ref, yf_ref,
                  wg_ref, wm_ref, ws_ref, wf_ref, wo_ref, o_ref):
    x = x_ref[...]
    tb, ts, d = x.shape
    h = _rms(x, gpre_ref[...]) * (1.0 + sc_ref[...]) + sh_ref[...]
    hb = h.reshape(tb * ts, d).astype(BF16)
    merged = None
    for bi, (y_ref, w_ref) in enumerate(((ym_ref, wm_ref), (ys_ref, ws_ref), (yf_ref, wf_ref))):
        gate = _sigmoid(_dot(hb, wg_ref[:, bi * d:(bi + 1) * d]))
        term = gate * _dot(y_ref[...], w_ref[...])
        merged = term if merged is None else merged + term
    out = _dot(merged.astype(BF16), wo_ref[...]).reshape(tb, ts, d)
    o_ref[...] = x + gt_ref[...] * _rms(out, gpost_ref[...])


def _merge(x, ada, g_pre, g_post, ym, ys, yf, wg, wm, ws, wf, wo):
    b, s, d = x.shape
    tb, ts = _x_tiles(b, s, 256)
    tm = tb * ts
    nj = s // ts
    row = lambda i, j: (i * nj + j, 0)
    wspec = lambda shape: pl.BlockSpec(shape, lambda i, j: (0, 0), pipeline_mode=pl.Buffered(1))
    return pl.pallas_call(
        _merge_kernel,
        grid=(b // tb, nj),
        in_specs=[pl.BlockSpec((tb, ts, d), lambda i, j: (i, j, 0)),
                  _ada_spec(tb, d, 0), _ada_spec(tb, d, 1), _ada_spec(tb, d, 2),
                  _const_spec((1, d)), _const_spec((1, d)),
                  pl.BlockSpec((tm, W_M), row), pl.BlockSpec((tm, D_INNER), row), pl.BlockSpec((tm, W_F), row),
                  wspec((d, 3 * d)), wspec((W_M, d)), wspec((D_INNER, d)), wspec((W_F, d)), wspec((d, d))],
        out_specs=pl.BlockSpec((tb, ts, d), lambda i, j: (i, j, 0)),
        out_shape=jax.ShapeDtypeStruct((b, s, d), F32),
        compiler_params=_cparams(("arbitrary", "arbitrary")),
        name="merge",
    )(x, ada, ada, ada, g_pre.reshape(1, d), g_post.reshape(1, d), ym, ys, yf, wg, wm, ws, wf, wo)


def _mlp_kernel(x_ref, sh_ref, sc_ref, gt_ref, gpre_ref, gpost_ref, wu_ref, wd_ref, o_ref):
    x = x_ref[...]
    tb, ts, d = x.shape
    h = _rms(x, gpre_ref[...]) * (1.0 + sc_ref[...]) + sh_ref[...]
    hb = h.reshape(tb * ts, d).astype(BF16)
    up = jnp.maximum(_dot(hb, wu_ref[...]), 0.0)
    f = _dot((up * up).astype(BF16), wd_ref[...]).reshape(tb, ts, d)
    o_ref[...] = x + gt_ref[...] * _rms(f, gpost_ref[...])


def _mlp(x, ada, g_pre, g_post, wu, wd):
    b, s, d = x.shape
    tb, ts = _x_tiles(b, s, 256)
    nj = s // ts
    dff = wu.shape[1]
    wspec = lambda shape: pl.BlockSpec(shape, lambda i, j: (0, 0), pipeline_mode=pl.Buffered(1))
    return pl.pallas_call(
        _mlp_kernel,
        grid=(b // tb, nj),
        in_specs=[pl.BlockSpec((tb, ts, d), lambda i, j: (i, j, 0)),
                  _ada_spec(tb, d, 3), _ada_spec(tb, d, 4), _ada_spec(tb, d, 5),
                  _const_spec((1, d)), _const_spec((1, d)),
                  wspec((d, dff)), wspec((dff, d))],
        out_specs=pl.BlockSpec((tb, ts, d), lambda i, j: (i, j, 0)),
        out_shape=jax.ShapeDtypeStruct((b, s, d), F32),
        compiler_params=_cparams(("arbitrary", "arbitrary")),
        name="mlp",
    )(x, ada, ada, ada, g_pre.reshape(1, d), g_post.reshape(1, d), wu, wd)


def _split_w_in(w):
    sizes = (NH_M * DQK_M, NH_M * DQK_M, W_M, NH_M, NH_M, W_M, D_INNER, CONV_CH, NH_S, W_F, W_F, W_F, NH_F)
    names = ("mq", "mk", "mv", "mi", "mf", "mo", "sz", "sxbc", "sdt", "fq", "fk", "fv", "ff")
    cols, off = {}, 0
    for nm, sz in zip(names, sizes):
        cols[nm] = w[:, off:off + sz]
        off += sz
    gates = w[:, off:]
    d = w.shape[0]
    zeros = lambda n: jnp.zeros((d, n), w.dtype)
    small = jnp.concatenate(
        [cols["mi"], cols["mf"], cols["sdt"], cols["ff"], cols["mf"], zeros(LN_CUM - LN_BM - NH_M),
         cols["sdt"], cols["ff"], zeros(LANES - LN_FT - NH_F)], axis=1)
    proj = jnp.concatenate([cols[sg[0]] for sg in SEGS[:-1]] + [small], axis=1)
    kv = jnp.concatenate([cols["fk"], cols["fv"]], axis=1)
    return proj.astype(BF16), kv.astype(BF16), gates.astype(BF16)


def _gate_lanes(b_mgate, dt_bias, b_ffox, a_log):
    z = lambda n: jnp.zeros((n,), F32)
    bias = jnp.concatenate([b_mgate, dt_bias, b_ffox, b_mgate[NH_M:], z(LN_CUM - LN_BM - NH_M),
                            dt_bias, b_ffox, z(LANES - LN_FT - NH_F)])
    alog = jnp.concatenate([z(LN_CUM), a_log, z(LANES - LN_CUM - NH_S)])
    return bias.reshape(1, LANES), alog.reshape(1, LANES)


def _mixer(x, ada, lw, state, attend, chunk, mm, kv_t):
    b, s, d = x.shape
    u = _inproj(x, ada, lw["g_pre_mix"], lw["w_proj"], lw["w_kv"].T if kv_t else lw["w_kv"], kv_t)
    g, gt = _prep(u["small"], lw["bias_lanes"], lw["alog_lanes"], s, chunk)
    if s % LANES:
        gt = gt.reshape(LANES, b, s).transpose(1, 0, 2)
    c0n, m0, conv0, h0 = state
    ym, c_n, m_new = _mlstm(u, g, gt, lw["g_mhead"], c0n, m0, b, s, chunk, mm)
    ys, h_new = _ssd(u, g, gt, lw["conv_w"], lw["conv_b"], lw["d_skip"], lw["g_ssm"], conv0, h0, b, s, chunk, mm)
    yf = attend(u, g, gt)
    x1 = _merge(x, ada, lw["g_pre_mix"], lw["g_post_mix"], ym, ys, yf,
                lw["w_gates"], lw["w_br_m"], lw["w_br_s"], lw["w_br_f"], lw["w_out"])
    x2 = _mlp(x1, ada, lw["g_pre_mlp"], lw["g_post_mlp"], lw["w_up"], lw["w_down"])
    if kv_t:
        rows = lambda a: jnp.transpose(a.reshape(b, NH_F, DH_F, s), (0, 3, 1, 2))
    else:
        rows = lambda a: a.reshape(b, s, NH_F, DH_F)
    outs = (rows(u["fk"]), rows(u["fv"]),
            g[:, LN_FF:LN_FF + NH_F].reshape(b, s, NH_F),
            jnp.swapaxes(c_n[..., :DV_M], -1, -2), c_n[..., DV_M], m_new[:, :, 0, 0],
            u["sxbc"].reshape(b, s, CONV_CH)[:, s - (CONV_W - 1):, :], h_new)
    return x2, outs


def kernel(x_prompt, x_sample, cache_k, cache_v, cache_logf, state_mlstm_C, state_mlstm_n, state_mlstm_m,
           state_conv, state_ssm, page_table, c_prompt, c_sample, w_ada, b_ada, g_pre_mix, g_post_mix, w_in,
           b_mgate, b_ffox, g_mhead, conv_w, conv_b, dt_bias, a_log, d_skip, g_ssm, w_br_m, w_br_s, w_br_f,
           w_out, g_pre_mlp, g_post_mlp, w_up, w_down):
    depth = w_in.shape[0]
    bp, sp, d = x_prompt.shape
    bs, ss, _ = x_sample.shape
    n_phys, page = cache_k.shape[1], cache_k.shape[2]

    pad = (-(bp + bs)) % SUBLANES
    c_all = jnp.concatenate([c_prompt, c_sample, jnp.zeros((pad, d), F32)], axis=0)
    ada_all = _ada(c_all, w_ada, b_ada)

    ck_t = jnp.transpose(cache_k, (0, 1, 3, 4, 2)).reshape(depth * n_phys, W_F, page)
    cv_t = jnp.transpose(cache_v, (0, 1, 3, 4, 2)).reshape(depth * n_phys, W_F, page)
    clf_t = jnp.transpose(cache_logf, (0, 1, 3, 2)).reshape(depth * n_phys, NH_F, page)

    chunk_p = SCAN_CHUNK if sp % SCAN_CHUNK == 0 else (CHUNK if sp % CHUNK == 0 else sp)
    chunk_s = SCAN_CHUNK if ss % SCAN_CHUNK == 0 else (CHUNK if ss % CHUNK == 0 else ss)
    yp, ys = x_prompt, x_sample
    res_p, res_s = [], []
    for l in range(depth):
        w_proj, w_kv, w_gates = _split_w_in(w_in[l])
        bias_lanes, alog_lanes = _gate_lanes(b_mgate[l], dt_bias[l], b_ffox[l], a_log[l])
        lw = dict(w_proj=w_proj, w_kv=w_kv, w_gates=w_gates, bias_lanes=bias_lanes, alog_lanes=alog_lanes,
                  g_pre_mix=g_pre_mix[l], g_post_mix=g_post_mix[l], g_mhead=g_mhead[l],
                  conv_w=conv_w[l], conv_b=conv_b[l], d_skip=d_skip[l], g_ssm=g_ssm[l],
                  w_br_m=w_br_m[l].astype(BF16), w_br_s=w_br_s[l].astype(BF16), w_br_f=w_br_f[l].astype(BF16),
                  w_out=w_out[l].astype(BF16), g_pre_mlp=g_pre_mlp[l], g_post_mlp=g_post_mlp[l],
                  w_up=w_up[l].astype(BF16), w_down=w_down[l].astype(BF16))
        ada_p = ada_all[l, :bp][:, None, :]
        ada_s = ada_all[l, bp:bp + bs][:, None, :]

        zero_state = (jnp.zeros((bp, NH_M, DQK_M, 2 * DV_M), F32),
                      jnp.zeros((bp, NH_M, 1, LANES), F32), jnp.zeros((bp, SUBLANES, CONV_CH), F32),
                      jnp.zeros((bp, NH_S, P_S, D_STATE), F32))
        yp, outs = _mixer(yp, ada_p, lw, zero_state,
                          lambda u, g, gt: _fox_prompt(u, g, gt, bp, sp), chunk_p, BF16, True)
        res_p.append(outs)

        state = (jnp.concatenate([jnp.swapaxes(state_mlstm_C[l], -1, -2),
                                  jnp.broadcast_to(state_mlstm_n[l][..., None], (bs, NH_M, DQK_M, DV_M))], axis=-1),
                 jnp.broadcast_to(state_mlstm_m[l][:, :, None, None], (bs, NH_M, 1, LANES)),
                 jnp.pad(state_conv[l], ((0, 0), (SUBLANES - (CONV_W - 1), 0), (0, 0))),
                 state_ssm[l])
        ys, outs = _mixer(ys, ada_s, lw, state,
                          lambda u, g, gt, l=l: _fox_sample(u, gt, page_table, ck_t, cv_t, clf_t, l * n_phys, bs, ss),
                          chunk_s, F32, False)
        res_s.append(outs)

    stack = lambda res, i: jnp.stack([r[i] for r in res])
    return (yp, ys) + tuple(stack(res_p, i) for i in range(8)) + tuple(stack(res_s, i) for i in range(8))
```

```python
import functools

import jax
import jax.numpy as jnp
from jax import lax
from jax.experimental import pallas as pl
from jax.experimental.pallas import tpu as pltpu

F32 = jnp.float32
BF16 = jnp.bfloat16

NH_M, DQK_M, DV_M = 4, 64, 128
W_M = NH_M * DV_M
GATE_CAP = 15.0
NH_S, P_S, N_GROUPS, D_STATE, CONV_W = 8, 64, 2, 128, 4
D_INNER = NH_S * P_S
CONV_CH = D_INNER + 2 * N_GROUPS * D_STATE
NH_F, DH_F = 8, 64
W_F = NH_F * DH_F
CHUNK = 64
SCAN_CHUNK = 256
EPS = 1e-6
LANES = 128
SUBLANES = 8

LN_LOGI, LN_LOGF, LN_DT, LN_FF, LN_BM, LN_CUM, LN_FT = 0, 4, 8, 16, 24, 32, 40

VMEM_LIMIT = 56 * 1024 * 1024


def _cparams(sem):
    return pltpu.CompilerParams(dimension_semantics=sem, vmem_limit_bytes=VMEM_LIMIT)


def _sigmoid(x):
    return 1.0 / (1.0 + jnp.exp(-x))


def _softplus(x):
    return jnp.maximum(x, 0.0) + jnp.log(1.0 + jnp.exp(-jnp.abs(x)))


def _rms(x, g):
    return x * lax.rsqrt(jnp.mean(x * x, axis=-1, keepdims=True) + EPS) * g


def _dot(a, b):
    return jnp.dot(a, b, preferred_element_type=F32)


def _dot_nt(a, b):
    return lax.dot_general(a, b, (((1,), (1,)), ((), ())), preferred_element_type=F32)


def _dot_tn(a, b):
    return lax.dot_general(a, b, (((0,), (0,)), ((), ())), preferred_element_type=F32)


def _const_spec(shape):
    nd = len(shape)
    return pl.BlockSpec(shape, lambda *_: (0,) * nd)


def _ada_kernel(c_ref, w_ref, b_ref, o_ref):
    c = c_ref[...]
    s = (c * _sigmoid(c)).astype(BF16)
    o_ref[0] = _dot(s, w_ref[0].astype(BF16)) + b_ref[0]


def _ada(c_all, w_ada, b_ada):
    depth, d, n = w_ada.shape
    rows = c_all.shape[0]
    tn = 512
    return pl.pallas_call(
        _ada_kernel,
        grid=(depth, n // tn),
        in_specs=[pl.BlockSpec((rows, d), lambda l, j: (0, 0)),
                  pl.BlockSpec((1, d, tn), lambda l, j: (l, 0, j)),
                  pl.BlockSpec((1, 1, tn), lambda l, j: (l, 0, j))],
        out_specs=pl.BlockSpec((1, rows, tn), lambda l, j: (l, 0, j)),
        out_shape=jax.ShapeDtypeStruct((depth, rows, n), F32),
        compiler_params=_cparams(("arbitrary", "arbitrary")),
        name="ada",
    )(c_all, w_ada, b_ada.reshape(depth, 1, n))


SEGS = (("mq", NH_M * DQK_M, DQK_M ** -0.5, F32), ("mk", NH_M * DQK_M, 1.0, F32), ("mv", W_M, 1.0, F32),
        ("mo", W_M, 1.0, F32), ("sz", D_INNER, 1.0, F32), ("sxbc", CONV_CH, 1.0, F32),
        ("fq", W_F, DH_F ** -0.5, BF16), ("small", LANES, 1.0, F32))
W_PROJ = sum(sg[1] for sg in SEGS)


def _inproj_kernel(x_ref, sh_ref, sc_ref, g_ref, w_ref, wkv_ref, *outs, kv_t):
    x = x_ref[...]
    tb, ts, d = x.shape
    h = _rms(x, g_ref[...]) * (1.0 + sc_ref[...]) + sh_ref[...]
    hb = h.reshape(tb * ts, d).astype(BF16)
    off = 0
    for (_, wd, scale, _), o in zip(SEGS, outs):
        r = _dot(hb, w_ref[:, off:off + wd])
        if scale != 1.0:
            r = r * scale
        o[...] = r.astype(o.dtype)
        off += wd
    small_ref, smallt_ref, fk_ref, fv_ref = outs[len(SEGS) - 1:]
    smallt_ref[...] = small_ref[...].T
    if kv_t:
        kv = _dot_nt(wkv_ref[...], hb)
        fk_ref[0] = kv[:W_F]
        fv_ref[0] = kv[W_F:]
    else:
        kv = _dot(hb, wkv_ref[...])
        fk_ref[...] = kv[:, :W_F]
        fv_ref[...] = kv[:, W_F:]


def _x_tiles(b, s, ts_max):
    if s >= ts_max:
        return 1, ts_max
    tb = max(1, min(b, ts_max // s))
    return tb, s


def _ada_spec(tb, d, col):
    return pl.BlockSpec((tb, 1, d), lambda i, j: (i, 0, col))


def _inproj(x, ada, g_pre, w, wkv, kv_t):
    b, s, d = x.shape
    tb, ts = _x_tiles(b, s, 256)
    tm = tb * ts
    nj = s // ts
    m = b * s
    row = lambda i, j: (i * nj + j, 0)
    out_shape = [jax.ShapeDtypeStruct((m, wd), dt) for _, wd, _, dt in SEGS]
    out_specs = [pl.BlockSpec((tm, wd), row) for _, wd, _, _ in SEGS]
    out_shape.append(jax.ShapeDtypeStruct((LANES, m), F32))
    out_specs.append(pl.BlockSpec((LANES, tm), lambda i, j: (0, i * nj + j)))
    if kv_t:
        assert tb == 1
        out_shape += [jax.ShapeDtypeStruct((b, W_F, s), F32)] * 2
        out_specs += [pl.BlockSpec((1, W_F, ts), lambda i, j: (i, 0, j))] * 2
    else:
        out_shape += [jax.ShapeDtypeStruct((m, W_F), F32)] * 2
        out_specs += [pl.BlockSpec((tm, W_F), row)] * 2
    outs = pl.pallas_call(
        functools.partial(_inproj_kernel, kv_t=kv_t),
        grid=(b // tb, nj),
        in_specs=[pl.BlockSpec((tb, ts, d), lambda i, j: (i, j, 0)),
                  _ada_spec(tb, d, 0), _ada_spec(tb, d, 1),
                  _const_spec((1, d)),
                  pl.BlockSpec((d, W_PROJ), lambda i, j: (0, 0), pipeline_mode=pl.Buffered(1)),
                  pl.BlockSpec(wkv.shape, lambda i, j: (0, 0), pipeline_mode=pl.Buffered(1))],
        out_specs=out_specs,
        out_shape=out_shape,
        compiler_params=_cparams(("arbitrary", "arbitrary")),
        name="inproj",
    )(x, ada, ada, g_pre.reshape(1, d), w, wkv)
    names = [sg[0] for sg in SEGS] + ["small_t", "fk", "fv"]
    return dict(zip(names, outs))


def _prep_kernel(small_ref, bias_ref, alog_ref, g_ref, gt_ref, carry_ref, *, chunk, seg, tiles_per_seq):
    i = pl.program_id(0)

    @pl.when(i % tiles_per_seq == 0)
    def _():
        carry_ref[...] = jnp.zeros_like(carry_ref)

    v = small_ref[...] + bias_ref[...]
    ts = v.shape[0]
    lane = lax.broadcasted_iota(jnp.int32, v.shape, 1)
    row = lax.broadcasted_iota(jnp.int32, v.shape, 0)
    capped = GATE_CAP * jnp.tanh(v * (1.0 / GATE_CAP))
    is_m = (lane < LN_DT) | ((lane >= LN_BM) & (lane < LN_CUM))
    vv = jnp.where(is_m, capped, v)
    sp = _softplus(vv)
    lsig = -_softplus(-vv)
    a = -jnp.exp(alog_ref[...])
    val = jnp.where(lane < LN_LOGF, vv,
          jnp.where(lane < LN_DT, lsig,
          jnp.where(lane < LN_FF, sp,
          jnp.where(lane < LN_CUM, lsig,
          jnp.where(lane < LN_FT, sp * a, lsig)))))
    val = jnp.where(lane < LN_FT + NH_F, val, 0.0)
    ridx = jnp.where(lane < LN_FT, row % chunk, row % seg)
    ridx = jnp.where(lane >= LN_BM, ridx, -1)
    x = val
    k = 1
    while k < min(ts, max(chunk, seg)):
        x = x + jnp.where(ridx >= k, pltpu.roll(x, k, 0), 0.0)
        k *= 2
    x = x + jnp.where(lane >= LN_FT, carry_ref[...], 0.0)
    carry_ref[...] = x[ts - 1:ts, :]
    g_ref[...] = x
    gt_ref[...] = x.T


def _prep(small, bias_lanes, alog_lanes, s, chunk):
    m = small.shape[0]
    ts = min(512, m)
    if s >= ts:
        seg, tiles_per_seq = ts, s // ts
    else:
        seg, tiles_per_seq = s, 1
    return pl.pallas_call(
        functools.partial(_prep_kernel, chunk=chunk, seg=seg, tiles_per_seq=tiles_per_seq),
        grid=(m // ts,),
        in_specs=[pl.BlockSpec((ts, LANES), lambda i: (i, 0)), _const_spec((1, LANES)), _const_spec((1, LANES))],
        out_specs=[pl.BlockSpec((ts, LANES), lambda i: (i, 0)), pl.BlockSpec((LANES, ts), lambda i: (0, i))],
        out_shape=[jax.ShapeDtypeStruct((m, LANES), F32), jax.ShapeDtypeStruct((LANES, m), F32)],
        scratch_shapes=[pltpu.VMEM((1, LANES), F32)],
        compiler_params=_cparams(("arbitrary",)),
        name="prep",
    )(small, bias_lanes, alog_lanes)


def _mlstm_kernel(q_ref, k_ref, v_ref, mo_ref, g_ref, gt_ref, gh_ref, c0_ref, m0_ref,
                  y_ref, c_ref, m_ref, *, chunk, gt3d, mm):
    j = pl.program_id(1)

    @pl.when(j == 0)
    def _():
        c_ref[...] = c0_ref[...]
        m_ref[...] = m0_ref[...]

    ts = q_ref.shape[0]
    ln = chunk
    gt = gt_ref[0] if gt3d else gt_ref[...]
    rr = lax.broadcasted_iota(jnp.int32, (ln, ln), 0)
    cc = lax.broadcasted_iota(jnp.int32, (ln, ln), 1)
    causal = cc <= rr
    ones = jnp.ones((ln, DV_M), F32)
    heads = range(NH_M)
    for c in range(ts // ln):
        lo, hi = c * ln, (c + 1) * ln
        g = g_ref[lo:hi, :]
        qb = [q_ref[lo:hi, h * DQK_M:(h + 1) * DQK_M].astype(mm) for h in heads]
        kb = [k_ref[lo:hi, h * DQK_M:(h + 1) * DQK_M].astype(mm) for h in heads]
        vf = [v_ref[lo:hi, h * DV_M:(h + 1) * DV_M] for h in heads]
        b_c = [g[:, LN_BM + h:LN_BM + h + 1] for h in heads]
        m_st = [m_ref[0, h][:, 0:1] for h in heads]
        st = [c_ref[0, h] for h in heads]
        qk = [_dot_nt(qb[h], kb[h]) for h in heads]
        qc = [_dot(qb[h], st[h].astype(mm)) for h in heads]
        dm = [jnp.where(causal, b_c[h] - gt[LN_BM + h:LN_BM + h + 1, lo:hi]
                        + gt[LN_LOGI + h:LN_LOGI + h + 1, lo:hi], -jnp.inf) for h in heads]
        inter = [b_c[h] + m_st[h] for h in heads]
        m_t = [jnp.maximum(inter[h], jnp.max(dm[h], axis=1, keepdims=True)) for h in heads]
        s = [(qk[h] * jnp.exp(dm[h] - m_t[h])).astype(mm) for h in heads]
        sv = [_dot(s[h], jnp.concatenate([vf[h], ones], axis=1).astype(mm)) for h in heads]
        m_new = [m_t[h][ln - 1:ln, :] for h in heads]
        wk = [jnp.exp(b_c[h][ln - 1:ln, :] - b_c[h] + g[:, LN_LOGI + h:LN_LOGI + h + 1] - m_new[h]) for h in heads]
        wv = [jnp.concatenate([wk[h] * vf[h], jnp.broadcast_to(wk[h], (ln, DV_M))], axis=1).astype(mm) for h in heads]
        upd = [_dot_tn(kb[h], wv[h]) for h in heads]
        for h in heads:
            decay = jnp.exp(b_c[h][ln - 1:ln, :] + m_st[h] - m_new[h])
            c_ref[0, h] = decay * st[h] + upd[h]
            m_ref[0, h] = jnp.broadcast_to(m_new[h], (1, LANES))
        for h in heads:
            both = sv[h] + jnp.exp(inter[h] - m_t[h]) * qc[h]
            num, den = both[:, :DV_M], both[:, DV_M:]
            hv = num / jnp.maximum(jnp.abs(den), jnp.exp(-m_t[h]))
            hm = _rms(hv, gh_ref[:, h * DV_M:(h + 1) * DV_M])
            y = hm * _sigmoid(mo_ref[lo:hi, h * DV_M:(h + 1) * DV_M])
            y_ref[lo:hi, h * DV_M:(h + 1) * DV_M] = y.astype(y_ref.dtype)


def _gt_spec(b, s, ts, nj, gt3d):
    if gt3d:
        return pl.BlockSpec((1, LANES, s), lambda i, j: (i, 0, 0))
    return pl.BlockSpec((LANES, ts), lambda i, j: (0, i * nj + j))


def _mlstm(u, g, gt, g_mhead, c0n, m0, b, s, chunk, mm):
    ts = min(256, s)
    nj = s // ts
    m = b * s
    gt3d = gt.ndim == 3
    row = lambda i, j: (i * nj + j, 0)
    st = lambda *shape: pl.BlockSpec((1,) + shape, lambda i, j: (i,) + (0,) * len(shape))
    return pl.pallas_call(
        functools.partial(_mlstm_kernel, chunk=chunk, gt3d=gt3d, mm=mm),
        grid=(b, nj),
        in_specs=[pl.BlockSpec((ts, NH_M * DQK_M), row), pl.BlockSpec((ts, NH_M * DQK_M), row),
                  pl.BlockSpec((ts, W_M), row), pl.BlockSpec((ts, W_M), row),
                  pl.BlockSpec((ts, LANES), row), _gt_spec(b, s, ts, nj, gt3d),
                  _const_spec((1, W_M)),
                  st(NH_M, DQK_M, 2 * DV_M), st(NH_M, 1, LANES)],
        out_specs=[pl.BlockSpec((ts, W_M), row),
                   st(NH_M, DQK_M, 2 * DV_M), st(NH_M, 1, LANES)],
        out_shape=[jax.ShapeDtypeStruct((m, W_M), BF16),
                   jax.ShapeDtypeStruct((b, NH_M, DQK_M, 2 * DV_M), F32),
                   jax.ShapeDtypeStruct((b, NH_M, 1, LANES), F32)],
        compiler_params=_cparams(("arbitrary", "arbitrary")),
        name="mlstm",
    )(u["mq"], u["mk"], u["mv"], u["mo"], g, gt, g_mhead.reshape(1, W_M), c0n, m0)


def _ssd_expansion(ln):
    stride = max(ln, LANES)
    r = lax.broadcasted_iota(jnp.int32, (LANES, 2 * D_INNER + NH_S * stride), 0)
    c = lax.broadcasted_iota(jnp.int32, (LANES, 2 * D_INNER + NH_S * stride), 1)
    src = jnp.where(c < D_INNER, LN_DT + c // P_S,
                    jnp.where(c < 2 * D_INNER, LN_CUM + (c - D_INNER) // P_S, LN_CUM + (c - 2 * D_INNER) // stride))
    return jnp.where(r == src, 1.0, 0.0).astype(BF16)


def _ssd_kernel(xbc_ref, sz_ref, g_ref, gt_ref, ex_ref, cw_ref, cb_ref, dsk_ref, gs_ref, conv0_ref, h0_ref,
                y_ref, hst_ref, tail_ref, *, chunk, gt3d, mm):
    j = pl.program_id(1)

    @pl.when(j == 0)
    def _():
        hst_ref[...] = h0_ref[...]
        tail_ref[...] = conv0_ref[0]

    x = xbc_ref[...]
    ts = x.shape[0]
    ln = chunk
    prev = tail_ref[...]
    row8 = lax.broadcasted_iota(jnp.int32, prev.shape, 0)
    acc = cb_ref[...] + cw_ref[CONV_W - 1:CONV_W, :] * x
    for k in range(1, CONV_W):
        rolled = pltpu.roll(x, k, 0)
        first = jnp.where(row8 < k, pltpu.roll(prev, k, 0), rolled[:SUBLANES])
        shifted = first if ts == SUBLANES else jnp.concatenate([first, rolled[SUBLANES:]], axis=0)
        acc = acc + cw_ref[CONV_W - 1 - k:CONV_W - k, :] * shifted
    tail_ref[...] = x[ts - SUBLANES:, :]
    xa = acc * _sigmoid(acc)

    gt = gt_ref[0] if gt3d else gt_ref[...]
    rr = lax.broadcasted_iota(jnp.int32, (ln, ln), 0)
    cc = lax.broadcasted_iota(jnp.int32, (ln, ln), 1)
    causal = cc <= rr
    hpg = NH_S // N_GROUPS
    for c in range(ts // ln):
        lo, hi = c * ln, (c + 1) * ln
        g = g_ref[lo:hi, :]
        heads, pairs = range(NH_S), range(NH_S // 2)
        ex_dt = BF16 if ln % (2 * SUBLANES) == 0 else F32
        full = sum(_dot(part.astype(ex_dt), ex_ref[...].astype(ex_dt)) for part in _split3(g))
        dt_full, cum_full = full[:, :D_INNER], full[:, D_INNER:2 * D_INNER]
        stride = max(ln, LANES)
        cum_b = [full[:, 2 * D_INNER + h * stride:2 * D_INNER + h * stride + ln] for h in heads]
        xs = xa[lo:hi, :D_INNER]
        xdt = xs * dt_full
        wkx = jnp.exp(cum_full[ln - 1:ln, :] - cum_full) * xdt
        bmg = [xa[lo:hi, D_INNER + grp * D_STATE:D_INNER + (grp + 1) * D_STATE].astype(mm) for grp in range(N_GROUPS)]
        co = D_INNER + N_GROUPS * D_STATE
        cmg = [xa[lo:hi, co + grp * D_STATE:co + (grp + 1) * D_STATE].astype(mm) for grp in range(N_GROUPS)]
        cbm = [_dot_nt(cmg[grp], bmg[grp]) for grp in range(N_GROUPS)]
        psl = [slice(hp * 2 * P_S, (hp + 1) * 2 * P_S) for hp in pairs]
        pgrp = [2 * hp // hpg for hp in pairs]
        hst = [jnp.concatenate([hst_ref[0, 2 * hp], hst_ref[0, 2 * hp + 1]], axis=0) for hp in pairs]
        ych = [_dot_nt(cmg[pgrp[hp]], hst[hp].astype(mm)) for hp in pairs]
        ldec = [jnp.exp(jnp.where(causal, cum_b[h] - gt[LN_CUM + h:LN_CUM + h + 1, lo:hi], -jnp.inf)) for h in heads]
        xdt_p = [xdt[:, psl[hp]].astype(mm) for hp in pairs]
        ycb = [_dot((cbm[h // hpg] * ldec[h]).astype(mm), xdt_p[h // 2]) for h in heads]
        upd = [_dot_tn(wkx[:, psl[hp]].astype(mm), bmg[pgrp[hp]]) for hp in pairs]
        low_l = lax.broadcasted_iota(jnp.int32, (ln, 2 * P_S), 1) < P_S
        low_r = lax.broadcasted_iota(jnp.int32, (2 * P_S, D_STATE), 0) < P_S
        ys = []
        for hp in pairs:
            d0 = jnp.exp(g[ln - 1:ln, LN_CUM + 2 * hp:LN_CUM + 2 * hp + 1])
            d1 = jnp.exp(g[ln - 1:ln, LN_CUM + 2 * hp + 1:LN_CUM + 2 * hp + 2])
            new = jnp.where(low_r, d0, d1) * hst[hp] + upd[hp]
            hst_ref[0, 2 * hp] = new[:P_S]
            hst_ref[0, 2 * hp + 1] = new[P_S:]
            ys.append(jnp.where(low_l, ycb[2 * hp], ycb[2 * hp + 1]) + jnp.exp(cum_full[:, psl[hp]]) * ych[hp]
                      + dsk_ref[:, psl[hp]] * xs[:, psl[hp]])
        yy = jnp.concatenate(ys, axis=1)
        z = sz_ref[lo:hi, :]
        y_ref[lo:hi, :] = _rms(yy * (z * _sigmoid(z)), gs_ref[...]).astype(y_ref.dtype)


def _ssd(u, g, gt, conv_w, conv_b, d_skip, g_ssm, conv0, h0, b, s, chunk, mm):
    ts = min(256, s)
    nj = s // ts
    m = b * s
    gt3d = gt.ndim == 3
    ex = _ssd_expansion(chunk)
    row = lambda i, j: (i * nj + j, 0)
    st = lambda *shape: pl.BlockSpec((1,) + shape, lambda i, j: (i,) + (0,) * len(shape))
    return pl.pallas_call(
        functools.partial(_ssd_kernel, chunk=chunk, gt3d=gt3d, mm=mm),
        grid=(b, nj),
        in_specs=[pl.BlockSpec((ts, CONV_CH), row), pl.BlockSpec((ts, D_INNER), row),
                  pl.BlockSpec((ts, LANES), row), _gt_spec(b, s, ts, nj, gt3d),
                  _const_spec(ex.shape),
                  _const_spec((CONV_W, CONV_CH)), _const_spec((1, CONV_CH)),
                  _const_spec((1, D_INNER)), _const_spec((1, D_INNER)),
                  st(SUBLANES, CONV_CH), st(NH_S, P_S, D_STATE)],
        out_specs=[pl.BlockSpec((ts, D_INNER), row), st(NH_S, P_S, D_STATE)],
        out_shape=[jax.ShapeDtypeStruct((m, D_INNER), BF16),
                   jax.ShapeDtypeStruct((b, NH_S, P_S, D_STATE), F32)],
        scratch_shapes=[pltpu.VMEM((SUBLANES, CONV_CH), F32)],
        compiler_params=_cparams(("arbitrary", "arbitrary")),
        name="ssd",
    )(u["sxbc"], u["sz"], g, gt, ex, conv_w, conv_b.reshape(1, CONV_CH),
      jnp.repeat(d_skip, P_S).reshape(1, D_INNER), g_ssm.reshape(1, D_INNER), conv0, h0)


def _split3(x):
    hi = x.astype(BF16).astype(F32)
    r = x - hi
    mid = r.astype(BF16).astype(F32)
    return hi, mid, r - mid


FOX_EXT = 16
FOX_VROWS = 2 * DH_F + FOX_EXT
FOX_SUB = 256
FOX_WIDE = 4
FOXS_PAGES = 16
FOX_SKIP = 112.0


def _foxp_kernel(q_ref, kt_ref, vt_ref, g_ref, ftr_ref, o_ref, kaug_ref, vaug_ref, stat_ref, *, tq):
    hp = pl.program_id(1)
    i = pl.program_id(2)
    hw = 2 * DH_F
    s_len = kaug_ref.shape[1]

    wide = FOX_WIDE * tq
    lane_row = lax.broadcasted_iota(jnp.int32, (1, LANES), 1)

    @pl.when(i == 0)
    def _():
        kb = kt_ref[0].astype(BF16)
        kaug_ref[0:hw, :] = kb
        vaug_ref[0:hw, :] = vt_ref[0].astype(BF16)
        ksq = kb.astype(F32) * kb.astype(F32)
        r = lax.broadcasted_iota(jnp.int32, (FOX_EXT, s_len), 0)
        ext = jnp.where(r < 3, 1.0, 0.0)
        for hh in range(2):
            ft_row = ftr_ref[pl.ds(hp * 2 + hh, 1), :]
            parts = _split3(ft_row)
            for pi, part in enumerate(parts):
                ext = jnp.where(r == 3 + 3 * hh + pi, -part, ext)
            ftq = jnp.full((1, LANES), -jnp.inf, F32)
            for jb in range(s_len // wide):
                ftq = jnp.where(lane_row == jb, ft_row[:, wide * (jb + 1) - 1:wide * (jb + 1)], ftq)
            stat_ref[hh:hh + 1, :] = ftq
            kn2 = jnp.sum(ksq[hh * DH_F:(hh + 1) * DH_F, :], axis=0, keepdims=True)
            stat_ref[2 + hh:3 + hh, :] = jnp.broadcast_to(jnp.sqrt(jnp.max(kn2, axis=1, keepdims=True)), (1, LANES))
        kaug_ref[hw:hw + FOX_EXT, :] = ext.astype(BF16)
        kaug_ref[hw + FOX_EXT:, :] = jnp.zeros((kaug_ref.shape[0] - hw - FOX_EXT, s_len), BF16)
        vaug_ref[hw:, :] = jnp.where(r == 0, 1.0, 0.0).astype(BF16)

    g = g_ref[...]
    q = q_ref[...]
    lane = lax.broadcasted_iota(jnp.int32, (tq, LANES), 1)
    low = lane < DH_F
    qaug = []
    qsq = q.astype(F32) * q.astype(F32)
    skippable = None
    for hh in range(2):
        mine = low if hh == 0 else ~low
        ft_c = jnp.sum(jnp.where(lane == LN_FT + hp * 2 + hh, g, 0.0), axis=1, keepdims=True)
        hi, mid, lo = _split3(ft_c)
        ext = jnp.where(lane == 0, hi, jnp.where(lane == 1, mid, jnp.where(lane == 2, lo, 0.0)))
        ext = jnp.where((lane >= 3 + 3 * hh) & (lane < 6 + 3 * hh), 1.0, ext)
        qm = jnp.where(mine, q, jnp.zeros_like(q))
        qaug.append(jnp.concatenate([qm, ext.astype(BF16)], axis=1))
        qmax = jnp.sqrt(jnp.max(jnp.sum(jnp.where(mine, qsq, 0.0), axis=1, keepdims=True), axis=0, keepdims=True))
        thr = ft_c[0:1, :] + 2.0 * qmax * stat_ref[2 + hh:3 + hh, 0:1] + FOX_SKIP
        cond = stat_ref[hh:hh + 1, :] > thr
        skippable = cond if skippable is None else skippable & cond
    n_skip = jnp.sum(jnp.where(skippable, 1, 0).astype(jnp.int32))

    rr = lax.broadcasted_iota(jnp.int32, (tq, tq), 0)
    cc = lax.broadcasted_iota(jnp.int32, (tq, tq), 1)
    causal = cc <= rr

    def block(carry, start, width, diag):
        subs = [pl.ds(pl.multiple_of(start + c * FOX_SUB, FOX_SUB), FOX_SUB) for c in range(width // FOX_SUB)]
        ms, ls, acc = carry
        new_m, new_l, alphas, pvs = [], [], [], []
        scores = []
        for hh in range(2):
            ss = []
            for c, sub in enumerate(subs):
                s = _dot(qaug[hh], kaug_ref[:, sub])
                dc = c * FOX_SUB - (width - tq)
                if diag and dc >= 0:
                    s = jnp.where(causal[:, dc:dc + FOX_SUB], s, -jnp.inf)
                ss.append(s)
            scores.append(ss)
        for hh in range(2):
            m_blk = jnp.max(functools.reduce(jnp.maximum, scores[hh]), axis=1, keepdims=True)
            new_m.append(jnp.maximum(ms[hh], m_blk))
            alphas.append(jnp.exp(ms[hh] - new_m[hh]))
        pv = [None, None]
        for c in range(len(subs)):
            for hh in range(2):
                p = jnp.exp(scores[hh][c] - new_m[hh]).astype(BF16)
                d = _dot_nt(p, vaug_ref[:, subs[c]])
                pv[hh] = d if pv[hh] is None else pv[hh] + d
        for hh in range(2):
            new_l.append(alphas[hh] * ls[hh] + pv[hh][:, hw:hw + 1])
            pvs.append(pv[hh][:, :hw])
        acc = jnp.where(low, alphas[0], alphas[1]) * acc + jnp.where(low, pvs[0], pvs[1])
        return tuple(new_m), tuple(new_l), acc

    neg = jnp.full((tq, 1), -jnp.inf, F32)
    zero = jnp.zeros((tq, 1), F32)
    init = ((neg, neg), (zero, zero), jnp.zeros((tq, hw), F32))
    carry = lax.fori_loop(n_skip, i // FOX_WIDE, lambda jb, c: block(c, jb * wide, wide, False), init)
    tail_start = (i // FOX_WIDE) * wide

    def tail(r):
        return lambda c: block(c, tail_start, (r + 1) * tq, True)

    def pick(lo, hi):
        if hi - lo == 1:
            return tail(lo)
        mid = (lo + hi) // 2
        return lambda c: lax.cond(i % FOX_WIDE < mid, pick(lo, mid), pick(mid, hi), c)

    _, ls, acc = pick(0, FOX_WIDE)(carry)
    o_ref[...] = (acc / jnp.where(low, ls[0], ls[1])).astype(o_ref.dtype)


def _fox_prompt(u, g, gt, b, s):
    tq = min(512, s)
    nq = s // tq
    m = b * s
    hw = 2 * DH_F
    return pl.pallas_call(
        functools.partial(_foxp_kernel, tq=tq),
        grid=(b, NH_F // 2, nq),
        in_specs=[pl.BlockSpec((tq, hw), lambda bb, hp, i: (bb * nq + i, hp)),
                  pl.BlockSpec((1, hw, s), lambda bb, hp, i: (bb, hp, 0)),
                  pl.BlockSpec((1, hw, s), lambda bb, hp, i: (bb, hp, 0)),
                  pl.BlockSpec((tq, LANES), lambda bb, hp, i: (bb * nq + i, 0)),
                  pl.BlockSpec((SUBLANES, s), lambda bb, hp, i: (LN_FT // SUBLANES, bb))],
        out_specs=pl.BlockSpec((tq, hw), lambda bb, hp, i: (bb * nq + i, hp)),
        out_shape=jax.ShapeDtypeStruct((m, W_F), BF16),
        scratch_shapes=[pltpu.VMEM((2 * hw, s), BF16), pltpu.VMEM((FOX_VROWS, s), BF16),
                        pltpu.VMEM((SUBLANES, LANES), F32)],
        compiler_params=_cparams(("arbitrary", "arbitrary", "arbitrary")),
        name="fox_prompt",
    )(u["fq"], u["fk"], u["fv"], g, gt)


def _foxs_kernel(pt_ref, q_ref, kn_ref, vn_ref, gt_ref, *rest, pages):
    del pt_ref
    k_refs, v_refs, lf_refs = rest[:pages], rest[pages:2 * pages], rest[2 * pages:3 * pages]
    o_ref, qbd_ref, m_ref, l_ref, acc_ref, carry_ref, kcat_ref, vcat_ref = rest[3 * pages:]
    gi = pl.program_id(1)
    t = q_ref.shape[0]
    ht = NH_F * t
    page = k_refs[0].shape[2]

    def expand_heads(x):
        return jnp.broadcast_to(x[:, None, :], (NH_F, t, x.shape[1])).reshape(ht, x.shape[1])

    @pl.when(gi == 0)
    def _():
        q = q_ref[...].astype(F32)
        qt = jnp.broadcast_to(q[None], (NH_F, t, W_F)).reshape(ht, W_F)
        rh = lax.broadcasted_iota(jnp.int32, (ht, W_F), 0) // t
        lh = lax.broadcasted_iota(jnp.int32, (ht, W_F), 1) // DH_F
        qbd_ref[...] = jnp.where(rh == lh, qt, 0.0)
        m_ref[...] = jnp.full_like(m_ref, -jnp.inf)
        l_ref[...] = jnp.zeros_like(l_ref)
        acc_ref[...] = jnp.zeros_like(acc_ref)
        carry_ref[...] = jnp.zeros_like(carry_ref)

    cn_rows = expand_heads(gt_ref[0][LN_FT:LN_FT + NH_F, :])
    tq_idx = lax.broadcasted_iota(jnp.int32, (ht, t), 0) % t
    tk_idx = lax.broadcasted_iota(jnp.int32, (ht, t), 1)
    cn_col = jnp.sum(jnp.where(tq_idx == tk_idx, cn_rows, 0.0), axis=1, keepdims=True)

    qbd = qbd_ref[...]
    qbd_b = qbd.astype(BF16)

    def update(s, pv):
        m = m_ref[...]
        m_new = jnp.maximum(m, jnp.max(s, axis=1, keepdims=True))
        p = jnp.exp(s - m_new)
        alpha = jnp.exp(m - m_new)
        l_ref[...] = alpha * l_ref[...] + jnp.sum(p, axis=1, keepdims=True)
        acc_ref[...] = alpha * acc_ref[...] + pv(p)
        m_ref[...] = m_new

    lf_all = jnp.concatenate([lf_refs[pi][0] for pi in range(pages)], axis=0)
    tri = jnp.where(lax.broadcasted_iota(jnp.int32, (page, page), 0)
                    <= lax.broadcasted_iota(jnp.int32, (page, page), 1), 1.0, 0.0).astype(BF16)
    pre_all = sum(_dot(part.astype(BF16), tri) for part in _split3(lf_all))
    carry = carry_ref[...]
    rs = [None] * pages
    for pi in reversed(range(pages)):
        pre = pre_all[pi * NH_F:(pi + 1) * NH_F, :]
        tot = pre[:, page - 1:page]
        rs[pi] = expand_heads(carry + tot - pre)
        carry = carry + tot
        kcat_ref[:, pi * page:(pi + 1) * page] = k_refs[pi][0].astype(BF16)
        vcat_ref[:, pi * page:(pi + 1) * page] = v_refs[pi][0].astype(BF16)
    carry_ref[...] = carry
    s_all = _dot(qbd_b, kcat_ref[...]) + jnp.concatenate(rs, axis=1) + cn_col
    update(s_all, lambda p: _dot_nt(p.astype(BF16), vcat_ref[...]))

    @pl.when(gi == pl.num_programs(1) - 1)
    def _():
        s_new = _dot_nt(qbd, kn_ref[...]) + (cn_col - cn_rows)
        s_new = jnp.where(tk_idx <= tq_idx, s_new, -jnp.inf)
        update(s_new, lambda p: _dot(p, vn_ref[...]))
        o = acc_ref[...] / l_ref[...]
        lh = lax.broadcasted_iota(jnp.int32, (t, W_F), 1) // DH_F
        y = jnp.zeros((t, W_F), F32)
        for h in range(NH_F):
            y = y + jnp.where(lh == h, o[h * t:(h + 1) * t, :], 0.0)
        o_ref[...] = y.astype(o_ref.dtype)


def _fox_sample(u, gt3, page_table, ck_t, cv_t, clf_t, page_base, b, t):
    n_pages = page_table.shape[1]
    page = ck_t.shape[2]
    pages = FOXS_PAGES
    while n_pages % pages:
        pages //= 2
    ng = n_pages // pages
    ht = NH_F * t

    def page_spec(width_shape, pi):
        return pl.BlockSpec((1,) + width_shape,
                            lambda bb, gi, pt: (page_base + pt[bb, (ng - 1 - gi) * pages + pi], 0, 0))

    tok = pl.BlockSpec((t, W_F), lambda bb, gi, pt: (bb, 0))
    in_specs = [tok, tok, tok, pl.BlockSpec((1, LANES, t), lambda bb, gi, pt: (bb, 0, 0))]
    in_specs += [page_spec((W_F, page), pi) for pi in range(pages)]
    in_specs += [page_spec((W_F, page), pi) for pi in range(pages)]
    in_specs += [page_spec((NH_F, page), pi) for pi in range(pages)]
    grid_spec = pltpu.PrefetchScalarGridSpec(
        num_scalar_prefetch=1,
        grid=(b, ng),
        in_specs=in_specs,
        out_specs=pl.BlockSpec((t, W_F), lambda bb, gi, pt: (bb, 0)),
        scratch_shapes=[pltpu.VMEM((ht, W_F), F32), pltpu.VMEM((ht, 1), F32), pltpu.VMEM((ht, 1), F32),
                        pltpu.VMEM((ht, W_F), F32), pltpu.VMEM((NH_F, 1), F32),
                        pltpu.VMEM((W_F, pages * page), BF16), pltpu.VMEM((W_F, pages * page), BF16)],
    )
    return pl.pallas_call(
        functools.partial(_foxs_kernel, pages=pages),
        grid_spec=grid_spec,
        out_shape=jax.ShapeDtypeStruct((b * t, W_F), BF16),
        compiler_params=_cparams(("arbitrary", "arbitrary")),
        name="fox_sample",
    )(page_table, u["fq"], u["fk"], u["fv"], gt3, *([ck_t] * pages), *([cv_t] * pages), *([clf_t] * pages))


def _merge_kernel(x_ref, sh_ref, sc_ref, gt_ref, gpre_ref, gpost_ref, ym_ref, ys_ref, yf_ref,
                  wg_ref, wm_ref, ws_ref, wf_ref, wo_ref, o_ref):
    x = x_ref[...]
    tb, ts, d = x.shape
    h = _rms(x, gpre_ref[...]) * (1.0 + sc_ref[...]) + sh_ref[...]
    hb = h.reshape(tb * ts, d).astype(BF16)
    merged = None
    for bi, (y_ref, w_ref) in enumerate(((ym_ref, wm_ref), (ys_ref, ws_ref), (yf_ref, wf_ref))):
        gate = _sigmoid(_dot(hb, wg_ref[:, bi * d:(bi + 1) * d]))
        term = gate * _dot(y_ref[...], w_ref[...])
        merged = term if merged is None else merged + term
    out = _dot(merged.astype(BF16), wo_ref[...]).reshape(tb, ts, d)
    o_ref[...] = x + gt_ref[...] * _rms(out, gpost_ref[...])


def _merge(x, ada, g_pre, g_post, ym, ys, yf, wg, wm, ws, wf, wo):
    b, s, d = x.shape
    tb, ts = _x_tiles(b, s, 256)
    tm = tb * ts
    nj = s // ts
    row = lambda i, j: (i * nj + j, 0)
    wspec = lambda shape: pl.BlockSpec(shape, lambda i, j: (0, 0), pipeline_mode=pl.Buffered(1))
    return pl.pallas_call(
        _merge_kernel,
        grid=(b // tb, nj),
        in_specs=[pl.BlockSpec((tb, ts, d), lambda i, j: (i, j, 0)),
                  _ada_spec(tb, d, 0), _ada_spec(tb, d, 1), _ada_spec(tb, d, 2),
                  _const_spec((1, d)), _const_spec((1, d)),
                  pl.BlockSpec((tm, W_M), row), pl.BlockSpec((tm, D_INNER), row), pl.BlockSpec((tm, W_F), row),
                  wspec((d, 3 * d)), wspec((W_M, d)), wspec((D_INNER, d)), wspec((W_F, d)), wspec((d, d))],
        out_specs=pl.BlockSpec((tb, ts, d), lambda i, j: (i, j, 0)),
        out_shape=jax.ShapeDtypeStruct((b, s, d), F32),
        compiler_params=_cparams(("arbitrary", "arbitrary")),
        name="merge",
    )(x, ada, ada, ada, g_pre.reshape(1, d), g_post.reshape(1, d), ym, ys, yf, wg, wm, ws, wf, wo)


def _mlp_kernel(x_ref, sh_ref, sc_ref, gt_ref, gpre_ref, gpost_ref, wu_ref, wd_ref, o_ref):
    x = x_ref[...]
    tb, ts, d = x.shape
    h = _rms(x, gpre_ref[...]) * (1.0 + sc_ref[...]) + sh_ref[...]
    hb = h.reshape(tb * ts, d).astype(BF16)
    up = jnp.maximum(_dot(hb, wu_ref[...]), 0.0)
    f = _dot((up * up).astype(BF16), wd_ref[...]).reshape(tb, ts, d)
    o_ref[...] = x + gt_ref[...] * _rms(f, gpost_ref[...])


def _mlp(x, ada, g_pre, g_post, wu, wd):
    b, s, d = x.shape
    tb, ts = _x_tiles(b, s, 256)
    nj = s // ts
    dff = wu.shape[1]
    wspec = lambda shape: pl.BlockSpec(shape, lambda i, j: (0, 0), pipeline_mode=pl.Buffered(1))
    return pl.pallas_call(
        _mlp_kernel,
        grid=(b // tb, nj),
        in_specs=[pl.BlockSpec((tb, ts, d), lambda i, j: (i, j, 0)),
                  _ada_spec(tb, d, 3), _ada_spec(tb, d, 4), _ada_spec(tb, d, 5),
                  _const_spec((1, d)), _const_spec((1, d)),
                  wspec((d, dff)), wspec((dff, d))],
        out_specs=pl.BlockSpec((tb, ts, d), lambda i, j: (i, j, 0)),
        out_shape=jax.ShapeDtypeStruct((b, s, d), F32),
        compiler_params=_cparams(("arbitrary", "arbitrary")),
        name="mlp",
    )(x, ada, ada, ada, g_pre.reshape(1, d), g_post.reshape(1, d), wu, wd)


def _split_w_in(w):
    sizes = (NH_M * DQK_M, NH_M * DQK_M, W_M, NH_M, NH_M, W_M, D_INNER, CONV_CH, NH_S, W_F, W_F, W_F, NH_F)
    names = ("mq", "mk", "mv", "mi", "mf", "mo", "sz", "sxbc", "sdt", "fq", "fk", "fv", "ff")
    cols, off = {}, 0
    for nm, sz in zip(names, sizes):
        cols[nm] = w[:, off:off + sz]
        off += sz
    gates = w[:, off:]
    d = w.shape[0]
    zeros = lambda n: jnp.zeros((d, n), w.dtype)
    small = jnp.concatenate(
        [cols["mi"], cols["mf"], cols["sdt"], cols["ff"], cols["mf"], zeros(LN_CUM - LN_BM - NH_M),
         cols["sdt"], cols["ff"], zeros(LANES - LN_FT - NH_F)], axis=1)
    proj = jnp.concatenate([cols[sg[0]] for sg in SEGS[:-1]] + [small], axis=1)
    kv = jnp.concatenate([cols["fk"], cols["fv"]], axis=1)
    return proj.astype(BF16), kv.astype(BF16), gates.astype(BF16)


def _gate_lanes(b_mgate, dt_bias, b_ffox, a_log):
    z = lambda n: jnp.zeros((n,), F32)
    bias = jnp.concatenate([b_mgate, dt_bias, b_ffox, b_mgate[NH_M:], z(LN_CUM - LN_BM - NH_M),
                            dt_bias, b_ffox, z(LANES - LN_FT - NH_F)])
    alog = jnp.concatenate([z(LN_CUM), a_log, z(LANES - LN_CUM - NH_S)])
    return bias.reshape(1, LANES), alog.reshape(1, LANES)


def _mixer(x, ada, lw, state, attend, chunk, mm, kv_t):
    b, s, d = x.shape
    u = _inproj(x, ada, lw["g_pre_mix"], lw["w_proj"], lw["w_kv"].T if kv_t else lw["w_kv"], kv_t)
    g, gt = _prep(u["small"], lw["bias_lanes"], lw["alog_lanes"], s, chunk)
    if s % LANES:
        gt = gt.reshape(LANES, b, s).transpose(1, 0, 2)
    c0n, m0, conv0, h0 = state
    ym, c_n, m_new = _mlstm(u, g, gt, lw["g_mhead"], c0n, m0, b, s, chunk, mm)
    ys, h_new = _ssd(u, g, gt, lw["conv_w"], lw["conv_b"], lw["d_skip"], lw["g_ssm"], conv0, h0, b, s, chunk, mm)
    yf = attend(u, g, gt)
    x1 = _merge(x, ada, lw["g_pre_mix"], lw["g_post_mix"], ym, ys, yf,
                lw["w_gates"], lw["w_br_m"], lw["w_br_s"], lw["w_br_f"], lw["w_out"])
    x2 = _mlp(x1, ada, lw["g_pre_mlp"], lw["g_post_mlp"], lw["w_up"], lw["w_down"])
    if kv_t:
        rows = lambda a: jnp.transpose(a.reshape(b, NH_F, DH_F, s), (0, 3, 1, 2))
    else:
        rows = lambda a: a.reshape(b, s, NH_F, DH_F)
    outs = (rows(u["fk"]), rows(u["fv"]),
            g[:, LN_FF:LN_FF + NH_F].reshape(b, s, NH_F),
            jnp.swapaxes(c_n[..., :DV_M], -1, -2), c_n[..., DV_M], m_new[:, :, 0, 0],
            u["sxbc"].reshape(b, s, CONV_CH)[:, s - (CONV_W - 1):, :], h_new)
    return x2, outs


def kernel(x_prompt, x_sample, cache_k, cache_v, cache_logf, state_mlstm_C, state_mlstm_n, state_mlstm_m,
           state_conv, state_ssm, page_table, c_prompt, c_sample, w_ada, b_ada, g_pre_mix, g_post_mix, w_in,
           b_mgate, b_ffox, g_mhead, conv_w, conv_b, dt_bias, a_log, d_skip, g_ssm, w_br_m, w_br_s, w_br_f,
           w_out, g_pre_mlp, g_post_mlp, w_up, w_down):
    depth = w_in.shape[0]
    bp, sp, d = x_prompt.shape
    bs, ss, _ = x_sample.shape
    n_phys, page = cache_k.shape[1], cache_k.shape[2]

    pad = (-(bp + bs)) % SUBLANES
    c_all = jnp.concatenate([c_prompt, c_sample, jnp.zeros((pad, d), F32)], axis=0)
    ada_all = _ada(c_all, w_ada, b_ada)

    ck_t = jnp.transpose(cache_k, (0, 1, 3, 4, 2)).reshape(depth * n_phys, W_F, page)
    cv_t = jnp.transpose(cache_v, (0, 1, 3, 4, 2)).reshape(depth * n_phys, W_F, page)
    clf_t = jnp.transpose(cache_logf, (0, 1, 3, 2)).reshape(depth * n_phys, NH_F, page)

    chunk_p = SCAN_CHUNK if sp % SCAN_CHUNK == 0 else (CHUNK if sp % CHUNK == 0 else sp)
    chunk_s = SCAN_CHUNK if ss % SCAN_CHUNK == 0 else (CHUNK if ss % CHUNK == 0 else ss)
    yp, ys = x_prompt, x_sample
    res_p, res_s = [], []
    for l in range(depth):
        w_proj, w_kv, w_gates = _split_w_in(w_in[l])
        bias_lanes, alog_lanes = _gate_lanes(b_mgate[l], dt_bias[l], b_ffox[l], a_log[l])
        lw = dict(w_proj=w_proj, w_kv=w_kv, w_gates=w_gates, bias_lanes=bias_lanes, alog_lanes=alog_lanes,
                  g_pre_mix=g_pre_mix[l], g_post_mix=g_post_mix[l], g_mhead=g_mhead[l],
                  conv_w=conv_w[l], conv_b=conv_b[l], d_skip=d_skip[l], g_ssm=g_ssm[l],
                  w_br_m=w_br_m[l].astype(BF16), w_br_s=w_br_s[l].astype(BF16), w_br_f=w_br_f[l].astype(BF16),
                  w_out=w_out[l].astype(BF16), g_pre_mlp=g_pre_mlp[l], g_post_mlp=g_post_mlp[l],
                  w_up=w_up[l].astype(BF16), w_down=w_down[l].astype(BF16))
        ada_p = ada_all[l, :bp][:, None, :]
        ada_s = ada_all[l, bp:bp + bs][:, None, :]

        zero_state = (jnp.zeros((bp, NH_M, DQK_M, 2 * DV_M), F32),
                      jnp.zeros((bp, NH_M, 1, LANES), F32), jnp.zeros((bp, SUBLANES, CONV_CH), F32),
                      jnp.zeros((bp, NH_S, P_S, D_STATE), F32))
        yp, outs = _mixer(yp, ada_p, lw, zero_state,
                          lambda u, g, gt: _fox_prompt(u, g, gt, bp, sp), chunk_p, BF16, True)
        res_p.append(outs)

        state = (jnp.concatenate([jnp.swapaxes(state_mlstm_C[l], -1, -2),
                                  jnp.broadcast_to(state_mlstm_n[l][..., None], (bs, NH_M, DQK_M, DV_M))], axis=-1),
                 jnp.broadcast_to(state_mlstm_m[l][:, :, None, None], (bs, NH_M, 1, LANES)),
                 jnp.pad(state_conv[l], ((0, 0), (SUBLANES - (CONV_W - 1), 0), (0, 0))),
                 state_ssm[l])
        ys, outs = _mixer(ys, ada_s, lw, state,
                          lambda u, g, gt, l=l: _fox_sample(u, gt, page_table, ck_t, cv_t, clf_t, l * n_phys, bs, ss),
                          chunk_s, F32, False)
        res_s.append(outs)

    stack = lambda res, i: jnp.stack([r[i] for r in res])
    return (yp, ys) + tuple(stack(res_p, i) for i in range(8)) + tuple(stack(res_s, i) for i in range(8))
```

```python
import functools

import jax
import jax.numpy as jnp
from jax import lax
from jax.experimental import pallas as pl
from jax.experimental.pallas import tpu as pltpu

F32 = jnp.float32
BF16 = jnp.bfloat16

NH_M, DQK_M, DV_M = 4, 64, 128
W_M = NH_M * DV_M
GATE_CAP = 15.0
NH_S, P_S, N_GROUPS, D_STATE, CONV_W = 8, 64, 2, 128, 4
D_INNER = NH_S * P_S
CONV_CH = D_INNER + 2 * N_GROUPS * D_STATE
NH_F, DH_F = 8, 64
W_F = NH_F * DH_F
CHUNK = 64
SCAN_CHUNK = 256
EPS = 1e-6
LANES = 128
SUBLANES = 8

LN_LOGI, LN_LOGF, LN_DT, LN_FF, LN_BM, LN_CUM, LN_FT = 0, 4, 8, 16, 24, 32, 40

VMEM_LIMIT = 56 * 1024 * 1024


def _cparams(sem):
    return pltpu.CompilerParams(dimension_semantics=sem, vmem_limit_bytes=VMEM_LIMIT)


def _sigmoid(x):
    return 1.0 / (1.0 + jnp.exp(-x))


def _softplus(x):
    return jnp.maximum(x, 0.0) + jnp.log(1.0 + jnp.exp(-jnp.abs(x)))


def _rms(x, g):
    return x * lax.rsqrt(jnp.mean(x * x, axis=-1, keepdims=True) + EPS) * g


def _dot(a, b):
    return jnp.dot(a, b, preferred_element_type=F32)


def _dot_nt(a, b):
    return lax.dot_general(a, b, (((1,), (1,)), ((), ())), preferred_element_type=F32)


def _dot_tn(a, b):
    return lax.dot_general(a, b, (((0,), (0,)), ((), ())), preferred_element_type=F32)


def _const_spec(shape):
    nd = len(shape)
    return pl.BlockSpec(shape, lambda *_: (0,) * nd)


def _ada_kernel(c_ref, w_ref, b_ref, o_ref):
    c = c_ref[...]
    s = (c * _sigmoid(c)).astype(BF16)
    o_ref[0] = _dot(s, w_ref[0].astype(BF16)) + b_ref[0]


def _ada(c_all, w_ada, b_ada):
    depth, d, n = w_ada.shape
    rows = c_all.shape[0]
    tn = 512
    return pl.pallas_call(
        _ada_kernel,
        grid=(depth, n // tn),
        in_specs=[pl.BlockSpec((rows, d), lambda l, j: (0, 0)),
                  pl.BlockSpec((1, d, tn), lambda l, j: (l, 0, j)),
                  pl.BlockSpec((1, 1, tn), lambda l, j: (l, 0, j))],
        out_specs=pl.BlockSpec((1, rows, tn), lambda l, j: (l, 0, j)),
        out_shape=jax.ShapeDtypeStruct((depth, rows, n), F32),
        compiler_params=_cparams(("arbitrary", "arbitrary")),
        name="ada",
    )(c_all, w_ada, b_ada.reshape(depth, 1, n))


SEGS = (("mq", NH_M * DQK_M, DQK_M ** -0.5, F32), ("mk", NH_M * DQK_M, 1.0, F32), ("mv", W_M, 1.0, F32),
        ("mo", W_M, 1.0, F32), ("sz", D_INNER, 1.0, F32), ("sxbc", CONV_CH, 1.0, F32),
        ("fq", W_F, DH_F ** -0.5, BF16), ("small", LANES, 1.0, F32))
W_PROJ = sum(sg[1] for sg in SEGS)


def _inproj_kernel(x_ref, sh_ref, sc_ref, g_ref, w_ref, wkv_ref, *rest, kv_t, n_alias):
    outs = rest[n_alias:]
    x = x_ref[...]
    tb, ts, d = x.shape
    h = _rms(x, g_ref[...]) * (1.0 + sc_ref[...]) + sh_ref[...]
    hb = h.reshape(tb * ts, d).astype(BF16)
    off = 0
    for (_, wd, scale, _), o in zip(SEGS, outs):
        r = _dot(hb, w_ref[:, off:off + wd])
        if scale != 1.0:
            r = r * scale
        o[...] = r.astype(o.dtype)
        off += wd
    small_ref, smallt_ref, fk_ref, fv_ref = outs[len(SEGS) - 1:]
    smallt_ref[...] = small_ref[...].T
    if kv_t:
        kv = _dot_nt(wkv_ref[...], hb)
        fk_ref[0, 0] = kv[:W_F]
        fv_ref[0, 0] = kv[W_F:]
    else:
        kv = _dot(hb, wkv_ref[...])
        fk_ref[...] = kv[:, :W_F]
        fv_ref[...] = kv[:, W_F:]


def _x_tiles(b, s, ts_max):
    if s >= ts_max:
        return 1, ts_max
    tb = max(1, min(b, ts_max // s))
    return tb, s


def _ada_spec(tb, d, col):
    return pl.BlockSpec((tb, 1, d), lambda i, j: (i, 0, col))


def _inproj(x, ada, g_pre, w, wkv, kv_t, layer=0, depth=1, kv_all=None):
    b, s, d = x.shape
    tb, ts = _x_tiles(b, s, 256)
    tm = tb * ts
    nj = s // ts
    m = b * s
    row = lambda i, j: (i * nj + j, 0)
    out_shape = [jax.ShapeDtypeStruct((m, wd), dt) for _, wd, _, dt in SEGS]
    out_specs = [pl.BlockSpec((tm, wd), row) for _, wd, _, _ in SEGS]
    out_shape.append(jax.ShapeDtypeStruct((LANES, m), F32))
    out_specs.append(pl.BlockSpec((LANES, tm), lambda i, j: (0, i * nj + j)))
    in_specs = [pl.BlockSpec((tb, ts, d), lambda i, j: (i, j, 0)),
                _ada_spec(tb, d, 0), _ada_spec(tb, d, 1),
                _const_spec((1, d)),
                pl.BlockSpec((d, W_PROJ), lambda i, j: (0, 0), pipeline_mode=pl.Buffered(1)),
                pl.BlockSpec(wkv.shape, lambda i, j: (0, 0), pipeline_mode=pl.Buffered(1))]
    args = [x, ada, ada, g_pre.reshape(1, d), w, wkv]
    aliases = {}
    if kv_t:
        assert tb == 1
        out_shape += [jax.ShapeDtypeStruct((depth, b, W_F, s), F32)] * 2
        out_specs += [pl.BlockSpec((1, 1, W_F, ts), lambda i, j: (layer, i, 0, j))] * 2
        if kv_all is not None:
            aliases = {len(args): len(SEGS) + 1, len(args) + 1: len(SEGS) + 2}
            in_specs += [pl.BlockSpec(memory_space=pl.ANY)] * 2
            args += list(kv_all)
    else:
        out_shape += [jax.ShapeDtypeStruct((m, W_F), F32)] * 2
        out_specs += [pl.BlockSpec((tm, W_F), row)] * 2
    outs = pl.pallas_call(
        functools.partial(_inproj_kernel, kv_t=kv_t, n_alias=len(aliases)),
        grid=(b // tb, nj),
        in_specs=in_specs,
        out_specs=out_specs,
        out_shape=out_shape,
        input_output_aliases=aliases,
        compiler_params=_cparams(("arbitrary", "arbitrary")),
        name="inproj",
    )(*args)
    names = [sg[0] for sg in SEGS] + ["small_t", "fk", "fv"]
    return dict(zip(names, outs))


def _prep_kernel(small_ref, bias_ref, alog_ref, g_ref, gt_ref, carry_ref, *, chunk, seg, tiles_per_seq):
    i = pl.program_id(0)

    @pl.when(i % tiles_per_seq == 0)
    def _():
        carry_ref[...] = jnp.zeros_like(carry_ref)

    v = small_ref[...] + bias_ref[...]
    ts = v.shape[0]
    lane = lax.broadcasted_iota(jnp.int32, v.shape, 1)
    row = lax.broadcasted_iota(jnp.int32, v.shape, 0)
    capped = GATE_CAP * jnp.tanh(v * (1.0 / GATE_CAP))
    is_m = (lane < LN_DT) | ((lane >= LN_BM) & (lane < LN_CUM))
    vv = jnp.where(is_m, capped, v)
    sp = _softplus(vv)
    lsig = -_softplus(-vv)
    a = -jnp.exp(alog_ref[...])
    val = jnp.where(lane < LN_LOGF, vv,
          jnp.where(lane < LN_DT, lsig,
          jnp.where(lane < LN_FF, sp,
          jnp.where(lane < LN_CUM, lsig,
          jnp.where(lane < LN_FT, sp * a, lsig)))))
    val = jnp.where(lane < LN_FT + NH_F, val, 0.0)
    ridx = jnp.where(lane < LN_FT, row % chunk, row % seg)
    ridx = jnp.where(lane >= LN_BM, ridx, -1)
    x = val
    k = 1
    while k < min(ts, max(chunk, seg)):
        x = x + jnp.where(ridx >= k, pltpu.roll(x, k, 0), 0.0)
        k *= 2
    x = x + jnp.where(lane >= LN_FT, carry_ref[...], 0.0)
    carry_ref[...] = x[ts - 1:ts, :]
    g_ref[...] = x
    gt_ref[...] = x.T


def _prep(small, bias_lanes, alog_lanes, s, chunk):
    m = small.shape[0]
    ts = min(512, m)
    if s >= ts:
        seg, tiles_per_seq = ts, s // ts
    else:
        seg, tiles_per_seq = s, 1
    return pl.pallas_call(
        functools.partial(_prep_kernel, chunk=chunk, seg=seg, tiles_per_seq=tiles_per_seq),
        grid=(m // ts,),
        in_specs=[pl.BlockSpec((ts, LANES), lambda i: (i, 0)), _const_spec((1, LANES)), _const_spec((1, LANES))],
        out_specs=[pl.BlockSpec((ts, LANES), lambda i: (i, 0)), pl.BlockSpec((LANES, ts), lambda i: (0, i))],
        out_shape=[jax.ShapeDtypeStruct((m, LANES), F32), jax.ShapeDtypeStruct((LANES, m), F32)],
        scratch_shapes=[pltpu.VMEM((1, LANES), F32)],
        compiler_params=_cparams(("arbitrary",)),
        name="prep",
    )(small, bias_lanes, alog_lanes)


def _mlstm_kernel(q_ref, k_ref, v_ref, mo_ref, g_ref, gt_ref, gh_ref, c0_ref, m0_ref,
                  y_ref, c_ref, m_ref, *, chunk, gt3d, mm):
    j = pl.program_id(1)

    @pl.when(j == 0)
    def _():
        c_ref[...] = c0_ref[...]
        m_ref[...] = m0_ref[...]

    ts = q_ref.shape[0]
    ln = chunk
    gt = gt_ref[0] if gt3d else gt_ref[...]
    rr = lax.broadcasted_iota(jnp.int32, (ln, ln), 0)
    cc = lax.broadcasted_iota(jnp.int32, (ln, ln), 1)
    causal = cc <= rr
    ones = jnp.ones((ln, DV_M), F32)
    heads = range(NH_M)
    for c in range(ts // ln):
        lo, hi = c * ln, (c + 1) * ln
        g = g_ref[lo:hi, :]
        qb = [q_ref[lo:hi, h * DQK_M:(h + 1) * DQK_M].astype(mm) for h in heads]
        kb = [k_ref[lo:hi, h * DQK_M:(h + 1) * DQK_M].astype(mm) for h in heads]
        vf = [v_ref[lo:hi, h * DV_M:(h + 1) * DV_M] for h in heads]
        b_c = [g[:, LN_BM + h:LN_BM + h + 1] for h in heads]
        m_st = [m_ref[0, h][:, 0:1] for h in heads]
        st = [c_ref[0, h] for h in heads]
        qk = [_dot_nt(qb[h], kb[h]) for h in heads]
        qc = [_dot(qb[h], st[h].astype(mm)) for h in heads]
        dm = [jnp.where(causal, b_c[h] - gt[LN_BM + h:LN_BM + h + 1, lo:hi]
                        + gt[LN_LOGI + h:LN_LOGI + h + 1, lo:hi], -jnp.inf) for h in heads]
        inter = [b_c[h] + m_st[h] for h in heads]
        m_t = [jnp.maximum(inter[h], jnp.max(dm[h], axis=1, keepdims=True)) for h in heads]
        s = [(qk[h] * jnp.exp(dm[h] - m_t[h])).astype(mm) for h in heads]
        sv = [_dot(s[h], jnp.concatenate([vf[h], ones], axis=1).astype(mm)) for h in heads]
        m_new = [m_t[h][ln - 1:ln, :] for h in heads]
        wk = [jnp.exp(b_c[h][ln - 1:ln, :] - b_c[h] + g[:, LN_LOGI + h:LN_LOGI + h + 1] - m_new[h]) for h in heads]
        wv = [jnp.concatenate([wk[h] * vf[h], jnp.broadcast_to(wk[h], (ln, DV_M))], axis=1).astype(mm) for h in heads]
        upd = [_dot_tn(kb[h], wv[h]) for h in heads]
        for h in heads:
            decay = jnp.exp(b_c[h][ln - 1:ln, :] + m_st[h] - m_new[h])
            c_ref[0, h] = decay * st[h] + upd[h]
            m_ref[0, h] = jnp.broadcast_to(m_new[h], (1, LANES))
        for h in heads:
            both = sv[h] + jnp.exp(inter[h] - m_t[h]) * qc[h]
            num, den = both[:, :DV_M], both[:, DV_M:]
            hv = num / jnp.maximum(jnp.abs(den), jnp.exp(-m_t[h]))
            hm = _rms(hv, gh_ref[:, h * DV_M:(h + 1) * DV_M])
            y = hm * _sigmoid(mo_ref[lo:hi, h * DV_M:(h + 1) * DV_M])
            y_ref[lo:hi, h * DV_M:(h + 1) * DV_M] = y.astype(y_ref.dtype)


def _gt_spec(b, s, ts, nj, gt3d):
    if gt3d:
        return pl.BlockSpec((1, LANES, s), lambda i, j: (i, 0, 0))
    return pl.BlockSpec((LANES, ts), lambda i, j: (0, i * nj + j))


def _mlstm(u, g, gt, g_mhead, c0n, m0, b, s, chunk, mm):
    ts = min(256, s)
    nj = s // ts
    m = b * s
    gt3d = gt.ndim == 3
    row = lambda i, j: (i * nj + j, 0)
    st = lambda *shape: pl.BlockSpec((1,) + shape, lambda i, j: (i,) + (0,) * len(shape))
    return pl.pallas_call(
        functools.partial(_mlstm_kernel, chunk=chunk, gt3d=gt3d, mm=mm),
        grid=(b, nj),
        in_specs=[pl.BlockSpec((ts, NH_M * DQK_M), row), pl.BlockSpec((ts, NH_M * DQK_M), row),
                  pl.BlockSpec((ts, W_M), row), pl.BlockSpec((ts, W_M), row),
                  pl.BlockSpec((ts, LANES), row), _gt_spec(b, s, ts, nj, gt3d),
                  _const_spec((1, W_M)),
                  st(NH_M, DQK_M, 2 * DV_M), st(NH_M, 1, LANES)],
        out_specs=[pl.BlockSpec((ts, W_M), row),
                   st(NH_M, DQK_M, 2 * DV_M), st(NH_M, 1, LANES)],
        out_shape=[jax.ShapeDtypeStruct((m, W_M), BF16),
                   jax.ShapeDtypeStruct((b, NH_M, DQK_M, 2 * DV_M), F32),
                   jax.ShapeDtypeStruct((b, NH_M, 1, LANES), F32)],
        compiler_params=_cparams(("arbitrary", "arbitrary")),
        name="mlstm",
    )(u["mq"], u["mk"], u["mv"], u["mo"], g, gt, g_mhead.reshape(1, W_M), c0n, m0)


def _ssd_expansion(ln):
    stride = max(ln, LANES)
    r = lax.broadcasted_iota(jnp.int32, (LANES, 2 * D_INNER + NH_S * stride), 0)
    c = lax.broadcasted_iota(jnp.int32, (LANES, 2 * D_INNER + NH_S * stride), 1)
    src = jnp.where(c < D_INNER, LN_DT + c // P_S,
                    jnp.where(c < 2 * D_INNER, LN_CUM + (c - D_INNER) // P_S, LN_CUM + (c - 2 * D_INNER) // stride))
    return jnp.where(r == src, 1.0, 0.0).astype(BF16)


def _ssd_kernel(xbc_ref, sz_ref, g_ref, gt_ref, ex_ref, cw_ref, cb_ref, dsk_ref, gs_ref, conv0_ref, h0_ref,
                y_ref, hst_ref, tail_ref, *, chunk, gt3d, mm):
    j = pl.program_id(1)

    @pl.when(j == 0)
    def _():
        hst_ref[...] = h0_ref[...]
        tail_ref[...] = conv0_ref[0]

    x = xbc_ref[...]
    ts = x.shape[0]
    ln = chunk
    prev = tail_ref[...]
    row8 = lax.broadcasted_iota(jnp.int32, prev.shape, 0)
    acc = cb_ref[...] + cw_ref[CONV_W - 1:CONV_W, :] * x
    for k in range(1, CONV_W):
        rolled = pltpu.roll(x, k, 0)
        first = jnp.where(row8 < k, pltpu.roll(prev, k, 0), rolled[:SUBLANES])
        shifted = first if ts == SUBLANES else jnp.concatenate([first, rolled[SUBLANES:]], axis=0)
        acc = acc + cw_ref[CONV_W - 1 - k:CONV_W - k, :] * shifted
    tail_ref[...] = x[ts - SUBLANES:, :]
    xa = acc * _sigmoid(acc)

    gt = gt_ref[0] if gt3d else gt_ref[...]
    rr = lax.broadcasted_iota(jnp.int32, (ln, ln), 0)
    cc = lax.broadcasted_iota(jnp.int32, (ln, ln), 1)
    causal = cc <= rr
    hpg = NH_S // N_GROUPS
    for c in range(ts // ln):
        lo, hi = c * ln, (c + 1) * ln
        g = g_ref[lo:hi, :]
        heads, pairs = range(NH_S), range(NH_S // 2)
        ex_dt = BF16 if ln % (2 * SUBLANES) == 0 else F32
        full = sum(_dot(part.astype(ex_dt), ex_ref[...].astype(ex_dt)) for part in _split3(g))
        dt_full, cum_full = full[:, :D_INNER], full[:, D_INNER:2 * D_INNER]
        stride = max(ln, LANES)
        cum_b = [full[:, 2 * D_INNER + h * stride:2 * D_INNER + h * stride + ln] for h in heads]
        xs = xa[lo:hi, :D_INNER]
        xdt = xs * dt_full
        wkx = jnp.exp(cum_full[ln - 1:ln, :] - cum_full) * xdt
        bmg = [xa[lo:hi, D_INNER + grp * D_STATE:D_INNER + (grp + 1) * D_STATE].astype(mm) for grp in range(N_GROUPS)]
        co = D_INNER + N_GROUPS * D_STATE
        cmg = [xa[lo:hi, co + grp * D_STATE:co + (grp + 1) * D_STATE].astype(mm) for grp in range(N_GROUPS)]
        cbm = [_dot_nt(cmg[grp], bmg[grp]) for grp in range(N_GROUPS)]
        psl = [slice(hp * 2 * P_S, (hp + 1) * 2 * P_S) for hp in pairs]
        pgrp = [2 * hp // hpg for hp in pairs]
        hst = [jnp.concatenate([hst_ref[0, 2 * hp], hst_ref[0, 2 * hp + 1]], axis=0) for hp in pairs]
        ych = [_dot_nt(cmg[pgrp[hp]], hst[hp].astype(mm)) for hp in pairs]
        ldec = [jnp.exp(jnp.where(causal, cum_b[h] - gt[LN_CUM + h:LN_CUM + h + 1, lo:hi], -jnp.inf)) for h in heads]
        xdt_p = [xdt[:, psl[hp]].astype(mm) for hp in pairs]
        ycb = [_dot((cbm[h // hpg] * ldec[h]).astype(mm), xdt_p[h // 2]) for h in heads]
        upd = [_dot_tn(wkx[:, psl[hp]].astype(mm), bmg[pgrp[hp]]) for hp in pairs]
        low_l = lax.broadcasted_iota(jnp.int32, (ln, 2 * P_S), 1) < P_S
        low_r = lax.broadcasted_iota(jnp.int32, (2 * P_S, D_STATE), 0) < P_S
        ys = []
        for hp in pairs:
            d0 = jnp.exp(g[ln - 1:ln, LN_CUM + 2 * hp:LN_CUM + 2 * hp + 1])
            d1 = jnp.exp(g[ln - 1:ln, LN_CUM + 2 * hp + 1:LN_CUM + 2 * hp + 2])
            new = jnp.where(low_r, d0, d1) * hst[hp] + upd[hp]
            hst_ref[0, 2 * hp] = new[:P_S]
            hst_ref[0, 2 * hp + 1] = new[P_S:]
            ys.append(jnp.where(low_l, ycb[2 * hp], ycb[2 * hp + 1]) + jnp.exp(cum_full[:, psl[hp]]) * ych[hp]
                      + dsk_ref[:, psl[hp]] * xs[:, psl[hp]])
        yy = jnp.concatenate(ys, axis=1)
        z = sz_ref[lo:hi, :]
        y_ref[lo:hi, :] = _rms(yy * (z * _sigmoid(z)), gs_ref[...]).astype(y_ref.dtype)


def _ssd(u, g, gt, conv_w, conv_b, d_skip, g_ssm, conv0, h0, b, s, chunk, mm):
    ts = min(256, s)
    nj = s // ts
    m = b * s
    gt3d = gt.ndim == 3
    ex = _ssd_expansion(chunk)
    row = lambda i, j: (i * nj + j, 0)
    st = lambda *shape: pl.BlockSpec((1,) + shape, lambda i, j: (i,) + (0,) * len(shape))
    return pl.pallas_call(
        functools.partial(_ssd_kernel, chunk=chunk, gt3d=gt3d, mm=mm),
        grid=(b, nj),
        in_specs=[pl.BlockSpec((ts, CONV_CH), row), pl.BlockSpec((ts, D_INNER), row),
                  pl.BlockSpec((ts, LANES), row), _gt_spec(b, s, ts, nj, gt3d),
                  _const_spec(ex.shape),
                  _const_spec((CONV_W, CONV_CH)), _const_spec((1, CONV_CH)),
                  _const_spec((1, D_INNER)), _const_spec((1, D_INNER)),
                  st(SUBLANES, CONV_CH), st(NH_S, P_S, D_STATE)],
        out_specs=[pl.BlockSpec((ts, D_INNER), row), st(NH_S, P_S, D_STATE)],
        out_shape=[jax.ShapeDtypeStruct((m, D_INNER), BF16),
                   jax.ShapeDtypeStruct((b, NH_S, P_S, D_STATE), F32)],
        scratch_shapes=[pltpu.VMEM((SUBLANES, CONV_CH), F32)],
        compiler_params=_cparams(("arbitrary", "arbitrary")),
        name="ssd",
    )(u["sxbc"], u["sz"], g, gt, ex, conv_w, conv_b.reshape(1, CONV_CH),
      jnp.repeat(d_skip, P_S).reshape(1, D_INNER), g_ssm.reshape(1, D_INNER), conv0, h0)


def _split3(x):
    hi = x.astype(BF16).astype(F32)
    r = x - hi
    mid = r.astype(BF16).astype(F32)
    return hi, mid, r - mid


FOX_EXT = 16
FOX_VROWS = 2 * DH_F + FOX_EXT
FOX_SUB = 256
FOX_WIDE = 4
FOXS_PAGES = 16
FOX_SKIP = 112.0


def _foxp_kernel(q_ref, kt_ref, vt_ref, g_ref, ftr_ref, o_ref, kaug_ref, vaug_ref, stat_ref, *, tq):
    hp = pl.program_id(1)
    i = pl.program_id(2)
    hw = 2 * DH_F
    s_len = kaug_ref.shape[1]

    wide = FOX_WIDE * tq
    lane_row = lax.broadcasted_iota(jnp.int32, (1, LANES), 1)

    @pl.when(i == 0)
    def _():
        kb = kt_ref[0, 0].astype(BF16)
        kaug_ref[0:hw, :] = kb
        vaug_ref[0:hw, :] = vt_ref[0, 0].astype(BF16)
        ksq = kb.astype(F32) * kb.astype(F32)
        r = lax.broadcasted_iota(jnp.int32, (FOX_EXT, s_len), 0)
        ext = jnp.where(r < 3, 1.0, 0.0)
        for hh in range(2):
            ft_row = ftr_ref[pl.ds(hp * 2 + hh, 1), :]
            parts = _split3(ft_row)
            for pi, part in enumerate(parts):
                ext = jnp.where(r == 3 + 3 * hh + pi, -part, ext)
            ftq = jnp.full((1, LANES), -jnp.inf, F32)
            for jb in range(s_len // tq):
                ftq = jnp.where(lane_row == jb, ft_row[:, tq * (jb + 1) - 1:tq * (jb + 1)], ftq)
            stat_ref[hh:hh + 1, :] = ftq
            kn2 = jnp.sum(ksq[hh * DH_F:(hh + 1) * DH_F, :], axis=0, keepdims=True)
            stat_ref[2 + hh:3 + hh, :] = jnp.broadcast_to(jnp.sqrt(jnp.max(kn2, axis=1, keepdims=True)), (1, LANES))
        kaug_ref[hw:hw + FOX_EXT, :] = ext.astype(BF16)
        kaug_ref[hw + FOX_EXT:, :] = jnp.zeros((kaug_ref.shape[0] - hw - FOX_EXT, s_len), BF16)
        vaug_ref[hw:, :] = jnp.where(r == 0, 1.0, 0.0).astype(BF16)

    g = g_ref[...]
    q = q_ref[...]
    lane = lax.broadcasted_iota(jnp.int32, (tq, LANES), 1)
    low = lane < DH_F
    qaug = []
    qsq = q.astype(F32) * q.astype(F32)
    skippable = None
    for hh in range(2):
        mine = low if hh == 0 else ~low
        ft_c = jnp.sum(jnp.where(lane == LN_FT + hp * 2 + hh, g, 0.0), axis=1, keepdims=True)
        hi, mid, lo = _split3(ft_c)
        ext = jnp.where(lane == 0, hi, jnp.where(lane == 1, mid, jnp.where(lane == 2, lo, 0.0)))
        ext = jnp.where((lane >= 3 + 3 * hh) & (lane < 6 + 3 * hh), 1.0, ext)
        qm = jnp.where(mine, q, jnp.zeros_like(q))
        qaug.append(jnp.concatenate([qm, ext.astype(BF16)], axis=1))
        qmax = jnp.sqrt(jnp.max(jnp.sum(jnp.where(mine, qsq, 0.0), axis=1, keepdims=True), axis=0, keepdims=True))
        thr = ft_c[0:1, :] + 2.0 * qmax * stat_ref[2 + hh:3 + hh, 0:1] + FOX_SKIP
        cond = stat_ref[hh:hh + 1, :] > thr
        skippable = cond if skippable is None else skippable & cond
    n_skip = jnp.sum(jnp.where(skippable, 1, 0).astype(jnp.int32))

    rr = lax.broadcasted_iota(jnp.int32, (tq, tq), 0)
    cc = lax.broadcasted_iota(jnp.int32, (tq, tq), 1)
    causal = cc <= rr

    def block(carry, start, width, diag):
        subs = [pl.ds(pl.multiple_of(start + c * FOX_SUB, FOX_SUB), FOX_SUB) for c in range(width // FOX_SUB)]
        ms, ls, acc = carry
        new_m, new_l, alphas, pvs = [], [], [], []
        scores = []
        for hh in range(2):
            ss = []
            for c, sub in enumerate(subs):
                s = _dot(qaug[hh], kaug_ref[:, sub])
                dc = c * FOX_SUB - (width - tq)
                if diag and dc >= 0:
                    s = jnp.where(causal[:, dc:dc + FOX_SUB], s, -jnp.inf)
                ss.append(s)
            scores.append(ss)
        for hh in range(2):
            m_blk = jnp.max(functools.reduce(jnp.maximum, scores[hh]), axis=1, keepdims=True)
            new_m.append(jnp.maximum(ms[hh], m_blk))
            alphas.append(jnp.exp(ms[hh] - new_m[hh]))
        pv = [None, None]
        for c in range(len(subs)):
            for hh in range(2):
                p = jnp.exp(scores[hh][c] - new_m[hh]).astype(BF16)
                d = _dot_nt(p, vaug_ref[:, subs[c]])
                pv[hh] = d if pv[hh] is None else pv[hh] + d
        for hh in range(2):
            new_l.append(alphas[hh] * ls[hh] + pv[hh][:, hw:hw + 1])
            pvs.append(pv[hh][:, :hw])
        acc = jnp.where(low, alphas[0], alphas[1]) * acc + jnp.where(low, pvs[0], pvs[1])
        return tuple(new_m), tuple(new_l), acc

    neg = jnp.full((tq, 1), -jnp.inf, F32)
    zero = jnp.zeros((tq, 1), F32)
    init = ((neg, neg), (zero, zero), jnp.zeros((tq, hw), F32))
    live = i - n_skip
    base = n_skip * tq
    carry = lax.fori_loop(0, live // FOX_WIDE, lambda jb, c: block(c, base + jb * wide, wide, False), init)
    tail_start = base + (live // FOX_WIDE) * wide

    def tail(r):
        return lambda c: block(c, tail_start, (r + 1) * tq, True)

    def pick(lo, hi):
        if hi - lo == 1:
            return tail(lo)
        mid = (lo + hi) // 2
        return lambda c: lax.cond(live % FOX_WIDE < mid, pick(lo, mid), pick(mid, hi), c)

    _, ls, acc = pick(0, FOX_WIDE)(carry)
    o_ref[...] = (acc / jnp.where(low, ls[0], ls[1])).astype(o_ref.dtype)


def _fox_prompt(u, g, gt, b, s, layer):
    tq = min(512, s)
    nq = s // tq
    m = b * s
    hw = 2 * DH_F
    return pl.pallas_call(
        functools.partial(_foxp_kernel, tq=tq),
        grid=(b, NH_F // 2, nq),
        in_specs=[pl.BlockSpec((tq, hw), lambda bb, hp, i: (bb * nq + i, hp)),
                  pl.BlockSpec((1, 1, hw, s), lambda bb, hp, i: (layer, bb, hp, 0)),
                  pl.BlockSpec((1, 1, hw, s), lambda bb, hp, i: (layer, bb, hp, 0)),
                  pl.BlockSpec((tq, LANES), lambda bb, hp, i: (bb * nq + i, 0)),
                  pl.BlockSpec((SUBLANES, s), lambda bb, hp, i: (LN_FT // SUBLANES, bb))],
        out_specs=pl.BlockSpec((tq, hw), lambda bb, hp, i: (bb * nq + i, hp)),
        out_shape=jax.ShapeDtypeStruct((m, W_F), BF16),
        scratch_shapes=[pltpu.VMEM((2 * hw, s), BF16), pltpu.VMEM((FOX_VROWS, s), BF16),
                        pltpu.VMEM((SUBLANES, LANES), F32)],
        compiler_params=_cparams(("arbitrary", "arbitrary", "arbitrary")),
        name="fox_prompt",
    )(u["fq"], u["fk"], u["fv"], g, gt)


def _foxs_kernel(pt_ref, q_ref, kn_ref, vn_ref, gt_ref, *rest, pages):
    del pt_ref
    k_refs, v_refs, lf_refs = rest[:pages], rest[pages:2 * pages], rest[2 * pages:3 * pages]
    o_ref, qbd_ref, m_ref, l_ref, acc_ref, carry_ref, kcat_ref, vcat_ref = rest[3 * pages:]
    gi = pl.program_id(1)
    t = q_ref.shape[0]
    ht = NH_F * t
    page = k_refs[0].shape[2]

    def expand_heads(x):
        return jnp.broadcast_to(x[:, None, :], (NH_F, t, x.shape[1])).reshape(ht, x.shape[1])

    @pl.when(gi == 0)
    def _():
        q = q_ref[...].astype(F32)
        qt = jnp.broadcast_to(q[None], (NH_F, t, W_F)).reshape(ht, W_F)
        rh = lax.broadcasted_iota(jnp.int32, (ht, W_F), 0) // t
        lh = lax.broadcasted_iota(jnp.int32, (ht, W_F), 1) // DH_F
        qbd_ref[...] = jnp.where(rh == lh, qt, 0.0)
        m_ref[...] = jnp.full_like(m_ref, -jnp.inf)
        l_ref[...] = jnp.zeros_like(l_ref)
        acc_ref[...] = jnp.zeros_like(acc_ref)
        carry_ref[...] = jnp.zeros_like(carry_ref)

    cn_rows = expand_heads(gt_ref[0][LN_FT:LN_FT + NH_F, :])
    tq_idx = lax.broadcasted_iota(jnp.int32, (ht, t), 0) % t
    tk_idx = lax.broadcasted_iota(jnp.int32, (ht, t), 1)
    cn_col = jnp.sum(jnp.where(tq_idx == tk_idx, cn_rows, 0.0), axis=1, keepdims=True)

    qbd = qbd_ref[...]
    qbd_b = qbd.astype(BF16)

    def update(s, pv):
        m = m_ref[...]
        m_new = jnp.maximum(m, jnp.max(s, axis=1, keepdims=True))
        p = jnp.exp(s - m_new)
        alpha = jnp.exp(m - m_new)
        l_ref[...] = alpha * l_ref[...] + jnp.sum(p, axis=1, keepdims=True)
        acc_ref[...] = alpha * acc_ref[...] + pv(p)
        m_ref[...] = m_new

    lf_all = jnp.concatenate([lf_refs[pi][0] for pi in range(pages)], axis=0)
    tri = jnp.where(lax.broadcasted_iota(jnp.int32, (page, page), 0)
                    <= lax.broadcasted_iota(jnp.int32, (page, page), 1), 1.0, 0.0).astype(BF16)
    pre_all = sum(_dot(part.astype(BF16), tri) for part in _split3(lf_all))
    carry = carry_ref[...]
    rs = [None] * pages
    for pi in reversed(range(pages)):
        pre = pre_all[pi * NH_F:(pi + 1) * NH_F, :]
        tot = pre[:, page - 1:page]
        rs[pi] = expand_heads(carry + tot - pre)
        carry = carry + tot
        kcat_ref[:, pi * page:(pi + 1) * page] = k_refs[pi][0].astype(BF16)
        vcat_ref[:, pi * page:(pi + 1) * page] = v_refs[pi][0].astype(BF16)
    carry_ref[...] = carry
    s_all = _dot(qbd_b, kcat_ref[...]) + jnp.concatenate(rs, axis=1) + cn_col
    update(s_all, lambda p: _dot_nt(p.astype(BF16), vcat_ref[...]))

    @pl.when(gi == pl.num_programs(1) - 1)
    def _():
        s_new = _dot_nt(qbd, kn_ref[...]) + (cn_col - cn_rows)
        s_new = jnp.where(tk_idx <= tq_idx, s_new, -jnp.inf)
        update(s_new, lambda p: _dot(p, vn_ref[...]))
        o = acc_ref[...] / l_ref[...]
        lh = lax.broadcasted_iota(jnp.int32, (t, W_F), 1) // DH_F
        y = jnp.zeros((t, W_F), F32)
        for h in range(NH_F):
            y = y + jnp.where(lh == h, o[h * t:(h + 1) * t, :], 0.0)
        o_ref[...] = y.astype(o_ref.dtype)


def _fox_sample(u, gt3, page_table, ck_t, cv_t, clf_t, page_base, b, t):
    n_pages = page_table.shape[1]
    page = ck_t.shape[2]
    pages = FOXS_PAGES
    while n_pages % pages:
        pages //= 2
    ng = n_pages // pages
    ht = NH_F * t

    def page_spec(width_shape, pi):
        return pl.BlockSpec((1,) + width_shape,
                            lambda bb, gi, pt: (page_base + pt[bb, (ng - 1 - gi) * pages + pi], 0, 0))

    tok = pl.BlockSpec((t, W_F), lambda bb, gi, pt: (bb, 0))
    in_specs = [tok, tok, tok, pl.BlockSpec((1, LANES, t), lambda bb, gi, pt: (bb, 0, 0))]
    in_specs += [page_spec((W_F, page), pi) for pi in range(pages)]
    in_specs += [page_spec((W_F, page), pi) for pi in range(pages)]
    in_specs += [page_spec((NH_F, page), pi) for pi in range(pages)]
    grid_spec = pltpu.PrefetchScalarGridSpec(
        num_scalar_prefetch=1,
        grid=(b, ng),
        in_specs=in_specs,
        out_specs=pl.BlockSpec((t, W_F), lambda bb, gi, pt: (bb, 0)),
        scratch_shapes=[pltpu.VMEM((ht, W_F), F32), pltpu.VMEM((ht, 1), F32), pltpu.VMEM((ht, 1), F32),
                        pltpu.VMEM((ht, W_F), F32), pltpu.VMEM((NH_F, 1), F32),
                        pltpu.VMEM((W_F, pages * page), BF16), pltpu.VMEM((W_F, pages * page), BF16)],
    )
    return pl.pallas_call(
        functools.partial(_foxs_kernel, pages=pages),
        grid_spec=grid_spec,
        out_shape=jax.ShapeDtypeStruct((b * t, W_F), BF16),
        compiler_params=_cparams(("arbitrary", "arbitrary")),
        name="fox_sample",
    )(page_table, u["fq"], u["fk"], u["fv"], gt3, *([ck_t] * pages), *([cv_t] * pages), *([clf_t] * pages))


def _merge_kernel(x_ref, sh_ref, sc_ref, gt_ref, gpre_ref, gpost_ref, ym_ref, ys_ref, yf_ref,
                  wg_ref, wm_ref, ws_ref, wf_ref, wo_ref, o_ref):
    x = x_ref[...]
    tb, ts, d = x.shape
    h = _rms(x, gpre_ref[...]) * (1.0 + sc_ref[...]) + sh_ref[...]
    hb = h.reshape(tb * ts, d).astype(BF16)
    merged = None
    for bi, (y_ref, w_ref) in enumerate(((ym_ref, wm_ref), (ys_ref, ws_ref), (yf_ref, wf_ref))):
        gate = _sigmoid(_dot(hb, wg_ref[:, bi * d:(bi + 1) * d]))
        term = gate * _dot(y_ref[...], w_ref[...])
        merged = term if merged is None else merged + term
    out = _dot(merged.astype(BF16), wo_ref[...]).reshape(tb, ts, d)
    o_ref[...] = x + gt_ref[...] * _rms(out, gpost_ref[...])


def _merge(x, ada, g_pre, g_post, ym, ys, yf, wg, wm, ws, wf, wo):
    b, s, d = x.shape
    tb, ts = _x_tiles(b, s, 256)
    tm = tb * ts
    nj = s // ts
    row = lambda i, j: (i * nj + j, 0)
    wspec = lambda shape: pl.BlockSpec(shape, lambda i, j: (0, 0), pipeline_mode=pl.Buffered(1))
    return pl.pallas_call(
        _merge_kernel,
        grid=(b // tb, nj),
        in_specs=[pl.BlockSpec((tb, ts, d), lambda i, j: (i, j, 0)),
                  _ada_spec(tb, d, 0), _ada_spec(tb, d, 1), _ada_spec(tb, d, 2),
                  _const_spec((1, d)), _const_spec((1, d)),
                  pl.BlockSpec((tm, W_M), row), pl.BlockSpec((tm, D_INNER), row), pl.BlockSpec((tm, W_F), row),
                  wspec((d, 3 * d)), wspec((W_M, d)), wspec((D_INNER, d)), wspec((W_F, d)), wspec((d, d))],
        out_specs=pl.BlockSpec((tb, ts, d), lambda i, j: (i, j, 0)),
        out_shape=jax.ShapeDtypeStruct((b, s, d), F32),
        compiler_params=_cparams(("arbitrary", "arbitrary")),
        name="merge",
    )(x, ada, ada, ada, g_pre.reshape(1, d), g_post.reshape(1, d), ym, ys, yf, wg, wm, ws, wf, wo)


def _mlp_kernel(x_ref, sh_ref, sc_ref, gt_ref, gpre_ref, gpost_ref, wu_ref, wd_ref, o_ref):
    x = x_ref[...]
    tb, ts, d = x.shape
    h = _rms(x, gpre_ref[...]) * (1.0 + sc_ref[...]) + sh_ref[...]
    hb = h.reshape(tb * ts, d).astype(BF16)
    up = jnp.maximum(_dot(hb, wu_ref[...]), 0.0)
    f = _dot((up * up).astype(BF16), wd_ref[...]).reshape(tb, ts, d)
    o_ref[...] = x + gt_ref[...] * _rms(f, gpost_ref[...])


def _mlp(x, ada, g_pre, g_post, wu, wd):
    b, s, d = x.shape
    tb, ts = _x_tiles(b, s, 256)
    nj = s // ts
    dff = wu.shape[1]
    wspec = lambda shape: pl.BlockSpec(shape, lambda i, j: (0, 0), pipeline_mode=pl.Buffered(1))
    return pl.pallas_call(
        _mlp_kernel,
        grid=(b // tb, nj),
        in_specs=[pl.BlockSpec((tb, ts, d), lambda i, j: (i, j, 0)),
                  _ada_spec(tb, d, 3), _ada_spec(tb, d, 4), _ada_spec(tb, d, 5),
                  _const_spec((1, d)), _const_spec((1, d)),
                  wspec((d, dff)), wspec((dff, d))],
        out_specs=pl.BlockSpec((tb, ts, d), lambda i, j: (i, j, 0)),
        out_shape=jax.ShapeDtypeStruct((b, s, d), F32),
        compiler_params=_cparams(("arbitrary", "arbitrary")),
        name="mlp",
    )(x, ada, ada, ada, g_pre.reshape(1, d), g_post.reshape(1, d), wu, wd)


def _split_w_in(w):
    sizes = (NH_M * DQK_M, NH_M * DQK_M, W_M, NH_M, NH_M, W_M, D_INNER, CONV_CH, NH_S, W_F, W_F, W_F, NH_F)
    names = ("mq", "mk", "mv", "mi", "mf", "mo", "sz", "sxbc", "sdt", "fq", "fk", "fv", "ff")
    cols, off = {}, 0
    for nm, sz in zip(names, sizes):
        cols[nm] = w[:, off:off + sz]
        off += sz
    gates = w[:, off:]
    d = w.shape[0]
    zeros = lambda n: jnp.zeros((d, n), w.dtype)
    small = jnp.concatenate(
        [cols["mi"], cols["mf"], cols["sdt"], cols["ff"], cols["mf"], zeros(LN_CUM - LN_BM - NH_M),
         cols["sdt"], cols["ff"], zeros(LANES - LN_FT - NH_F)], axis=1)
    proj = jnp.concatenate([cols[sg[0]] for sg in SEGS[:-1]] + [small], axis=1)
    kv = jnp.concatenate([cols["fk"], cols["fv"]], axis=1)
    return proj.astype(BF16), kv.astype(BF16), gates.astype(BF16)


def _gate_lanes(b_mgate, dt_bias, b_ffox, a_log):
    z = lambda n: jnp.zeros((n,), F32)
    bias = jnp.concatenate([b_mgate, dt_bias, b_ffox, b_mgate[NH_M:], z(LN_CUM - LN_BM - NH_M),
                            dt_bias, b_ffox, z(LANES - LN_FT - NH_F)])
    alog = jnp.concatenate([z(LN_CUM), a_log, z(LANES - LN_CUM - NH_S)])
    return bias.reshape(1, LANES), alog.reshape(1, LANES)


def _mixer(x, ada, lw, state, attend, chunk, mm, kv_t, layer=0, depth=1, kv_all=None):
    b, s, d = x.shape
    u = _inproj(x, ada, lw["g_pre_mix"], lw["w_proj"], lw["w_kv"].T if kv_t else lw["w_kv"], kv_t,
                layer, depth, kv_all)
    g, gt = _prep(u["small"], lw["bias_lanes"], lw["alog_lanes"], s, chunk)
    if s % LANES:
        gt = gt.reshape(LANES, b, s).transpose(1, 0, 2)
    c0n, m0, conv0, h0 = state
    ym, c_n, m_new = _mlstm(u, g, gt, lw["g_mhead"], c0n, m0, b, s, chunk, mm)
    ys, h_new = _ssd(u, g, gt, lw["conv_w"], lw["conv_b"], lw["d_skip"], lw["g_ssm"], conv0, h0, b, s, chunk, mm)
    yf = attend(u, g, gt)
    x1 = _merge(x, ada, lw["g_pre_mix"], lw["g_post_mix"], ym, ys, yf,
                lw["w_gates"], lw["w_br_m"], lw["w_br_s"], lw["w_br_f"], lw["w_out"])
    x2 = _mlp(x1, ada, lw["g_pre_mlp"], lw["g_post_mlp"], lw["w_up"], lw["w_down"])
    if kv_t:
        rows = lambda a: a
    else:
        rows = lambda a: a.reshape(b, s, NH_F, DH_F)
    outs = (rows(u["fk"]), rows(u["fv"]),
            g[:, LN_FF:LN_FF + NH_F].reshape(b, s, NH_F),
            jnp.swapaxes(c_n[..., :DV_M], -1, -2), c_n[..., DV_M], m_new[:, :, 0, 0],
            u["sxbc"].reshape(b, s, CONV_CH)[:, s - (CONV_W - 1):, :], h_new)
    return x2, outs


def kernel(x_prompt, x_sample, cache_k, cache_v, cache_logf, state_mlstm_C, state_mlstm_n, state_mlstm_m,
           state_conv, state_ssm, page_table, c_prompt, c_sample, w_ada, b_ada, g_pre_mix, g_post_mix, w_in,
           b_mgate, b_ffox, g_mhead, conv_w, conv_b, dt_bias, a_log, d_skip, g_ssm, w_br_m, w_br_s, w_br_f,
           w_out, g_pre_mlp, g_post_mlp, w_up, w_down):
    depth = w_in.shape[0]
    bp, sp, d = x_prompt.shape
    bs, ss, _ = x_sample.shape
    n_phys, page = cache_k.shape[1], cache_k.shape[2]

    pad = (-(bp + bs)) % SUBLANES
    c_all = jnp.concatenate([c_prompt, c_sample, jnp.zeros((pad, d), F32)], axis=0)
    ada_all = _ada(c_all, w_ada, b_ada)

    ck_t = jnp.transpose(cache_k, (0, 1, 3, 4, 2)).reshape(depth * n_phys, W_F, page)
    cv_t = jnp.transpose(cache_v, (0, 1, 3, 4, 2)).reshape(depth * n_phys, W_F, page)
    clf_t = jnp.transpose(cache_logf, (0, 1, 3, 2)).reshape(depth * n_phys, NH_F, page)

    chunk_p = SCAN_CHUNK if sp % SCAN_CHUNK == 0 else (CHUNK if sp % CHUNK == 0 else sp)
    chunk_s = SCAN_CHUNK if ss % SCAN_CHUNK == 0 else (CHUNK if ss % CHUNK == 0 else ss)
    yp, ys = x_prompt, x_sample
    res_p, res_s = [], []
    kv_all = None
    for l in range(depth):
        w_proj, w_kv, w_gates = _split_w_in(w_in[l])
        bias_lanes, alog_lanes = _gate_lanes(b_mgate[l], dt_bias[l], b_ffox[l], a_log[l])
        lw = dict(w_proj=w_proj, w_kv=w_kv, w_gates=w_gates, bias_lanes=bias_lanes, alog_lanes=alog_lanes,
                  g_pre_mix=g_pre_mix[l], g_post_mix=g_post_mix[l], g_mhead=g_mhead[l],
                  conv_w=conv_w[l], conv_b=conv_b[l], d_skip=d_skip[l], g_ssm=g_ssm[l],
                  w_br_m=w_br_m[l].astype(BF16), w_br_s=w_br_s[l].astype(BF16), w_br_f=w_br_f[l].astype(BF16),
                  w_out=w_out[l].astype(BF16), g_pre_mlp=g_pre_mlp[l], g_post_mlp=g_post_mlp[l],
                  w_up=w_up[l].astype(BF16), w_down=w_down[l].astype(BF16))
        ada_p = ada_all[l, :bp][:, None, :]
        ada_s = ada_all[l, bp:bp + bs][:, None, :]

        zero_state = (jnp.zeros((bp, NH_M, DQK_M, 2 * DV_M), F32),
                      jnp.zeros((bp, NH_M, 1, LANES), F32), jnp.zeros((bp, SUBLANES, CONV_CH), F32),
                      jnp.zeros((bp, NH_S, P_S, D_STATE), F32))
        yp, outs = _mixer(yp, ada_p, lw, zero_state,
                          lambda u, g, gt, l=l: _fox_prompt(u, g, gt, bp, sp, l), chunk_p, BF16, True,
                          l, depth, kv_all)
        kv_all = outs[:2]
        res_p.append(outs)

        state = (jnp.concatenate([jnp.swapaxes(state_mlstm_C[l], -1, -2),
                                  jnp.broadcast_to(state_mlstm_n[l][..., None], (bs, NH_M, DQK_M, DV_M))], axis=-1),
                 jnp.broadcast_to(state_mlstm_m[l][:, :, None, None], (bs, NH_M, 1, LANES)),
                 jnp.pad(state_conv[l], ((0, 0), (SUBLANES - (CONV_W - 1), 0), (0, 0))),
                 state_ssm[l])
        ys, outs = _mixer(ys, ada_s, lw, state,
                          lambda u, g, gt, l=l: _fox_sample(u, gt, page_table, ck_t, cv_t, clf_t, l * n_phys, bs, ss),
                          chunk_s, F32, False)
        res_s.append(outs)

    stack = lambda res, i: jnp.stack([r[i] for r in res])
    kv_prompt = tuple(jnp.transpose(a.reshape(depth, bp, NH_F, DH_F, sp), (0, 1, 4, 2, 3)) for a in kv_all)
    return ((yp, ys) + kv_prompt + tuple(stack(res_p, i) for i in range(2, 8))
            + tuple(stack(res_s, i) for i in range(8)))
```

```python
import functools

import jax
import jax.numpy as jnp
from jax import lax
from jax.experimental import pallas as pl
from jax.experimental.pallas import tpu as pltpu

F32 = jnp.float32
BF16 = jnp.bfloat16

NH_M, DQK_M, DV_M = 4, 64, 128
W_M = NH_M * DV_M
GATE_CAP = 15.0
NH_S, P_S, N_GROUPS, D_STATE, CONV_W = 8, 64, 2, 128, 4
D_INNER = NH_S * P_S
CONV_CH = D_INNER + 2 * N_GROUPS * D_STATE
NH_F, DH_F = 8, 64
W_F = NH_F * DH_F
CHUNK = 64
SCAN_CHUNK = 256
EPS = 1e-6
LANES = 128
SUBLANES = 8

LN_LOGI, LN_LOGF, LN_DT, LN_FF, LN_BM, LN_CUM, LN_FT = 0, 4, 8, 16, 24, 32, 40

VMEM_LIMIT = 56 * 1024 * 1024


def _cparams(sem):
    return pltpu.CompilerParams(dimension_semantics=sem, vmem_limit_bytes=VMEM_LIMIT)


def _sigmoid(x):
    return 1.0 / (1.0 + jnp.exp(-x))


def _softplus(x):
    return jnp.maximum(x, 0.0) + jnp.log(1.0 + jnp.exp(-jnp.abs(x)))


def _rms(x, g):
    return x * lax.rsqrt(jnp.mean(x * x, axis=-1, keepdims=True) + EPS) * g


def _dot(a, b):
    return jnp.dot(a, b, preferred_element_type=F32)


def _dot_nt(a, b):
    return lax.dot_general(a, b, (((1,), (1,)), ((), ())), preferred_element_type=F32)


def _dot_tn(a, b):
    return lax.dot_general(a, b, (((0,), (0,)), ((), ())), preferred_element_type=F32)


def _const_spec(shape):
    nd = len(shape)
    return pl.BlockSpec(shape, lambda *_: (0,) * nd)


def _ada_kernel(c_ref, w_ref, b_ref, o_ref):
    c = c_ref[...]
    s = (c * _sigmoid(c)).astype(BF16)
    o_ref[0] = _dot(s, w_ref[0].astype(BF16)) + b_ref[0]


def _ada(c_all, w_ada, b_ada):
    depth, d, n = w_ada.shape
    rows = c_all.shape[0]
    tn = 512
    return pl.pallas_call(
        _ada_kernel,
        grid=(depth, n // tn),
        in_specs=[pl.BlockSpec((rows, d), lambda l, j: (0, 0)),
                  pl.BlockSpec((1, d, tn), lambda l, j: (l, 0, j)),
                  pl.BlockSpec((1, 1, tn), lambda l, j: (l, 0, j))],
        out_specs=pl.BlockSpec((1, rows, tn), lambda l, j: (l, 0, j)),
        out_shape=jax.ShapeDtypeStruct((depth, rows, n), F32),
        compiler_params=_cparams(("arbitrary", "arbitrary")),
        name="ada",
    )(c_all, w_ada, b_ada.reshape(depth, 1, n))


SEGS = (("mq", NH_M * DQK_M, DQK_M ** -0.5, F32), ("mk", NH_M * DQK_M, 1.0, F32), ("mv", W_M, 1.0, F32),
        ("mo", W_M, 1.0, F32), ("sz", D_INNER, 1.0, F32), ("sxbc", CONV_CH, 1.0, F32),
        ("fq", W_F, DH_F ** -0.5, BF16), ("small", LANES, 1.0, F32))
W_PROJ = sum(sg[1] for sg in SEGS)


def _inproj_kernel(x_ref, sh_ref, sc_ref, g_ref, w_ref, wkv_ref, *rest, kv_t, n_alias):
    outs = rest[n_alias:]
    x = x_ref[...]
    tb, ts, d = x.shape
    h = _rms(x, g_ref[...]) * (1.0 + sc_ref[...]) + sh_ref[...]
    hb = h.reshape(tb * ts, d).astype(BF16)
    off = 0
    for (_, wd, scale, _), o in zip(SEGS, outs):
        r = _dot(hb, w_ref[:, off:off + wd])
        if scale != 1.0:
            r = r * scale
        o[...] = r.astype(o.dtype)
        off += wd
    small_ref, smallt_ref, fk_ref, fv_ref = outs[len(SEGS) - 1:]
    smallt_ref[...] = small_ref[...].T
    if kv_t:
        kv = _dot_nt(wkv_ref[...], hb)
        fk_ref[0, 0] = kv[:W_F]
        fv_ref[0, 0] = kv[W_F:]
    else:
        kv = _dot(hb, wkv_ref[...])
        fk_ref[...] = kv[:, :W_F]
        fv_ref[...] = kv[:, W_F:]


def _x_tiles(b, s, ts_max):
    if s >= ts_max:
        return 1, ts_max
    tb = max(1, min(b, ts_max // s))
    return tb, s


def _ada_spec(tb, d, col):
    return pl.BlockSpec((tb, 1, d), lambda i, j: (i, 0, col))


def _inproj(x, ada, g_pre, w, wkv, kv_t, layer=0, depth=1, kv_all=None):
    b, s, d = x.shape
    tb, ts = _x_tiles(b, s, 256)
    tm = tb * ts
    nj = s // ts
    m = b * s
    row = lambda i, j: (i * nj + j, 0)
    out_shape = [jax.ShapeDtypeStruct((m, wd), dt) for _, wd, _, dt in SEGS]
    out_specs = [pl.BlockSpec((tm, wd), row) for _, wd, _, _ in SEGS]
    out_shape.append(jax.ShapeDtypeStruct((LANES, m), F32))
    out_specs.append(pl.BlockSpec((LANES, tm), lambda i, j: (0, i * nj + j)))
    in_specs = [pl.BlockSpec((tb, ts, d), lambda i, j: (i, j, 0)),
                _ada_spec(tb, d, 0), _ada_spec(tb, d, 1),
                _const_spec((1, d)),
                pl.BlockSpec((d, W_PROJ), lambda i, j: (0, 0), pipeline_mode=pl.Buffered(1)),
                pl.BlockSpec(wkv.shape, lambda i, j: (0, 0), pipeline_mode=pl.Buffered(1))]
    args = [x, ada, ada, g_pre.reshape(1, d), w, wkv]
    aliases = {}
    if kv_t:
        assert tb == 1
        out_shape += [jax.ShapeDtypeStruct((depth, b, W_F, s), F32)] * 2
        out_specs += [pl.BlockSpec((1, 1, W_F, ts), lambda i, j: (layer, i, 0, j))] * 2
        if kv_all is not None:
            aliases = {len(args): len(SEGS) + 1, len(args) + 1: len(SEGS) + 2}
            in_specs += [pl.BlockSpec(memory_space=pl.ANY)] * 2
            args += list(kv_all)
    else:
        out_shape += [jax.ShapeDtypeStruct((m, W_F), F32)] * 2
        out_specs += [pl.BlockSpec((tm, W_F), row)] * 2
    outs = pl.pallas_call(
        functools.partial(_inproj_kernel, kv_t=kv_t, n_alias=len(aliases)),
        grid=(b // tb, nj),
        in_specs=in_specs,
        out_specs=out_specs,
        out_shape=out_shape,
        input_output_aliases=aliases,
        compiler_params=_cparams(("arbitrary", "arbitrary")),
        name="inproj",
    )(*args)
    names = [sg[0] for sg in SEGS] + ["small_t", "fk", "fv"]
    return dict(zip(names, outs))


def _prep_kernel(small_ref, bias_ref, alog_ref, g_ref, gt_ref, carry_ref, *, chunk, seg, tiles_per_seq):
    i = pl.program_id(0)

    @pl.when(i % tiles_per_seq == 0)
    def _():
        carry_ref[...] = jnp.zeros_like(carry_ref)

    v = small_ref[...] + bias_ref[...]
    ts = v.shape[0]
    lane = lax.broadcasted_iota(jnp.int32, v.shape, 1)
    row = lax.broadcasted_iota(jnp.int32, v.shape, 0)
    capped = GATE_CAP * jnp.tanh(v * (1.0 / GATE_CAP))
    is_m = (lane < LN_DT) | ((lane >= LN_BM) & (lane < LN_CUM))
    vv = jnp.where(is_m, capped, v)
    sp = _softplus(vv)
    lsig = vv - sp
    a = -jnp.exp(alog_ref[...])
    val = jnp.where(lane < LN_LOGF, vv,
          jnp.where(lane < LN_DT, lsig,
          jnp.where(lane < LN_FF, sp,
          jnp.where(lane < LN_CUM, lsig,
          jnp.where(lane < LN_FT, sp * a, lsig)))))
    val = jnp.where(lane < LN_FT + NH_F, val, 0.0)
    ridx = jnp.where(lane < LN_FT, row % chunk, row % seg)
    ridx = jnp.where(lane >= LN_BM, ridx, -1)
    x = val
    k = 1
    while k < min(ts, max(chunk, seg)):
        x = x + jnp.where(ridx >= k, pltpu.roll(x, k, 0), 0.0)
        k *= 2
    x = x + jnp.where(lane >= LN_FT, carry_ref[...], 0.0)
    carry_ref[...] = x[ts - 1:ts, :]
    g_ref[...] = x
    gt_ref[...] = x.T


def _prep(small, bias_lanes, alog_lanes, s, chunk):
    m = small.shape[0]
    ts = min(512, m)
    if s >= ts:
        seg, tiles_per_seq = ts, s // ts
    else:
        seg, tiles_per_seq = s, 1
    return pl.pallas_call(
        functools.partial(_prep_kernel, chunk=chunk, seg=seg, tiles_per_seq=tiles_per_seq),
        grid=(m // ts,),
        in_specs=[pl.BlockSpec((ts, LANES), lambda i: (i, 0)), _const_spec((1, LANES)), _const_spec((1, LANES))],
        out_specs=[pl.BlockSpec((ts, LANES), lambda i: (i, 0)), pl.BlockSpec((LANES, ts), lambda i: (0, i))],
        out_shape=[jax.ShapeDtypeStruct((m, LANES), F32), jax.ShapeDtypeStruct((LANES, m), F32)],
        scratch_shapes=[pltpu.VMEM((1, LANES), F32)],
        compiler_params=_cparams(("arbitrary",)),
        name="prep",
    )(small, bias_lanes, alog_lanes)


def _mlstm_kernel(q_ref, k_ref, v_ref, mo_ref, g_ref, *rest, nb, chunk, gt3d, mm):
    gt_refs = rest[:nb]
    gh_ref, c0_ref, m0_ref, y_ref, c_ref, m_ref = rest[nb:]
    j = pl.program_id(1)

    @pl.when(j == 0)
    def _():
        c_ref[...] = c0_ref[...]
        m_ref[...] = m0_ref[...]

    ts = q_ref.shape[1]
    ln = chunk
    gts = [r[0] if gt3d else r[...] for r in gt_refs]
    rr = lax.broadcasted_iota(jnp.int32, (ln, ln), 0)
    cc = lax.broadcasted_iota(jnp.int32, (ln, ln), 1)
    causal = cc <= rr
    ones = jnp.ones((ln, DV_M), F32)
    units = [(bi, h) for bi in range(nb) for h in range(NH_M)]
    for c in range(ts // ln):
        lo, hi = c * ln, (c + 1) * ln
        g = [g_ref[bi, lo:hi, :] for bi in range(nb)]
        qb = [q_ref[bi, lo:hi, h * DQK_M:(h + 1) * DQK_M].astype(mm) for bi, h in units]
        kb = [k_ref[bi, lo:hi, h * DQK_M:(h + 1) * DQK_M].astype(mm) for bi, h in units]
        vf = [v_ref[bi, lo:hi, h * DV_M:(h + 1) * DV_M] for bi, h in units]
        b_c = [g[bi][:, LN_BM + h:LN_BM + h + 1] for bi, h in units]
        logi_c = [g[bi][:, LN_LOGI + h:LN_LOGI + h + 1] for bi, h in units]
        m_st = [m_ref[bi, h][:, 0:1] for bi, h in units]
        st = [c_ref[bi, h] for bi, h in units]
        un = range(len(units))
        qk = [_dot_nt(qb[u], kb[u]) for u in un]
        qc = [_dot(qb[u], st[u].astype(mm)) for u in un]
        dm = [jnp.where(causal, b_c[u] - gts[bi][LN_BM + h:LN_BM + h + 1, lo:hi]
                        + gts[bi][LN_LOGI + h:LN_LOGI + h + 1, lo:hi], -jnp.inf) for u, (bi, h) in enumerate(units)]
        inter = [b_c[u] + m_st[u] for u in un]
        m_t = [jnp.maximum(inter[u], jnp.max(dm[u], axis=1, keepdims=True)) for u in un]
        s = [(qk[u] * jnp.exp(dm[u] - m_t[u])).astype(mm) for u in un]
        sv = [_dot(s[u], jnp.concatenate([vf[u], ones], axis=1).astype(mm)) for u in un]
        m_new = [m_t[u][ln - 1:ln, :] for u in un]
        wk = [jnp.exp(b_c[u][ln - 1:ln, :] - b_c[u] + logi_c[u] - m_new[u]) for u in un]
        wv = [jnp.concatenate([wk[u] * vf[u], jnp.broadcast_to(wk[u], (ln, DV_M))], axis=1).astype(mm) for u in un]
        upd = [_dot_tn(kb[u], wv[u]) for u in un]
        for u, (bi, h) in enumerate(units):
            decay = jnp.exp(b_c[u][ln - 1:ln, :] + m_st[u] - m_new[u])
            c_ref[bi, h] = decay * st[u] + upd[u]
            m_ref[bi, h] = jnp.broadcast_to(m_new[u], (1, LANES))
        for u, (bi, h) in enumerate(units):
            both = sv[u] + jnp.exp(inter[u] - m_t[u]) * qc[u]
            num, den = both[:, :DV_M], both[:, DV_M:]
            hv = num / jnp.maximum(jnp.abs(den), jnp.exp(-m_t[u]))
            hm = _rms(hv, gh_ref[:, h * DV_M:(h + 1) * DV_M])
            y = hm * _sigmoid(mo_ref[bi, lo:hi, h * DV_M:(h + 1) * DV_M])
            y_ref[bi, lo:hi, h * DV_M:(h + 1) * DV_M] = y.astype(y_ref.dtype)


def _gt_specs(s, ts, nj, gt3d, nb):
    if gt3d:
        return [pl.BlockSpec((1, LANES, s), lambda i, j, bi=bi: (i * nb + bi, 0, 0)) for bi in range(nb)]
    return [pl.BlockSpec((LANES, ts), lambda i, j, bi=bi: (0, (i * nb + bi) * nj + j)) for bi in range(nb)]


def _seqs_per_step(b, ts):
    nb = 2 if ts >= CHUNK else 4
    return nb if b % nb == 0 else 1


def _mlstm(u, g, gt, g_mhead, c0n, m0, b, s, chunk, mm):
    ts = min(256, s)
    nj = s // ts
    gt3d = gt.ndim == 3
    nb = _seqs_per_step(b, ts)
    tok = lambda w: pl.BlockSpec((nb, ts, w), lambda i, j: (i, j, 0))
    st = lambda *shape: pl.BlockSpec((nb,) + shape, lambda i, j: (i,) + (0,) * len(shape))
    seq = lambda a: a.reshape(b, s, a.shape[-1])
    y, c_n, m_new = pl.pallas_call(
        functools.partial(_mlstm_kernel, nb=nb, chunk=chunk, gt3d=gt3d, mm=mm),
        grid=(b // nb, nj),
        in_specs=[tok(NH_M * DQK_M), tok(NH_M * DQK_M), tok(W_M), tok(W_M), tok(LANES)]
                 + _gt_specs(s, ts, nj, gt3d, nb)
                 + [_const_spec((1, W_M)), st(NH_M, DQK_M, 2 * DV_M), st(NH_M, 1, LANES)],
        out_specs=[tok(W_M), st(NH_M, DQK_M, 2 * DV_M), st(NH_M, 1, LANES)],
        out_shape=[jax.ShapeDtypeStruct((b, s, W_M), BF16),
                   jax.ShapeDtypeStruct((b, NH_M, DQK_M, 2 * DV_M), F32),
                   jax.ShapeDtypeStruct((b, NH_M, 1, LANES), F32)],
        compiler_params=_cparams(("arbitrary", "arbitrary")),
        name="mlstm",
    )(seq(u["mq"]), seq(u["mk"]), seq(u["mv"]), seq(u["mo"]), seq(g), *([gt] * nb),
      g_mhead.reshape(1, W_M), c0n, m0)
    return y.reshape(b * s, W_M), c_n, m_new


def _ssd_expansion(ln):
    stride = max(ln, LANES)
    r = lax.broadcasted_iota(jnp.int32, (LANES, 2 * D_INNER + NH_S * stride), 0)
    c = lax.broadcasted_iota(jnp.int32, (LANES, 2 * D_INNER + NH_S * stride), 1)
    src = jnp.where(c < D_INNER, LN_DT + c // P_S,
                    jnp.where(c < 2 * D_INNER, LN_CUM + (c - D_INNER) // P_S, LN_CUM + (c - 2 * D_INNER) // stride))
    return jnp.where(r == src, 1.0, 0.0).astype(BF16)


def _ssd_kernel(xbc_ref, sz_ref, g_ref, *rest, nb, chunk, gt3d, mm):
    gt_refs = rest[:nb]
    ex_ref, cw_ref, cb_ref, dsk_ref, gs_ref, conv0_ref, h0_ref, y_ref, hst_ref, tail_ref = rest[nb:]
    j = pl.program_id(1)

    @pl.when(j == 0)
    def _():
        hst_ref[...] = h0_ref[...]
        tail_ref[:, 0:SUBLANES, :] = conv0_ref[...]

    ts = xbc_ref.shape[1]
    ln = chunk
    rr = lax.broadcasted_iota(jnp.int32, (ln, ln), 0)
    cc = lax.broadcasted_iota(jnp.int32, (ln, ln), 1)
    causal = cc <= rr
    for bi in range(nb):
        _ssd_sequence(xbc_ref.at[bi], sz_ref.at[bi], g_ref.at[bi], gt_refs[bi][0] if gt3d else gt_refs[bi][...],
                      ex_ref, cw_ref, cb_ref, dsk_ref, gs_ref, y_ref.at[bi], hst_ref.at[bi], tail_ref.at[bi],
                      causal, ts, ln, mm)


def _ssd_sequence(xbc_ref, sz_ref, g_ref, gt, ex_ref, cw_ref, cb_ref, dsk_ref, gs_ref, y_ref, hst_ref, tail_ref,
                  causal, ts, ln, mm):
    x = xbc_ref[...]
    tail_ref[SUBLANES:, :] = x
    acc = cb_ref[...] + cw_ref[CONV_W - 1:CONV_W, :] * x
    for k in range(1, CONV_W):
        acc = acc + cw_ref[CONV_W - 1 - k:CONV_W - k, :] * tail_ref[SUBLANES - k:SUBLANES - k + ts, :]
    tail_ref[0:SUBLANES, :] = x[ts - SUBLANES:, :]
    xa = acc * _sigmoid(acc)

    hpg = NH_S // N_GROUPS
    for c in range(ts // ln):
        lo, hi = c * ln, (c + 1) * ln
        g = g_ref[lo:hi, :]
        heads, pairs = range(NH_S), range(NH_S // 2)
        ex_dt = BF16 if ln % (2 * SUBLANES) == 0 else F32
        full = sum(_dot(part.astype(ex_dt), ex_ref[...].astype(ex_dt)) for part in _split3(g))
        dt_full, cum_full = full[:, :D_INNER], full[:, D_INNER:2 * D_INNER]
        stride = max(ln, LANES)
        cum_b = [full[:, 2 * D_INNER + h * stride:2 * D_INNER + h * stride + ln] for h in heads]
        xs = xa[lo:hi, :D_INNER]
        xdt = xs * dt_full
        wkx = jnp.exp(cum_full[ln - 1:ln, :] - cum_full) * xdt
        bmg = [xa[lo:hi, D_INNER + grp * D_STATE:D_INNER + (grp + 1) * D_STATE].astype(mm) for grp in range(N_GROUPS)]
        co = D_INNER + N_GROUPS * D_STATE
        cmg = [xa[lo:hi, co + grp * D_STATE:co + (grp + 1) * D_STATE].astype(mm) for grp in range(N_GROUPS)]
        cbm = [_dot_nt(cmg[grp], bmg[grp]) for grp in range(N_GROUPS)]
        low_l = lax.broadcasted_iota(jnp.int32, (ln, 2 * P_S), 1) < P_S
        low_r = lax.broadcasted_iota(jnp.int32, (2 * P_S, D_STATE), 0) < P_S
        ys = []
        for hp in pairs:
            psl = slice(hp * 2 * P_S, (hp + 1) * 2 * P_S)
            grp = 2 * hp // hpg
            hst = jnp.concatenate([hst_ref[2 * hp], hst_ref[2 * hp + 1]], axis=0)
            ych = _dot_nt(cmg[grp], hst.astype(mm))
            xdt_p = xdt[:, psl].astype(mm)
            ycb = []
            for h in (2 * hp, 2 * hp + 1):
                ldec = jnp.exp(jnp.where(causal, cum_b[h] - gt[LN_CUM + h:LN_CUM + h + 1, lo:hi], -jnp.inf))
                ycb.append(_dot((cbm[grp] * ldec).astype(mm), xdt_p))
            upd = _dot_tn(wkx[:, psl].astype(mm), bmg[grp])
            d0 = jnp.exp(g[ln - 1:ln, LN_CUM + 2 * hp:LN_CUM + 2 * hp + 1])
            d1 = jnp.exp(g[ln - 1:ln, LN_CUM + 2 * hp + 1:LN_CUM + 2 * hp + 2])
            new = jnp.where(low_r, d0, d1) * hst + upd
            hst_ref[2 * hp] = new[:P_S]
            hst_ref[2 * hp + 1] = new[P_S:]
            ys.append(jnp.where(low_l, ycb[0], ycb[1]) + jnp.exp(cum_full[:, psl]) * ych
                      + dsk_ref[:, psl] * xs[:, psl])
        yy = jnp.concatenate(ys, axis=1)
        z = sz_ref[lo:hi, :]
        y_ref[lo:hi, :] = _rms(yy * (z * _sigmoid(z)), gs_ref[...]).astype(y_ref.dtype)


def _ssd(u, g, gt, conv_w, conv_b, d_skip, g_ssm, conv0, h0, b, s, chunk, mm):
    ts = min(256, s)
    nj = s // ts
    gt3d = gt.ndim == 3
    nb = _seqs_per_step(b, ts)
    ex = _ssd_expansion(chunk)
    tok = lambda w: pl.BlockSpec((nb, ts, w), lambda i, j: (i, j, 0))
    st = lambda *shape: pl.BlockSpec((nb,) + shape, lambda i, j: (i,) + (0,) * len(shape))
    seq = lambda a: a.reshape(b, s, a.shape[-1])
    y, h_new = pl.pallas_call(
        functools.partial(_ssd_kernel, nb=nb, chunk=chunk, gt3d=gt3d, mm=mm),
        grid=(b // nb, nj),
        in_specs=[tok(CONV_CH), tok(D_INNER), tok(LANES)] + _gt_specs(s, ts, nj, gt3d, nb)
                 + [_const_spec(ex.shape),
                    _const_spec((CONV_W, CONV_CH)), _const_spec((1, CONV_CH)),
                    _const_spec((1, D_INNER)), _const_spec((1, D_INNER)),
                    st(SUBLANES, CONV_CH), st(NH_S, P_S, D_STATE)],
        out_specs=[tok(D_INNER), st(NH_S, P_S, D_STATE)],
        out_shape=[jax.ShapeDtypeStruct((b, s, D_INNER), BF16),
                   jax.ShapeDtypeStruct((b, NH_S, P_S, D_STATE), F32)],
        scratch_shapes=[pltpu.VMEM((nb, SUBLANES + ts, CONV_CH), F32)],
        compiler_params=_cparams(("arbitrary", "arbitrary")),
        name="ssd",
    )(seq(u["sxbc"]), seq(u["sz"]), seq(g), *([gt] * nb), ex, conv_w, conv_b.reshape(1, CONV_CH),
      jnp.repeat(d_skip, P_S).reshape(1, D_INNER), g_ssm.reshape(1, D_INNER), conv0, h0)
    return y.reshape(b * s, D_INNER), h_new


def _split3(x):
    hi = x.astype(BF16).astype(F32)
    r = x - hi
    mid = r.astype(BF16).astype(F32)
    return hi, mid, r - mid


FOX_EXT = 16
FOX_VROWS = 2 * DH_F + FOX_EXT
FOX_SUB = 256
FOX_WIDE = 4
FOXS_PAGES = 16
FOX_SKIP = 112.0


def _foxp_kernel(q_ref, kt_ref, vt_ref, g_ref, ftr_ref, o_ref, kaug_ref, vaug_ref, stat_ref, *, tq):
    hp = pl.program_id(1)
    i = pl.program_id(2)
    hw = 2 * DH_F
    s_len = kaug_ref.shape[1]

    wide = FOX_WIDE * tq
    lane_row = lax.broadcasted_iota(jnp.int32, (1, LANES), 1)

    @pl.when(i == 0)
    def _():
        kb = kt_ref[0, 0].astype(BF16)
        kaug_ref[0:hw, :] = kb
        vaug_ref[0:hw, :] = vt_ref[0, 0].astype(BF16)
        ksq = kb.astype(F32) * kb.astype(F32)
        r = lax.broadcasted_iota(jnp.int32, (FOX_EXT, s_len), 0)
        ext = jnp.where(r < 3, 1.0, 0.0)
        for hh in range(2):
            ft_row = ftr_ref[pl.ds(hp * 2 + hh, 1), :]
            parts = _split3(ft_row)
            for pi, part in enumerate(parts):
                ext = jnp.where(r == 3 + 3 * hh + pi, -part, ext)
            ftq = jnp.full((1, LANES), -jnp.inf, F32)
            for jb in range(s_len // tq):
                ftq = jnp.where(lane_row == jb, ft_row[:, tq * (jb + 1) - 1:tq * (jb + 1)], ftq)
            stat_ref[hh:hh + 1, :] = ftq
            kn2 = jnp.sum(ksq[hh * DH_F:(hh + 1) * DH_F, :], axis=0, keepdims=True)
            stat_ref[2 + hh:3 + hh, :] = jnp.broadcast_to(jnp.sqrt(jnp.max(kn2, axis=1, keepdims=True)), (1, LANES))
        kaug_ref[hw:hw + FOX_EXT, :] = ext.astype(BF16)
        kaug_ref[hw + FOX_EXT:, :] = jnp.zeros((kaug_ref.shape[0] - hw - FOX_EXT, s_len), BF16)
        vaug_ref[hw:, :] = jnp.where(r == 0, 1.0, 0.0).astype(BF16)

    g = g_ref[...]
    q = q_ref[...]
    lane = lax.broadcasted_iota(jnp.int32, (tq, LANES), 1)
    low = lane < DH_F
    qaug = []
    qsq = q.astype(F32) * q.astype(F32)
    skippable = None
    for hh in range(2):
        mine = low if hh == 0 else ~low
        ft_c = jnp.sum(jnp.where(lane == LN_FT + hp * 2 + hh, g, 0.0), axis=1, keepdims=True)
        hi, mid, lo = _split3(ft_c)
        ext = jnp.where(lane == 0, hi, jnp.where(lane == 1, mid, jnp.where(lane == 2, lo, 0.0)))
        ext = jnp.where((lane >= 3 + 3 * hh) & (lane < 6 + 3 * hh), 1.0, ext)
        qm = jnp.where(mine, q, jnp.zeros_like(q))
        qaug.append(jnp.concatenate([qm, ext.astype(BF16)], axis=1))
        qmax = jnp.sqrt(jnp.max(jnp.sum(jnp.where(mine, qsq, 0.0), axis=1, keepdims=True), axis=0, keepdims=True))
        thr = ft_c[0:1, :] + 2.0 * qmax * stat_ref[2 + hh:3 + hh, 0:1] + FOX_SKIP
        cond = stat_ref[hh:hh + 1, :] > thr
        skippable = cond if skippable is None else skippable & cond
    n_skip = jnp.sum(jnp.where(skippable, 1, 0).astype(jnp.int32))

    rr = lax.broadcasted_iota(jnp.int32, (tq, tq), 0)
    cc = lax.broadcasted_iota(jnp.int32, (tq, tq), 1)
    causal = cc <= rr

    def block(carry, start, width, diag):
        subs = [pl.ds(pl.multiple_of(start + c * FOX_SUB, FOX_SUB), FOX_SUB) for c in range(width // FOX_SUB)]
        ms, ls, acc = carry
        new_m, new_l, alphas, pvs = [], [], [], []
        scores = []
        for hh in range(2):
            ss = []
            for c, sub in enumerate(subs):
                s = _dot(qaug[hh], kaug_ref[:, sub])
                dc = c * FOX_SUB - (width - tq)
                if diag and dc >= 0:
                    s = jnp.where(causal[:, dc:dc + FOX_SUB], s, -jnp.inf)
                ss.append(s)
            scores.append(ss)
        for hh in range(2):
            m_blk = jnp.max(functools.reduce(jnp.maximum, scores[hh]), axis=1, keepdims=True)
            new_m.append(jnp.maximum(ms[hh], m_blk))
            alphas.append(jnp.exp(ms[hh] - new_m[hh]))
        pv = [None, None]
        for c in range(len(subs)):
            for hh in range(2):
                p = jnp.exp(scores[hh][c] - new_m[hh]).astype(BF16)
                d = _dot_nt(p, vaug_ref[:, subs[c]])
                pv[hh] = d if pv[hh] is None else pv[hh] + d
        for hh in range(2):
            new_l.append(alphas[hh] * ls[hh] + pv[hh][:, hw:hw + 1])
            pvs.append(pv[hh][:, :hw])
        acc = jnp.where(low, alphas[0], alphas[1]) * acc + jnp.where(low, pvs[0], pvs[1])
        return tuple(new_m), tuple(new_l), acc

    neg = jnp.full((tq, 1), -jnp.inf, F32)
    zero = jnp.zeros((tq, 1), F32)
    init = ((neg, neg), (zero, zero), jnp.zeros((tq, hw), F32))
    live = i - n_skip
    base = n_skip * tq
    carry = lax.fori_loop(0, live // FOX_WIDE, lambda jb, c: block(c, base + jb * wide, wide, False), init)
    tail_start = base + (live // FOX_WIDE) * wide

    def tail(r):
        return lambda c: block(c, tail_start, (r + 1) * tq, True)

    def pick(lo, hi):
        if hi - lo == 1:
            return tail(lo)
        mid = (lo + hi) // 2
        return lambda c: lax.cond(live % FOX_WIDE < mid, pick(lo, mid), pick(mid, hi), c)

    _, ls, acc = pick(0, FOX_WIDE)(carry)
    o_ref[...] = (acc / jnp.where(low, ls[0], ls[1])).astype(o_ref.dtype)


def _fox_prompt(u, g, gt, b, s, layer):
    tq = min(512, s)
    nq = s // tq
    m = b * s
    hw = 2 * DH_F
    return pl.pallas_call(
        functools.partial(_foxp_kernel, tq=tq),
        grid=(b, NH_F // 2, nq),
        in_specs=[pl.BlockSpec((tq, hw), lambda bb, hp, i: (bb * nq + i, hp)),
                  pl.BlockSpec((1, 1, hw, s), lambda bb, hp, i: (layer, bb, hp, 0)),
                  pl.BlockSpec((1, 1, hw, s), lambda bb, hp, i: (layer, bb, hp, 0)),
                  pl.BlockSpec((tq, LANES), lambda bb, hp, i: (bb * nq + i, 0)),
                  pl.BlockSpec((SUBLANES, s), lambda bb, hp, i: (LN_FT // SUBLANES, bb))],
        out_specs=pl.BlockSpec((tq, hw), lambda bb, hp, i: (bb * nq + i, hp)),
        out_shape=jax.ShapeDtypeStruct((m, W_F), BF16),
        scratch_shapes=[pltpu.VMEM((2 * hw, s), BF16), pltpu.VMEM((FOX_VROWS, s), BF16),
                        pltpu.VMEM((SUBLANES, LANES), F32)],
        compiler_params=_cparams(("arbitrary", "arbitrary", "arbitrary")),
        name="fox_prompt",
    )(u["fq"], u["fk"], u["fv"], g, gt)


def _foxs_kernel(pt_ref, q_ref, kn_ref, vn_ref, gt_ref, *rest, pages):
    del pt_ref
    k_refs, v_refs, lf_refs = rest[:pages], rest[pages:2 * pages], rest[2 * pages:3 * pages]
    o_ref, qbd_ref, m_ref, l_ref, acc_ref, carry_ref, kcat_ref, vcat_ref = rest[3 * pages:]
    gi = pl.program_id(1)
    t = q_ref.shape[0]
    ht = NH_F * t
    page = k_refs[0].shape[2]

    def expand_heads(x):
        return jnp.broadcast_to(x[:, None, :], (NH_F, t, x.shape[1])).reshape(ht, x.shape[1])

    @pl.when(gi == 0)
    def _():
        q = q_ref[...].astype(F32)
        qt = jnp.broadcast_to(q[None], (NH_F, t, W_F)).reshape(ht, W_F)
        rh = lax.broadcasted_iota(jnp.int32, (ht, W_F), 0) // t
        lh = lax.broadcasted_iota(jnp.int32, (ht, W_F), 1) // DH_F
        qbd_ref[...] = jnp.where(rh == lh, qt, 0.0)
        m_ref[...] = jnp.full_like(m_ref, -jnp.inf)
        l_ref[...] = jnp.zeros_like(l_ref)
        acc_ref[...] = jnp.zeros_like(acc_ref)
        carry_ref[...] = jnp.zeros_like(carry_ref)

    cn_rows = expand_heads(gt_ref[0][LN_FT:LN_FT + NH_F, :])
    tq_idx = lax.broadcasted_iota(jnp.int32, (ht, t), 0) % t
    tk_idx = lax.broadcasted_iota(jnp.int32, (ht, t), 1)
    cn_col = jnp.sum(jnp.where(tq_idx == tk_idx, cn_rows, 0.0), axis=1, keepdims=True)

    qbd = qbd_ref[...]
    qbd_b = qbd.astype(BF16)

    def update(s, pv):
        m = m_ref[...]
        m_new = jnp.maximum(m, jnp.max(s, axis=1, keepdims=True))
        p = jnp.exp(s - m_new)
        alpha = jnp.exp(m - m_new)
        l_ref[...] = alpha * l_ref[...] + jnp.sum(p, axis=1, keepdims=True)
        acc_ref[...] = alpha * acc_ref[...] + pv(p)
        m_ref[...] = m_new

    lf_all = jnp.concatenate([lf_refs[pi][0] for pi in range(pages)], axis=0)
    tri = jnp.where(lax.broadcasted_iota(jnp.int32, (page, page), 0)
                    <= lax.broadcasted_iota(jnp.int32, (page, page), 1), 1.0, 0.0).astype(BF16)
    pre_all = sum(_dot(part.astype(BF16), tri) for part in _split3(lf_all))
    carry = carry_ref[...]
    rs = [None] * pages
    for pi in reversed(range(pages)):
        pre = pre_all[pi * NH_F:(pi + 1) * NH_F, :]
        tot = pre[:, page - 1:page]
        rs[pi] = expand_heads(carry + tot - pre)
        carry = carry + tot
        kcat_ref[:, pi * page:(pi + 1) * page] = k_refs[pi][0].astype(BF16)
        vcat_ref[:, pi * page:(pi + 1) * page] = v_refs[pi][0].astype(BF16)
    carry_ref[...] = carry
    s_all = _dot(qbd_b, kcat_ref[...]) + jnp.concatenate(rs, axis=1) + cn_col
    update(s_all, lambda p: _dot_nt(p.astype(BF16), vcat_ref[...]))

    @pl.when(gi == pl.num_programs(1) - 1)
    def _():
        s_new = _dot_nt(qbd, kn_ref[...]) + (cn_col - cn_rows)
        s_new = jnp.where(tk_idx <= tq_idx, s_new, -jnp.inf)
        update(s_new, lambda p: _dot(p, vn_ref[...]))
        o = acc_ref[...] / l_ref[...]
        lh = lax.broadcasted_iota(jnp.int32, (t, W_F), 1) // DH_F
        y = jnp.zeros((t, W_F), F32)
        for h in range(NH_F):
            y = y + jnp.where(lh == h, o[h * t:(h + 1) * t, :], 0.0)
        o_ref[...] = y.astype(o_ref.dtype)


def _fox_sample(u, gt3, page_table, ck_t, cv_t, clf_t, page_base, b, t):
    n_pages = page_table.shape[1]
    page = ck_t.shape[2]
    pages = FOXS_PAGES
    while n_pages % pages:
        pages //= 2
    ng = n_pages // pages
    ht = NH_F * t

    def page_spec(width_shape, pi):
        return pl.BlockSpec((1,) + width_shape,
                            lambda bb, gi, pt: (page_base + pt[bb, (ng - 1 - gi) * pages + pi], 0, 0))

    tok = pl.BlockSpec((t, W_F), lambda bb, gi, pt: (bb, 0))
    in_specs = [tok, tok, tok, pl.BlockSpec((1, LANES, t), lambda bb, gi, pt: (bb, 0, 0))]
    in_specs += [page_spec((W_F, page), pi) for pi in range(pages)]
    in_specs += [page_spec((W_F, page), pi) for pi in range(pages)]
    in_specs += [page_spec((NH_F, page), pi) for pi in range(pages)]
    grid_spec = pltpu.PrefetchScalarGridSpec(
        num_scalar_prefetch=1,
        grid=(b, ng),
        in_specs=in_specs,
        out_specs=pl.BlockSpec((t, W_F), lambda bb, gi, pt: (bb, 0)),
        scratch_shapes=[pltpu.VMEM((ht, W_F), F32), pltpu.VMEM((ht, 1), F32), pltpu.VMEM((ht, 1), F32),
                        pltpu.VMEM((ht, W_F), F32), pltpu.VMEM((NH_F, 1), F32),
                        pltpu.VMEM((W_F, pages * page), BF16), pltpu.VMEM((W_F, pages * page), BF16)],
    )
    return pl.pallas_call(
        functools.partial(_foxs_kernel, pages=pages),
        grid_spec=grid_spec,
        out_shape=jax.ShapeDtypeStruct((b * t, W_F), BF16),
        compiler_params=_cparams(("arbitrary", "arbitrary")),
        name="fox_sample",
    )(page_table, u["fq"], u["fk"], u["fv"], gt3, *([ck_t] * pages), *([cv_t] * pages), *([clf_t] * pages))


def _merge_kernel(x_ref, sh_ref, sc_ref, gt_ref, gpre_ref, gpost_ref, ym_ref, ys_ref, yf_ref,
                  wg_ref, wm_ref, ws_ref, wf_ref, wo_ref, o_ref):
    x = x_ref[...]
    tb, ts, d = x.shape
    h = _rms(x, gpre_ref[...]) * (1.0 + sc_ref[...]) + sh_ref[...]
    hb = h.reshape(tb * ts, d).astype(BF16)
    merged = None
    for bi, (y_ref, w_ref) in enumerate(((ym_ref, wm_ref), (ys_ref, ws_ref), (yf_ref, wf_ref))):
        gate = _sigmoid(_dot(hb, wg_ref[:, bi * d:(bi + 1) * d]))
        term = gate * _dot(y_ref[...], w_ref[...])
        merged = term if merged is None else merged + term
    out = _dot(merged.astype(BF16), wo_ref[...]).reshape(tb, ts, d)
    o_ref[...] = x + gt_ref[...] * _rms(out, gpost_ref[...])


def _merge(x, ada, g_pre, g_post, ym, ys, yf, wg, wm, ws, wf, wo):
    b, s, d = x.shape
    tb, ts = _x_tiles(b, s, 512)
    tm = tb * ts
    nj = s // ts
    row = lambda i, j: (i * nj + j, 0)
    wspec = lambda shape: pl.BlockSpec(shape, lambda i, j: (0, 0), pipeline_mode=pl.Buffered(1))
    return pl.pallas_call(
        _merge_kernel,
        grid=(b // tb, nj),
        in_specs=[pl.BlockSpec((tb, ts, d), lambda i, j: (i, j, 0)),
                  _ada_spec(tb, d, 0), _ada_spec(tb, d, 1), _ada_spec(tb, d, 2),
                  _const_spec((1, d)), _const_spec((1, d)),
                  pl.BlockSpec((tm, W_M), row), pl.BlockSpec((tm, D_INNER), row), pl.BlockSpec((tm, W_F), row),
                  wspec((d, 3 * d)), wspec((W_M, d)), wspec((D_INNER, d)), wspec((W_F, d)), wspec((d, d))],
        out_specs=pl.BlockSpec((tb, ts, d), lambda i, j: (i, j, 0)),
        out_shape=jax.ShapeDtypeStruct((b, s, d), F32),
        compiler_params=_cparams(("arbitrary", "arbitrary")),
        name="merge",
    )(x, ada, ada, ada, g_pre.reshape(1, d), g_post.reshape(1, d), ym, ys, yf, wg, wm, ws, wf, wo)


def _mlp_kernel(x_ref, sh_ref, sc_ref, gt_ref, gpre_ref, gpost_ref, wu_ref, wd_ref, o_ref):
    x = x_ref[...]
    tb, ts, d = x.shape
    h = _rms(x, gpre_ref[...]) * (1.0 + sc_ref[...]) + sh_ref[...]
    hb = h.reshape(tb * ts, d).astype(BF16)
    up = jnp.maximum(_dot(hb, wu_ref[...]), 0.0)
    f = _dot((up * up).astype(BF16), wd_ref[...]).reshape(tb, ts, d)
    o_ref[...] = x + gt_ref[...] * _rms(f, gpost_ref[...])


def _mlp(x, ada, g_pre, g_post, wu, wd):
    b, s, d = x.shape
    tb, ts = _x_tiles(b, s, 512)
    nj = s // ts
    dff = wu.shape[1]
    wspec = lambda shape: pl.BlockSpec(shape, lambda i, j: (0, 0), pipeline_mode=pl.Buffered(1))
    return pl.pallas_call(
        _mlp_kernel,
        grid=(b // tb, nj),
        in_specs=[pl.BlockSpec((tb, ts, d), lambda i, j: (i, j, 0)),
                  _ada_spec(tb, d, 3), _ada_spec(tb, d, 4), _ada_spec(tb, d, 5),
                  _const_spec((1, d)), _const_spec((1, d)),
                  wspec((d, dff)), wspec((dff, d))],
        out_specs=pl.BlockSpec((tb, ts, d), lambda i, j: (i, j, 0)),
        out_shape=jax.ShapeDtypeStruct((b, s, d), F32),
        compiler_params=_cparams(("arbitrary", "arbitrary")),
        name="mlp",
    )(x, ada, ada, ada, g_pre.reshape(1, d), g_post.reshape(1, d), wu, wd)


def _split_w_in(w):
    sizes = (NH_M * DQK_M, NH_M * DQK_M, W_M, NH_M, NH_M, W_M, D_INNER, CONV_CH, NH_S, W_F, W_F, W_F, NH_F)
    names = ("mq", "mk", "mv", "mi", "mf", "mo", "sz", "sxbc", "sdt", "fq", "fk", "fv", "ff")
    cols, off = {}, 0
    for nm, sz in zip(names, sizes):
        cols[nm] = w[:, off:off + sz]
        off += sz
    gates = w[:, off:]
    d = w.shape[0]
    zeros = lambda n: jnp.zeros((d, n), w.dtype)
    small = jnp.concatenate(
        [cols["mi"], cols["mf"], cols["sdt"], cols["ff"], cols["mf"], zeros(LN_CUM - LN_BM - NH_M),
         cols["sdt"], cols["ff"], zeros(LANES - LN_FT - NH_F)], axis=1)
    proj = jnp.concatenate([cols[sg[0]] for sg in SEGS[:-1]] + [small], axis=1)
    kv = jnp.concatenate([cols["fk"], cols["fv"]], axis=1)
    return proj.astype(BF16), kv.astype(BF16), gates.astype(BF16)


def _gate_lanes(b_mgate, dt_bias, b_ffox, a_log):
    z = lambda n: jnp.zeros((n,), F32)
    bias = jnp.concatenate([b_mgate, dt_bias, b_ffox, b_mgate[NH_M:], z(LN_CUM - LN_BM - NH_M),
                            dt_bias, b_ffox, z(LANES - LN_FT - NH_F)])
    alog = jnp.concatenate([z(LN_CUM), a_log, z(LANES - LN_CUM - NH_S)])
    return bias.reshape(1, LANES), alog.reshape(1, LANES)


def _mixer(x, ada, lw, state, attend, chunk, mm, kv_t, layer=0, depth=1, kv_all=None):
    b, s, d = x.shape
    u = _inproj(x, ada, lw["g_pre_mix"], lw["w_proj"], lw["w_kv"].T if kv_t else lw["w_kv"], kv_t,
                layer, depth, kv_all)
    g, gt = _prep(u["small"], lw["bias_lanes"], lw["alog_lanes"], s, chunk)
    if s % LANES:
        gt = gt.reshape(LANES, b, s).transpose(1, 0, 2)
    c0n, m0, conv0, h0 = state
    ym, c_n, m_new = _mlstm(u, g, gt, lw["g_mhead"], c0n, m0, b, s, chunk, mm)
    ys, h_new = _ssd(u, g, gt, lw["conv_w"], lw["conv_b"], lw["d_skip"], lw["g_ssm"], conv0, h0, b, s, chunk, mm)
    yf = attend(u, g, gt)
    x1 = _merge(x, ada, lw["g_pre_mix"], lw["g_post_mix"], ym, ys, yf,
                lw["w_gates"], lw["w_br_m"], lw["w_br_s"], lw["w_br_f"], lw["w_out"])
    x2 = _mlp(x1, ada, lw["g_pre_mlp"], lw["g_post_mlp"], lw["w_up"], lw["w_down"])
    if kv_t:
        rows = lambda a: a
    else:
        rows = lambda a: a.reshape(b, s, NH_F, DH_F)
    outs = (rows(u["fk"]), rows(u["fv"]),
            g[:, LN_FF:LN_FF + NH_F].reshape(b, s, NH_F),
            jnp.swapaxes(c_n[..., :DV_M], -1, -2), c_n[..., DV_M], m_new[:, :, 0, 0],
            u["sxbc"].reshape(b, s, CONV_CH)[:, s - (CONV_W - 1):, :], h_new)
    return x2, outs


def kernel(x_prompt, x_sample, cache_k, cache_v, cache_logf, state_mlstm_C, state_mlstm_n, state_mlstm_m,
           state_conv, state_ssm, page_table, c_prompt, c_sample, w_ada, b_ada, g_pre_mix, g_post_mix, w_in,
           b_mgate, b_ffox, g_mhead, conv_w, conv_b, dt_bias, a_log, d_skip, g_ssm, w_br_m, w_br_s, w_br_f,
           w_out, g_pre_mlp, g_post_mlp, w_up, w_down):
    depth = w_in.shape[0]
    bp, sp, d = x_prompt.shape
    bs, ss, _ = x_sample.shape
    n_phys, page = cache_k.shape[1], cache_k.shape[2]

    pad = (-(bp + bs)) % SUBLANES
    c_all = jnp.concatenate([c_prompt, c_sample, jnp.zeros((pad, d), F32)], axis=0)
    ada_all = _ada(c_all, w_ada, b_ada)

    ck_t = jnp.transpose(cache_k, (0, 1, 3, 4, 2)).reshape(depth * n_phys, W_F, page)
    cv_t = jnp.transpose(cache_v, (0, 1, 3, 4, 2)).reshape(depth * n_phys, W_F, page)
    clf_t = jnp.transpose(cache_logf, (0, 1, 3, 2)).reshape(depth * n_phys, NH_F, page)

    chunk_p = SCAN_CHUNK if sp % SCAN_CHUNK == 0 else (CHUNK if sp % CHUNK == 0 else sp)
    chunk_s = SCAN_CHUNK if ss % SCAN_CHUNK == 0 else (CHUNK if ss % CHUNK == 0 else ss)
    yp, ys = x_prompt, x_sample
    res_p, res_s = [], []
    kv_all = None
    for l in range(depth):
        w_proj, w_kv, w_gates = _split_w_in(w_in[l])
        bias_lanes, alog_lanes = _gate_lanes(b_mgate[l], dt_bias[l], b_ffox[l], a_log[l])
        lw = dict(w_proj=w_proj, w_kv=w_kv, w_gates=w_gates, bias_lanes=bias_lanes, alog_lanes=alog_lanes,
                  g_pre_mix=g_pre_mix[l], g_post_mix=g_post_mix[l], g_mhead=g_mhead[l],
                  conv_w=conv_w[l], conv_b=conv_b[l], d_skip=d_skip[l], g_ssm=g_ssm[l],
                  w_br_m=w_br_m[l].astype(BF16), w_br_s=w_br_s[l].astype(BF16), w_br_f=w_br_f[l].astype(BF16),
                  w_out=w_out[l].astype(BF16), g_pre_mlp=g_pre_mlp[l], g_post_mlp=g_post_mlp[l],
                  w_up=w_up[l].astype(BF16), w_down=w_down[l].astype(BF16))
        ada_p = ada_all[l, :bp][:, None, :]
        ada_s = ada_all[l, bp:bp + bs][:, None, :]

        zero_state = (jnp.zeros((bp, NH_M, DQK_M, 2 * DV_M), F32),
                      jnp.zeros((bp, NH_M, 1, LANES), F32), jnp.zeros((bp, SUBLANES, CONV_CH), F32),
                      jnp.zeros((bp, NH_S, P_S, D_STATE), F32))
        yp, outs = _mixer(yp, ada_p, lw, zero_state,
                          lambda u, g, gt, l=l: _fox_prompt(u, g, gt, bp, sp, l), chunk_p, BF16, True,
                          l, depth, kv_all)
        kv_all = outs[:2]
        res_p.append(outs)

        state = (jnp.concatenate([jnp.swapaxes(state_mlstm_C[l], -1, -2),
                                  jnp.broadcast_to(state_mlstm_n[l][..., None], (bs, NH_M, DQK_M, DV_M))], axis=-1),
                 jnp.broadcast_to(state_mlstm_m[l][:, :, None, None], (bs, NH_M, 1, LANES)),
                 jnp.pad(state_conv[l], ((0, 0), (SUBLANES - (CONV_W - 1), 0), (0, 0))),
                 state_ssm[l])
        ys, outs = _mixer(ys, ada_s, lw, state,
                          lambda u, g, gt, l=l: _fox_sample(u, gt, page_table, ck_t, cv_t, clf_t, l * n_phys, bs, ss),
                          chunk_s, F32, False)
        res_s.append(outs)

    stack = lambda res, i: jnp.stack([r[i] for r in res])
    kv_prompt = tuple(jnp.transpose(a.reshape(depth, bp, NH_F, DH_F, sp), (0, 1, 4, 2, 3)) for a in kv_all)
    return ((yp, ys) + kv_prompt + tuple(stack(res_p, i) for i in range(2, 8))
            + tuple(stack(res_s, i) for i in range(8)))
```

```python
import functools

import jax
import jax.numpy as jnp
from jax import lax
from jax.experimental import pallas as pl
from jax.experimental.pallas import tpu as pltpu

F32 = jnp.float32
BF16 = jnp.bfloat16

NH_M, DQK_M, DV_M = 4, 64, 128
W_M = NH_M * DV_M
GATE_CAP = 15.0
NH_S, P_S, N_GROUPS, D_STATE, CONV_W = 8, 64, 2, 128, 4
D_INNER = NH_S * P_S
CONV_CH = D_INNER + 2 * N_GROUPS * D_STATE
NH_F, DH_F = 8, 64
W_F = NH_F * DH_F
CHUNK = 64
SCAN_CHUNK = 256
EPS = 1e-6
LANES = 128
SUBLANES = 8

LN_LOGI, LN_LOGF, LN_DT, LN_FF, LN_BM, LN_CUM, LN_FT = 0, 4, 8, 16, 24, 32, 40

VMEM_LIMIT = 56 * 1024 * 1024


def _cparams(sem):
    return pltpu.CompilerParams(dimension_semantics=sem, vmem_limit_bytes=VMEM_LIMIT)


def _sigmoid(x):
    return 1.0 / (1.0 + jnp.exp(-x))


def _softplus(x):
    return jnp.maximum(x, 0.0) + jnp.log(1.0 + jnp.exp(-jnp.abs(x)))


def _rms(x, g):
    return x * lax.rsqrt(jnp.mean(x * x, axis=-1, keepdims=True) + EPS) * g


def _dot(a, b):
    return jnp.dot(a, b, preferred_element_type=F32)


def _dot_nt(a, b):
    return lax.dot_general(a, b, (((1,), (1,)), ((), ())), preferred_element_type=F32)


def _dot_tn(a, b):
    return lax.dot_general(a, b, (((0,), (0,)), ((), ())), preferred_element_type=F32)


def _const_spec(shape):
    nd = len(shape)
    return pl.BlockSpec(shape, lambda *_: (0,) * nd)


def _ada_kernel(c_ref, w_ref, b_ref, o_ref):
    c = c_ref[...]
    s = (c * _sigmoid(c)).astype(BF16)
    o_ref[0] = _dot(s, w_ref[0].astype(BF16)) + b_ref[0]


def _ada(c_all, w_ada, b_ada):
    depth, d, n = w_ada.shape
    rows = c_all.shape[0]
    tn = 512
    return pl.pallas_call(
        _ada_kernel,
        grid=(depth, n // tn),
        in_specs=[pl.BlockSpec((rows, d), lambda l, j: (0, 0)),
                  pl.BlockSpec((1, d, tn), lambda l, j: (l, 0, j)),
                  pl.BlockSpec((1, 1, tn), lambda l, j: (l, 0, j))],
        out_specs=pl.BlockSpec((1, rows, tn), lambda l, j: (l, 0, j)),
        out_shape=jax.ShapeDtypeStruct((depth, rows, n), F32),
        compiler_params=_cparams(("arbitrary", "arbitrary")),
        name="ada",
    )(c_all, w_ada, b_ada.reshape(depth, 1, n))


SEGS = (("mq", NH_M * DQK_M, DQK_M ** -0.5, F32), ("mk", NH_M * DQK_M, 1.0, F32), ("mv", W_M, 1.0, F32),
        ("mo", W_M, 1.0, F32), ("sz", D_INNER, 1.0, F32), ("sxbc", CONV_CH, 1.0, F32),
        ("fq", W_F, DH_F ** -0.5, BF16), ("small", LANES, 1.0, F32))
W_PROJ = sum(sg[1] for sg in SEGS)


def _inproj_kernel(x_ref, sh_ref, sc_ref, g_ref, w_ref, wkv_ref, *rest, kv_t, n_alias, layer):
    outs = rest[n_alias:]
    x = x_ref[...]
    tb, ts, d = x.shape
    h = _rms(x, g_ref[...]) * (1.0 + sc_ref[...]) + sh_ref[...]
    hb = h.reshape(tb * ts, d).astype(BF16)
    off = 0
    for (_, wd, scale, _), o in zip(SEGS, outs):
        r = _dot(hb, w_ref[:, off:off + wd])
        if scale != 1.0:
            r = r * scale
        o[...] = r.astype(o.dtype)
        off += wd
    small_ref, smallt_ref, fk_ref, fv_ref = outs[len(SEGS) - 1:]
    smallt_ref[...] = small_ref[...].T
    if kv_t:
        kv = _dot_nt(wkv_ref[...], hb)
        own = layer if fk_ref.shape[0] > 1 else 0
        for o_ref, val in ((fk_ref, kv[:W_F]), (fv_ref, kv[W_F:])):
            for dl in range(o_ref.shape[0]):
                o_ref[dl, 0] = val if dl == own else jnp.zeros_like(val)
    else:
        kv = _dot(hb, wkv_ref[...])
        fk_ref[...] = kv[:, :W_F]
        fv_ref[...] = kv[:, W_F:]


def _x_tiles(b, s, ts_max):
    if s >= ts_max:
        return 1, ts_max
    tb = max(1, min(b, ts_max // s))
    return tb, s


def _ada_spec(tb, d, col):
    return pl.BlockSpec((tb, 1, d), lambda i, j: (i, 0, col))


def _inproj(x, ada, g_pre, w, wkv, kv_t, layer=0, depth=1, kv_all=None):
    b, s, d = x.shape
    tb, ts = _x_tiles(b, s, 256)
    tm = tb * ts
    nj = s // ts
    m = b * s
    row = lambda i, j: (i * nj + j, 0)
    out_shape = [jax.ShapeDtypeStruct((m, wd), dt) for _, wd, _, dt in SEGS]
    out_specs = [pl.BlockSpec((tm, wd), row) for _, wd, _, _ in SEGS]
    out_shape.append(jax.ShapeDtypeStruct((LANES, m), F32))
    out_specs.append(pl.BlockSpec((LANES, tm), lambda i, j: (0, i * nj + j)))
    in_specs = [pl.BlockSpec((tb, ts, d), lambda i, j: (i, j, 0)),
                _ada_spec(tb, d, 0), _ada_spec(tb, d, 1),
                _const_spec((1, d)),
                pl.BlockSpec((d, W_PROJ), lambda i, j: (0, 0), pipeline_mode=pl.Buffered(1)),
                pl.BlockSpec(wkv.shape, lambda i, j: (0, 0), pipeline_mode=pl.Buffered(1))]
    args = [x, ada, ada, g_pre.reshape(1, d), w, wkv]
    aliases = {}
    if kv_t:
        assert tb == 1
        out_shape += [jax.ShapeDtypeStruct((depth, b, W_F, s), F32)] * 2
        if kv_all is None:
            out_specs += [pl.BlockSpec((depth, 1, W_F, ts), lambda i, j: (0, i, 0, j))] * 2
        else:
            out_specs += [pl.BlockSpec((1, 1, W_F, ts), lambda i, j: (layer, i, 0, j))] * 2
        if kv_all is not None:
            aliases = {len(args): len(SEGS) + 1, len(args) + 1: len(SEGS) + 2}
            in_specs += [pl.BlockSpec(memory_space=pl.ANY)] * 2
            args += list(kv_all)
    else:
        out_shape += [jax.ShapeDtypeStruct((m, W_F), F32)] * 2
        out_specs += [pl.BlockSpec((tm, W_F), row)] * 2
    outs = pl.pallas_call(
        functools.partial(_inproj_kernel, kv_t=kv_t, n_alias=len(aliases), layer=layer),
        grid=(b // tb, nj),
        in_specs=in_specs,
        out_specs=out_specs,
        out_shape=out_shape,
        input_output_aliases=aliases,
        compiler_params=_cparams(("arbitrary", "arbitrary")),
        name="inproj",
    )(*args)
    names = [sg[0] for sg in SEGS] + ["small_t", "fk", "fv"]
    return dict(zip(names, outs))


def _prep_kernel(small_ref, bias_ref, alog_ref, g_ref, gt_ref, carry_ref, *, chunk, seg, tiles_per_seq):
    i = pl.program_id(0)

    @pl.when(i % tiles_per_seq == 0)
    def _():
        carry_ref[...] = jnp.zeros_like(carry_ref)

    v = small_ref[...] + bias_ref[...]
    ts = v.shape[0]
    lane = lax.broadcasted_iota(jnp.int32, v.shape, 1)
    row = lax.broadcasted_iota(jnp.int32, v.shape, 0)
    capped = GATE_CAP * jnp.tanh(v * (1.0 / GATE_CAP))
    is_m = (lane < LN_DT) | ((lane >= LN_BM) & (lane < LN_CUM))
    vv = jnp.where(is_m, capped, v)
    sp = _softplus(vv)
    lsig = vv - sp
    a = -jnp.exp(alog_ref[...])
    val = jnp.where(lane < LN_LOGF, vv,
          jnp.where(lane < LN_DT, lsig,
          jnp.where(lane < LN_FF, sp,
          jnp.where(lane < LN_CUM, lsig,
          jnp.where(lane < LN_FT, sp * a, lsig)))))
    val = jnp.where(lane < LN_FT + NH_F, val, 0.0)
    ridx = jnp.where(lane < LN_FT, row % chunk, row % seg)
    ridx = jnp.where(lane >= LN_BM, ridx, -1)
    x = val
    k = 1
    while k < min(ts, max(chunk, seg)):
        x = x + jnp.where(ridx >= k, pltpu.roll(x, k, 0), 0.0)
        k *= 2
    x = x + jnp.where(lane >= LN_FT, carry_ref[...], 0.0)
    carry_ref[...] = x[ts - 1:ts, :]
    g_ref[...] = x
    gt_ref[...] = x.T


def _prep(small, bias_lanes, alog_lanes, s, chunk):
    m = small.shape[0]
    ts = min(512, m)
    if s >= ts:
        seg, tiles_per_seq = ts, s // ts
    else:
        seg, tiles_per_seq = s, 1
    return pl.pallas_call(
        functools.partial(_prep_kernel, chunk=chunk, seg=seg, tiles_per_seq=tiles_per_seq),
        grid=(m // ts,),
        in_specs=[pl.BlockSpec((ts, LANES), lambda i: (i, 0)), _const_spec((1, LANES)), _const_spec((1, LANES))],
        out_specs=[pl.BlockSpec((ts, LANES), lambda i: (i, 0)), pl.BlockSpec((LANES, ts), lambda i: (0, i))],
        out_shape=[jax.ShapeDtypeStruct((m, LANES), F32), jax.ShapeDtypeStruct((LANES, m), F32)],
        scratch_shapes=[pltpu.VMEM((1, LANES), F32)],
        compiler_params=_cparams(("arbitrary",)),
        name="prep",
    )(small, bias_lanes, alog_lanes)


def _mlstm_kernel(q_ref, k_ref, v_ref, mo_ref, g_ref, *rest, nb, chunk, gt3d, mm):
    gt_refs = rest[:nb]
    gh_ref, c0_ref, m0_ref, y_ref, c_ref, m_ref = rest[nb:]
    j = pl.program_id(1)

    @pl.when(j == 0)
    def _():
        c_ref[...] = c0_ref[...]
        m_ref[...] = m0_ref[...]

    ts = q_ref.shape[1]
    ln = chunk
    gts = [r[0] if gt3d else r[...] for r in gt_refs]
    rr = lax.broadcasted_iota(jnp.int32, (ln, ln), 0)
    cc = lax.broadcasted_iota(jnp.int32, (ln, ln), 1)
    causal = cc <= rr
    ones = jnp.ones((ln, DV_M), F32)
    units = [(bi, h) for bi in range(nb) for h in range(NH_M)]
    for c in range(ts // ln):
        lo, hi = c * ln, (c + 1) * ln
        g = [g_ref[bi, lo:hi, :] for bi in range(nb)]
        qb = [q_ref[bi, lo:hi, h * DQK_M:(h + 1) * DQK_M].astype(mm) for bi, h in units]
        kb = [k_ref[bi, lo:hi, h * DQK_M:(h + 1) * DQK_M].astype(mm) for bi, h in units]
        vf = [v_ref[bi, lo:hi, h * DV_M:(h + 1) * DV_M] for bi, h in units]
        b_c = [g[bi][:, LN_BM + h:LN_BM + h + 1] for bi, h in units]
        logi_c = [g[bi][:, LN_LOGI + h:LN_LOGI + h + 1] for bi, h in units]
        m_st = [m_ref[bi, h][:, 0:1] for bi, h in units]
        st = [c_ref[bi, h] for bi, h in units]
        un = range(len(units))
        qk = [_dot_nt(qb[u], kb[u]) for u in un]
        qc = [_dot(qb[u], st[u].astype(mm)) for u in un]
        dm = [jnp.where(causal, b_c[u] - gts[bi][LN_BM + h:LN_BM + h + 1, lo:hi]
                        + gts[bi][LN_LOGI + h:LN_LOGI + h + 1, lo:hi], -jnp.inf) for u, (bi, h) in enumerate(units)]
        inter = [b_c[u] + m_st[u] for u in un]
        m_t = [jnp.maximum(inter[u], jnp.max(dm[u], axis=1, keepdims=True)) for u in un]
        s = [(qk[u] * jnp.exp(dm[u] - m_t[u])).astype(mm) for u in un]
        sv = [_dot(s[u], jnp.concatenate([vf[u], ones], axis=1).astype(mm)) for u in un]
        m_new = [m_t[u][ln - 1:ln, :] for u in un]
        wk = [jnp.exp(b_c[u][ln - 1:ln, :] - b_c[u] + logi_c[u] - m_new[u]) for u in un]
        wv = [jnp.concatenate([wk[u] * vf[u], jnp.broadcast_to(wk[u], (ln, DV_M))], axis=1).astype(mm) for u in un]
        upd = [_dot_tn(kb[u], wv[u]) for u in un]
        for u, (bi, h) in enumerate(units):
            decay = jnp.exp(b_c[u][ln - 1:ln, :] + m_st[u] - m_new[u])
            c_ref[bi, h] = decay * st[u] + upd[u]
            m_ref[bi, h] = jnp.broadcast_to(m_new[u], (1, LANES))
        for u, (bi, h) in enumerate(units):
            both = sv[u] + jnp.exp(inter[u] - m_t[u]) * qc[u]
            num, den = both[:, :DV_M], both[:, DV_M:]
            hv = num / jnp.maximum(jnp.abs(den), jnp.exp(-m_t[u]))
            hm = _rms(hv, gh_ref[:, h * DV_M:(h + 1) * DV_M])
            y = hm * _sigmoid(mo_ref[bi, lo:hi, h * DV_M:(h + 1) * DV_M])
            y_ref[bi, lo:hi, h * DV_M:(h + 1) * DV_M] = y.astype(y_ref.dtype)


def _gt_specs(s, ts, nj, gt3d, nb):
    if gt3d:
        return [pl.BlockSpec((1, LANES, s), lambda i, j, bi=bi: (i * nb + bi, 0, 0)) for bi in range(nb)]
    return [pl.BlockSpec((LANES, ts), lambda i, j, bi=bi: (0, (i * nb + bi) * nj + j)) for bi in range(nb)]


def _seqs_per_step(b, ts):
    nb = 2 if ts >= CHUNK else 4
    return nb if b % nb == 0 else 1


def _mlstm(u, g, gt, g_mhead, c0n, m0, b, s, chunk, mm):
    ts = min(256, s)
    nj = s // ts
    gt3d = gt.ndim == 3
    nb = _seqs_per_step(b, ts)
    tok = lambda w: pl.BlockSpec((nb, ts, w), lambda i, j: (i, j, 0))
    st = lambda *shape: pl.BlockSpec((nb,) + shape, lambda i, j: (i,) + (0,) * len(shape))
    seq = lambda a: a.reshape(b, s, a.shape[-1])
    y, c_n, m_new = pl.pallas_call(
        functools.partial(_mlstm_kernel, nb=nb, chunk=chunk, gt3d=gt3d, mm=mm),
        grid=(b // nb, nj),
        in_specs=[tok(NH_M * DQK_M), tok(NH_M * DQK_M), tok(W_M), tok(W_M), tok(LANES)]
                 + _gt_specs(s, ts, nj, gt3d, nb)
                 + [_const_spec((1, W_M)), st(NH_M, DQK_M, 2 * DV_M), st(NH_M, 1, LANES)],
        out_specs=[tok(W_M), st(NH_M, DQK_M, 2 * DV_M), st(NH_M, 1, LANES)],
        out_shape=[jax.ShapeDtypeStruct((b, s, W_M), BF16),
                   jax.ShapeDtypeStruct((b, NH_M, DQK_M, 2 * DV_M), F32),
                   jax.ShapeDtypeStruct((b, NH_M, 1, LANES), F32)],
        compiler_params=_cparams(("arbitrary", "arbitrary")),
        name="mlstm",
    )(seq(u["mq"]), seq(u["mk"]), seq(u["mv"]), seq(u["mo"]), seq(g), *([gt] * nb),
      g_mhead.reshape(1, W_M), c0n, m0)
    return y.reshape(b * s, W_M), c_n, m_new


def _ssd_expansion(ln):
    stride = max(ln, LANES)
    r = lax.broadcasted_iota(jnp.int32, (LANES, 2 * D_INNER + NH_S * stride), 0)
    c = lax.broadcasted_iota(jnp.int32, (LANES, 2 * D_INNER + NH_S * stride), 1)
    src = jnp.where(c < D_INNER, LN_DT + c // P_S,
                    jnp.where(c < 2 * D_INNER, LN_CUM + (c - D_INNER) // P_S, LN_CUM + (c - 2 * D_INNER) // stride))
    return jnp.where(r == src, 1.0, 0.0).astype(BF16)


def _ssd_kernel(xbc_ref, sz_ref, g_ref, *rest, nb, chunk, gt3d, mm):
    gt_refs = rest[:nb]
    ex_ref, cw_ref, cb_ref, dsk_ref, gs_ref, conv0_ref, h0_ref, y_ref, hst_ref, tail_ref = rest[nb:]
    j = pl.program_id(1)

    @pl.when(j == 0)
    def _():
        hst_ref[...] = h0_ref[...]
        tail_ref[:, 0:SUBLANES, :] = conv0_ref[...]

    ts = xbc_ref.shape[1]
    ln = chunk
    rr = lax.broadcasted_iota(jnp.int32, (ln, ln), 0)
    cc = lax.broadcasted_iota(jnp.int32, (ln, ln), 1)
    causal = cc <= rr
    for bi in range(nb):
        _ssd_sequence(xbc_ref.at[bi], sz_ref.at[bi], g_ref.at[bi], gt_refs[bi][0] if gt3d else gt_refs[bi][...],
                      ex_ref, cw_ref, cb_ref, dsk_ref, gs_ref, y_ref.at[bi], hst_ref.at[bi], tail_ref.at[bi],
                      causal, ts, ln, mm)


def _ssd_sequence(xbc_ref, sz_ref, g_ref, gt, ex_ref, cw_ref, cb_ref, dsk_ref, gs_ref, y_ref, hst_ref, tail_ref,
                  causal, ts, ln, mm):
    x = xbc_ref[...]
    tail_ref[SUBLANES:, :] = x
    acc = cb_ref[...] + cw_ref[CONV_W - 1:CONV_W, :] * x
    for k in range(1, CONV_W):
        acc = acc + cw_ref[CONV_W - 1 - k:CONV_W - k, :] * tail_ref[SUBLANES - k:SUBLANES - k + ts, :]
    tail_ref[0:SUBLANES, :] = x[ts - SUBLANES:, :]
    xa = acc * _sigmoid(acc)

    hpg = NH_S // N_GROUPS
    for c in range(ts // ln):
        lo, hi = c * ln, (c + 1) * ln
        g = g_ref[lo:hi, :]
        heads, pairs = range(NH_S), range(NH_S // 2)
        ex_dt = BF16 if ln % (2 * SUBLANES) == 0 else F32
        full = sum(_dot(part.astype(ex_dt), ex_ref[...].astype(ex_dt)) for part in _split3(g))
        dt_full, cum_full = full[:, :D_INNER], full[:, D_INNER:2 * D_INNER]
        stride = max(ln, LANES)
        cum_b = [full[:, 2 * D_INNER + h * stride:2 * D_INNER + h * stride + ln] for h in heads]
        xs = xa[lo:hi, :D_INNER]
        xdt = xs * dt_full
        wkx = jnp.exp(cum_full[ln - 1:ln, :] - cum_full) * xdt
        bmg = [xa[lo:hi, D_INNER + grp * D_STATE:D_INNER + (grp + 1) * D_STATE].astype(mm) for grp in range(N_GROUPS)]
        co = D_INNER + N_GROUPS * D_STATE
        cmg = [xa[lo:hi, co + grp * D_STATE:co + (grp + 1) * D_STATE].astype(mm) for grp in range(N_GROUPS)]
        cbm = [_dot_nt(cmg[grp], bmg[grp]) for grp in range(N_GROUPS)]
        low_l = lax.broadcasted_iota(jnp.int32, (ln, 2 * P_S), 1) < P_S
        low_r = lax.broadcasted_iota(jnp.int32, (2 * P_S, D_STATE), 0) < P_S
        ys = []
        for hp in pairs:
            psl = slice(hp * 2 * P_S, (hp + 1) * 2 * P_S)
            grp = 2 * hp // hpg
            hst = jnp.concatenate([hst_ref[2 * hp], hst_ref[2 * hp + 1]], axis=0)
            ych = _dot_nt(cmg[grp], hst.astype(mm))
            xdt_p = xdt[:, psl].astype(mm)
            ycb = []
            for h in (2 * hp, 2 * hp + 1):
                ldec = jnp.exp(jnp.where(causal, cum_b[h] - gt[LN_CUM + h:LN_CUM + h + 1, lo:hi], -jnp.inf))
                ycb.append(_dot((cbm[grp] * ldec).astype(mm), xdt_p))
            upd = _dot_tn(wkx[:, psl].astype(mm), bmg[grp])
            d0 = jnp.exp(g[ln - 1:ln, LN_CUM + 2 * hp:LN_CUM + 2 * hp + 1])
            d1 = jnp.exp(g[ln - 1:ln, LN_CUM + 2 * hp + 1:LN_CUM + 2 * hp + 2])
            new = jnp.where(low_r, d0, d1) * hst + upd
            hst_ref[2 * hp] = new[:P_S]
            hst_ref[2 * hp + 1] = new[P_S:]
            ys.append(jnp.where(low_l, ycb[0], ycb[1]) + jnp.exp(cum_full[:, psl]) * ych
                      + dsk_ref[:, psl] * xs[:, psl])
        yy = jnp.concatenate(ys, axis=1)
        z = sz_ref[lo:hi, :]
        y_ref[lo:hi, :] = _rms(yy * (z * _sigmoid(z)), gs_ref[...]).astype(y_ref.dtype)


def _ssd(u, g, gt, conv_w, conv_b, d_skip, g_ssm, conv0, h0, b, s, chunk, mm):
    ts = min(256, s)
    nj = s // ts
    gt3d = gt.ndim == 3
    nb = _seqs_per_step(b, ts)
    ex = _ssd_expansion(chunk)
    tok = lambda w: pl.BlockSpec((nb, ts, w), lambda i, j: (i, j, 0))
    st = lambda *shape: pl.BlockSpec((nb,) + shape, lambda i, j: (i,) + (0,) * len(shape))
    seq = lambda a: a.reshape(b, s, a.shape[-1])
    y, h_new = pl.pallas_call(
        functools.partial(_ssd_kernel, nb=nb, chunk=chunk, gt3d=gt3d, mm=mm),
        grid=(b // nb, nj),
        in_specs=[tok(CONV_CH), tok(D_INNER), tok(LANES)] + _gt_specs(s, ts, nj, gt3d, nb)
                 + [_const_spec(ex.shape),
                    _const_spec((CONV_W, CONV_CH)), _const_spec((1, CONV_CH)),
                    _const_spec((1, D_INNER)), _const_spec((1, D_INNER)),
                    st(SUBLANES, CONV_CH), st(NH_S, P_S, D_STATE)],
        out_specs=[tok(D_INNER), st(NH_S, P_S, D_STATE)],
        out_shape=[jax.ShapeDtypeStruct((b, s, D_INNER), BF16),
                   jax.ShapeDtypeStruct((b, NH_S, P_S, D_STATE), F32)],
        scratch_shapes=[pltpu.VMEM((nb, SUBLANES + ts, CONV_CH), F32)],
        compiler_params=_cparams(("arbitrary", "arbitrary")),
        name="ssd",
    )(seq(u["sxbc"]), seq(u["sz"]), seq(g), *([gt] * nb), ex, conv_w, conv_b.reshape(1, CONV_CH),
      jnp.repeat(d_skip, P_S).reshape(1, D_INNER), g_ssm.reshape(1, D_INNER), conv0, h0)
    return y.reshape(b * s, D_INNER), h_new


def _split3(x):
    hi = x.astype(BF16).astype(F32)
    r = x - hi
    mid = r.astype(BF16).astype(F32)
    return hi, mid, r - mid


FOX_EXT = 16
FOX_VROWS = 2 * DH_F + FOX_EXT
FOX_SUB = 256
FOX_WIDE = 4
FOXS_PAGES = 32
FOX_SKIP = 112.0


def _foxp_kernel(q_ref, kt_ref, vt_ref, g_ref, ftr_ref, o_ref, kaug_ref, vaug_ref, stat_ref, *, tq):
    hp = pl.program_id(1)
    i = pl.program_id(2)
    hw = 2 * DH_F
    s_len = kaug_ref.shape[1]

    wide = FOX_WIDE * tq
    lane_row = lax.broadcasted_iota(jnp.int32, (1, LANES), 1)

    @pl.when(i == 0)
    def _():
        kb = kt_ref[0, 0].astype(BF16)
        kaug_ref[0:hw, :] = kb
        vaug_ref[0:hw, :] = vt_ref[0, 0].astype(BF16)
        ksq = kb.astype(F32) * kb.astype(F32)
        r = lax.broadcasted_iota(jnp.int32, (FOX_EXT, s_len), 0)
        ext = jnp.where(r < 3, 1.0, 0.0)
        for hh in range(2):
            ft_row = ftr_ref[pl.ds(hp * 2 + hh, 1), :]
            parts = _split3(ft_row)
            for pi, part in enumerate(parts):
                ext = jnp.where(r == 3 + 3 * hh + pi, -part, ext)
            ftq = jnp.full((1, LANES), -jnp.inf, F32)
            for jb in range(s_len // tq):
                ftq = jnp.where(lane_row == jb, ft_row[:, tq * (jb + 1) - 1:tq * (jb + 1)], ftq)
            stat_ref[hh:hh + 1, :] = ftq
            kn2 = jnp.sum(ksq[hh * DH_F:(hh + 1) * DH_F, :], axis=0, keepdims=True)
            stat_ref[2 + hh:3 + hh, :] = jnp.broadcast_to(jnp.sqrt(jnp.max(kn2, axis=1, keepdims=True)), (1, LANES))
        kaug_ref[hw:hw + FOX_EXT, :] = ext.astype(BF16)
        kaug_ref[hw + FOX_EXT:, :] = jnp.zeros((kaug_ref.shape[0] - hw - FOX_EXT, s_len), BF16)
        vaug_ref[hw:, :] = jnp.where(r == 0, 1.0, 0.0).astype(BF16)

    g = g_ref[...]
    q = q_ref[...]
    lane = lax.broadcasted_iota(jnp.int32, (tq, LANES), 1)
    low = lane < DH_F
    qaug = []
    qsq = q.astype(F32) * q.astype(F32)
    skippable = None
    for hh in range(2):
        mine = low if hh == 0 else ~low
        ft_c = jnp.sum(jnp.where(lane == LN_FT + hp * 2 + hh, g, 0.0), axis=1, keepdims=True)
        hi, mid, lo = _split3(ft_c)
        ext = jnp.where(lane == 0, hi, jnp.where(lane == 1, mid, jnp.where(lane == 2, lo, 0.0)))
        ext = jnp.where((lane >= 3 + 3 * hh) & (lane < 6 + 3 * hh), 1.0, ext)
        qm = jnp.where(mine, q, jnp.zeros_like(q))
        qaug.append(jnp.concatenate([qm, ext.astype(BF16)], axis=1))
        qmax = jnp.sqrt(jnp.max(jnp.sum(jnp.where(mine, qsq, 0.0), axis=1, keepdims=True), axis=0, keepdims=True))
        thr = ft_c[0:1, :] + 2.0 * qmax * stat_ref[2 + hh:3 + hh, 0:1] + FOX_SKIP
        cond = stat_ref[hh:hh + 1, :] > thr
        skippable = cond if skippable is None else skippable & cond
    n_skip = jnp.sum(jnp.where(skippable, 1, 0).astype(jnp.int32))

    rr = lax.broadcasted_iota(jnp.int32, (tq, tq), 0)
    cc = lax.broadcasted_iota(jnp.int32, (tq, tq), 1)
    causal = cc <= rr

    def block(carry, start, width, diag):
        subs = [pl.ds(pl.multiple_of(start + c * FOX_SUB, FOX_SUB), FOX_SUB) for c in range(width // FOX_SUB)]
        ms, ls, acc = carry
        new_m, new_l, alphas, pvs = [], [], [], []
        scores = []
        for hh in range(2):
            ss = []
            for c, sub in enumerate(subs):
                s = _dot(qaug[hh], kaug_ref[:, sub])
                dc = c * FOX_SUB - (width - tq)
                if diag and dc >= 0:
                    s = jnp.where(causal[:, dc:dc + FOX_SUB], s, -jnp.inf)
                ss.append(s)
            scores.append(ss)
        for hh in range(2):
            m_blk = jnp.max(functools.reduce(jnp.maximum, scores[hh]), axis=1, keepdims=True)
            new_m.append(jnp.maximum(ms[hh], m_blk))
            alphas.append(jnp.exp(ms[hh] - new_m[hh]))
        pv = [None, None]
        for c in range(len(subs)):
            for hh in range(2):
                p = jnp.exp(scores[hh][c] - new_m[hh]).astype(BF16)
                d = _dot_nt(p, vaug_ref[:, subs[c]])
                pv[hh] = d if pv[hh] is None else pv[hh] + d
        for hh in range(2):
            new_l.append(alphas[hh] * ls[hh] + pv[hh][:, hw:hw + 1])
            pvs.append(pv[hh][:, :hw])
        acc = jnp.where(low, alphas[0], alphas[1]) * acc + jnp.where(low, pvs[0], pvs[1])
        return tuple(new_m), tuple(new_l), acc

    neg = jnp.full((tq, 1), -jnp.inf, F32)
    zero = jnp.zeros((tq, 1), F32)
    init = ((neg, neg), (zero, zero), jnp.zeros((tq, hw), F32))
    live = i - n_skip
    base = n_skip * tq
    carry = lax.fori_loop(0, live // FOX_WIDE, lambda jb, c: block(c, base + jb * wide, wide, False), init)
    tail_start = base + (live // FOX_WIDE) * wide

    def tail(r):
        return lambda c: block(c, tail_start, (r + 1) * tq, True)

    def pick(lo, hi):
        if hi - lo == 1:
            return tail(lo)
        mid = (lo + hi) // 2
        return lambda c: lax.cond(live % FOX_WIDE < mid, pick(lo, mid), pick(mid, hi), c)

    _, ls, acc = pick(0, FOX_WIDE)(carry)
    o_ref[...] = (acc / jnp.where(low, ls[0], ls[1])).astype(o_ref.dtype)


def _fox_prompt(u, g, gt, b, s, layer):
    tq = min(512, s)
    nq = s // tq
    m = b * s
    hw = 2 * DH_F
    return pl.pallas_call(
        functools.partial(_foxp_kernel, tq=tq),
        grid=(b, NH_F // 2, nq),
        in_specs=[pl.BlockSpec((tq, hw), lambda bb, hp, i: (bb * nq + i, hp)),
                  pl.BlockSpec((1, 1, hw, s), lambda bb, hp, i: (layer, bb, hp, 0)),
                  pl.BlockSpec((1, 1, hw, s), lambda bb, hp, i: (layer, bb, hp, 0)),
                  pl.BlockSpec((tq, LANES), lambda bb, hp, i: (bb * nq + i, 0)),
                  pl.BlockSpec((SUBLANES, s), lambda bb, hp, i: (LN_FT // SUBLANES, bb))],
        out_specs=pl.BlockSpec((tq, hw), lambda bb, hp, i: (bb * nq + i, hp)),
        out_shape=jax.ShapeDtypeStruct((m, W_F), BF16),
        scratch_shapes=[pltpu.VMEM((2 * hw, s), BF16), pltpu.VMEM((FOX_VROWS, s), BF16),
                        pltpu.VMEM((SUBLANES, LANES), F32)],
        compiler_params=_cparams(("arbitrary", "arbitrary", "arbitrary")),
        name="fox_prompt",
    )(u["fq"], u["fk"], u["fv"], g, gt)


def _foxs_kernel(pt_ref, q_ref, kn_ref, vn_ref, gt_ref, *rest, pages):
    del pt_ref
    k_refs, v_refs, lf_refs = rest[:pages], rest[pages:2 * pages], rest[2 * pages:3 * pages]
    o_ref, qbd_ref, m_ref, l_ref, acc_ref, carry_ref, kcat_ref, vcat_ref = rest[3 * pages:]
    gi = pl.program_id(1)
    t = q_ref.shape[0]
    ht = NH_F * t
    page = k_refs[0].shape[2]

    def expand_heads(x):
        return jnp.broadcast_to(x[:, None, :], (NH_F, t, x.shape[1])).reshape(ht, x.shape[1])

    @pl.when(gi == 0)
    def _():
        q = q_ref[...].astype(F32)
        qt = jnp.broadcast_to(q[None], (NH_F, t, W_F)).reshape(ht, W_F)
        rh = lax.broadcasted_iota(jnp.int32, (ht, W_F), 0) // t
        lh = lax.broadcasted_iota(jnp.int32, (ht, W_F), 1) // DH_F
        qbd_ref[...] = jnp.where(rh == lh, qt, 0.0)
        m_ref[...] = jnp.full_like(m_ref, -jnp.inf)
        l_ref[...] = jnp.zeros_like(l_ref)
        acc_ref[...] = jnp.zeros_like(acc_ref)
        carry_ref[...] = jnp.zeros_like(carry_ref)

    cn_rows = expand_heads(gt_ref[0][LN_FT:LN_FT + NH_F, :])
    tq_idx = lax.broadcasted_iota(jnp.int32, (ht, t), 0) % t
    tk_idx = lax.broadcasted_iota(jnp.int32, (ht, t), 1)
    cn_col = jnp.sum(jnp.where(tq_idx == tk_idx, cn_rows, 0.0), axis=1, keepdims=True)

    qbd = qbd_ref[...]
    qbd_b = qbd.astype(BF16)

    def update(s, pv):
        m = m_ref[...]
        m_new = jnp.maximum(m, jnp.max(s, axis=1, keepdims=True))
        p = jnp.exp(s - m_new)
        alpha = jnp.exp(m - m_new)
        l_ref[...] = alpha * l_ref[...] + jnp.sum(p, axis=1, keepdims=True)
        acc_ref[...] = alpha * acc_ref[...] + pv(p)
        m_ref[...] = m_new

    lf_all = jnp.concatenate([lf_refs[pi][0] for pi in range(pages)], axis=0)
    tri = jnp.where(lax.broadcasted_iota(jnp.int32, (page, page), 0)
                    <= lax.broadcasted_iota(jnp.int32, (page, page), 1), 1.0, 0.0).astype(BF16)
    pre_all = sum(_dot(part.astype(BF16), tri) for part in _split3(lf_all))
    carry = carry_ref[...]
    rs = [None] * pages
    for pi in reversed(range(pages)):
        pre = pre_all[pi * NH_F:(pi + 1) * NH_F, :]
        tot = pre[:, page - 1:page]
        rs[pi] = expand_heads(carry + tot - pre)
        carry = carry + tot
        kcat_ref[:, pi * page:(pi + 1) * page] = k_refs[pi][0].astype(BF16)
        vcat_ref[:, pi * page:(pi + 1) * page] = v_refs[pi][0].astype(BF16)
    carry_ref[...] = carry
    s_all = _dot(qbd_b, kcat_ref[...]) + jnp.concatenate(rs, axis=1) + cn_col
    update(s_all, lambda p: _dot_nt(p.astype(BF16), vcat_ref[...]))

    @pl.when(gi == pl.num_programs(1) - 1)
    def _():
        s_new = _dot_nt(qbd, kn_ref[...]) + (cn_col - cn_rows)
        s_new = jnp.where(tk_idx <= tq_idx, s_new, -jnp.inf)
        update(s_new, lambda p: _dot(p, vn_ref[...]))
        o = acc_ref[...] / l_ref[...]
        lh = lax.broadcasted_iota(jnp.int32, (t, W_F), 1) // DH_F
        y = jnp.zeros((t, W_F), F32)
        for h in range(NH_F):
            y = y + jnp.where(lh == h, o[h * t:(h + 1) * t, :], 0.0)
        o_ref[...] = y.astype(o_ref.dtype)


def _fox_sample(u, gt3, page_table, ck_t, cv_t, clf_t, page_base, b, t):
    n_pages = page_table.shape[1]
    page = ck_t.shape[2]
    pages = FOXS_PAGES
    while n_pages % pages:
        pages //= 2
    ng = n_pages // pages
    ht = NH_F * t

    def page_spec(width_shape, pi):
        return pl.BlockSpec((1,) + width_shape,
                            lambda bb, gi, pt: (page_base + pt[bb, (ng - 1 - gi) * pages + pi], 0, 0))

    tok = pl.BlockSpec((t, W_F), lambda bb, gi, pt: (bb, 0))
    in_specs = [tok, tok, tok, pl.BlockSpec((1, LANES, t), lambda bb, gi, pt: (bb, 0, 0))]
    in_specs += [page_spec((W_F, page), pi) for pi in range(pages)]
    in_specs += [page_spec((W_F, page), pi) for pi in range(pages)]
    in_specs += [page_spec((NH_F, page), pi) for pi in range(pages)]
    grid_spec = pltpu.PrefetchScalarGridSpec(
        num_scalar_prefetch=1,
        grid=(b, ng),
        in_specs=in_specs,
        out_specs=pl.BlockSpec((t, W_F), lambda bb, gi, pt: (bb, 0)),
        scratch_shapes=[pltpu.VMEM((ht, W_F), F32), pltpu.VMEM((ht, 1), F32), pltpu.VMEM((ht, 1), F32),
                        pltpu.VMEM((ht, W_F), F32), pltpu.VMEM((NH_F, 1), F32),
                        pltpu.VMEM((W_F, pages * page), BF16), pltpu.VMEM((W_F, pages * page), BF16)],
    )
    return pl.pallas_call(
        functools.partial(_foxs_kernel, pages=pages),
        grid_spec=grid_spec,
        out_shape=jax.ShapeDtypeStruct((b * t, W_F), BF16),
        compiler_params=_cparams(("arbitrary", "arbitrary")),
        name="fox_sample",
    )(page_table, u["fq"], u["fk"], u["fv"], gt3, *([ck_t] * pages), *([cv_t] * pages), *([clf_t] * pages))


def _merge_kernel(x_ref, sh_ref, sc_ref, gt_ref, gpre_ref, gpost_ref, ym_ref, ys_ref, yf_ref,
                  wg_ref, wm_ref, ws_ref, wf_ref, wo_ref, o_ref):
    x = x_ref[...]
    tb, ts, d = x.shape
    h = _rms(x, gpre_ref[...]) * (1.0 + sc_ref[...]) + sh_ref[...]
    hb = h.reshape(tb * ts, d).astype(BF16)
    merged = None
    for bi, (y_ref, w_ref) in enumerate(((ym_ref, wm_ref), (ys_ref, ws_ref), (yf_ref, wf_ref))):
        gate = _sigmoid(_dot(hb, wg_ref[:, bi * d:(bi + 1) * d]))
        term = gate * _dot(y_ref[...], w_ref[...])
        merged = term if merged is None else merged + term
    out = _dot(merged.astype(BF16), wo_ref[...]).reshape(tb, ts, d)
    o_ref[...] = x + gt_ref[...] * _rms(out, gpost_ref[...])


def _merge(x, ada, g_pre, g_post, ym, ys, yf, wg, wm, ws, wf, wo):
    b, s, d = x.shape
    tb, ts = _x_tiles(b, s, 512)
    tm = tb * ts
    nj = s // ts
    row = lambda i, j: (i * nj + j, 0)
    wspec = lambda shape: pl.BlockSpec(shape, lambda i, j: (0, 0), pipeline_mode=pl.Buffered(1))
    return pl.pallas_call(
        _merge_kernel,
        grid=(b // tb, nj),
        in_specs=[pl.BlockSpec((tb, ts, d), lambda i, j: (i, j, 0)),
                  _ada_spec(tb, d, 0), _ada_spec(tb, d, 1), _ada_spec(tb, d, 2),
                  _const_spec((1, d)), _const_spec((1, d)),
                  pl.BlockSpec((tm, W_M), row), pl.BlockSpec((tm, D_INNER), row), pl.BlockSpec((tm, W_F), row),
                  wspec((d, 3 * d)), wspec((W_M, d)), wspec((D_INNER, d)), wspec((W_F, d)), wspec((d, d))],
        out_specs=pl.BlockSpec((tb, ts, d), lambda i, j: (i, j, 0)),
        out_shape=jax.ShapeDtypeStruct((b, s, d), F32),
        compiler_params=_cparams(("arbitrary", "arbitrary")),
        name="merge",
    )(x, ada, ada, ada, g_pre.reshape(1, d), g_post.reshape(1, d), ym, ys, yf, wg, wm, ws, wf, wo)


def _mlp_kernel(x_ref, sh_ref, sc_ref, gt_ref, gpre_ref, gpost_ref, wu_ref, wd_ref, o_ref):
    x = x_ref[...]
    tb, ts, d = x.shape
    h = _rms(x, gpre_ref[...]) * (1.0 + sc_ref[...]) + sh_ref[...]
    hb = h.reshape(tb * ts, d).astype(BF16)
    up = jnp.maximum(_dot(hb, wu_ref[...]), 0.0)
    f = _dot((up * up).astype(BF16), wd_ref[...]).reshape(tb, ts, d)
    o_ref[...] = x + gt_ref[...] * _rms(f, gpost_ref[...])


def _mlp(x, ada, g_pre, g_post, wu, wd):
    b, s, d = x.shape
    tb, ts = _x_tiles(b, s, 512)
    nj = s // ts
    dff = wu.shape[1]
    wspec = lambda shape: pl.BlockSpec(shape, lambda i, j: (0, 0), pipeline_mode=pl.Buffered(1))
    return pl.pallas_call(
        _mlp_kernel,
        grid=(b // tb, nj),
        in_specs=[pl.BlockSpec((tb, ts, d), lambda i, j: (i, j, 0)),
                  _ada_spec(tb, d, 3), _ada_spec(tb, d, 4), _ada_spec(tb, d, 5),
                  _const_spec((1, d)), _const_spec((1, d)),
                  wspec((d, dff)), wspec((dff, d))],
        out_specs=pl.BlockSpec((tb, ts, d), lambda i, j: (i, j, 0)),
        out_shape=jax.ShapeDtypeStruct((b, s, d), F32),
        compiler_params=_cparams(("arbitrary", "arbitrary")),
        name="mlp",
    )(x, ada, ada, ada, g_pre.reshape(1, d), g_post.reshape(1, d), wu, wd)


def _split_w_in(w):
    sizes = (NH_M * DQK_M, NH_M * DQK_M, W_M, NH_M, NH_M, W_M, D_INNER, CONV_CH, NH_S, W_F, W_F, W_F, NH_F)
    names = ("mq", "mk", "mv", "mi", "mf", "mo", "sz", "sxbc", "sdt", "fq", "fk", "fv", "ff")
    cols, off = {}, 0
    for nm, sz in zip(names, sizes):
        cols[nm] = w[:, off:off + sz]
        off += sz
    gates = w[:, off:]
    d = w.shape[0]
    zeros = lambda n: jnp.zeros((d, n), w.dtype)
    small = jnp.concatenate(
        [cols["mi"], cols["mf"], cols["sdt"], cols["ff"], cols["mf"], zeros(LN_CUM - LN_BM - NH_M),
         cols["sdt"], cols["ff"], zeros(LANES - LN_FT - NH_F)], axis=1)
    proj = jnp.concatenate([cols[sg[0]] for sg in SEGS[:-1]] + [small], axis=1)
    kv = jnp.concatenate([cols["fk"], cols["fv"]], axis=1)
    return proj.astype(BF16), kv.astype(BF16), gates.astype(BF16)


def _gate_lanes(b_mgate, dt_bias, b_ffox, a_log):
    z = lambda n: jnp.zeros((n,), F32)
    bias = jnp.concatenate([b_mgate, dt_bias, b_ffox, b_mgate[NH_M:], z(LN_CUM - LN_BM - NH_M),
                            dt_bias, b_ffox, z(LANES - LN_FT - NH_F)])
    alog = jnp.concatenate([z(LN_CUM), a_log, z(LANES - LN_CUM - NH_S)])
    return bias.reshape(1, LANES), alog.reshape(1, LANES)


def _mixer(x, ada, lw, state, attend, chunk, mm, kv_t, layer=0, depth=1, kv_all=None):
    b, s, d = x.shape
    u = _inproj(x, ada, lw["g_pre_mix"], lw["w_proj"], lw["w_kv"].T if kv_t else lw["w_kv"], kv_t,
                layer, depth, kv_all)
    g, gt = _prep(u["small"], lw["bias_lanes"], lw["alog_lanes"], s, chunk)
    if s % LANES:
        gt = gt.reshape(LANES, b, s).transpose(1, 0, 2)
    c0n, m0, conv0, h0 = state
    ym, c_n, m_new = _mlstm(u, g, gt, lw["g_mhead"], c0n, m0, b, s, chunk, mm)
    ys, h_new = _ssd(u, g, gt, lw["conv_w"], lw["conv_b"], lw["d_skip"], lw["g_ssm"], conv0, h0, b, s, chunk, mm)
    yf = attend(u, g, gt)
    x1 = _merge(x, ada, lw["g_pre_mix"], lw["g_post_mix"], ym, ys, yf,
                lw["w_gates"], lw["w_br_m"], lw["w_br_s"], lw["w_br_f"], lw["w_out"])
    x2 = _mlp(x1, ada, lw["g_pre_mlp"], lw["g_post_mlp"], lw["w_up"], lw["w_down"])
    if kv_t:
        rows = lambda a: a
    else:
        rows = lambda a: a.reshape(b, s, NH_F, DH_F)
    outs = (rows(u["fk"]), rows(u["fv"]),
            g[:, LN_FF:LN_FF + NH_F].reshape(b, s, NH_F),
            jnp.swapaxes(c_n[..., :DV_M], -1, -2), c_n[..., DV_M], m_new[:, :, 0, 0],
            u["sxbc"].reshape(b, s, CONV_CH)[:, s - (CONV_W - 1):, :], h_new)
    return x2, outs


def kernel(x_prompt, x_sample, cache_k, cache_v, cache_logf, state_mlstm_C, state_mlstm_n, state_mlstm_m,
           state_conv, state_ssm, page_table, c_prompt, c_sample, w_ada, b_ada, g_pre_mix, g_post_mix, w_in,
           b_mgate, b_ffox, g_mhead, conv_w, conv_b, dt_bias, a_log, d_skip, g_ssm, w_br_m, w_br_s, w_br_f,
           w_out, g_pre_mlp, g_post_mlp, w_up, w_down):
    depth = w_in.shape[0]
    bp, sp, d = x_prompt.shape
    bs, ss, _ = x_sample.shape
    n_phys, page = cache_k.shape[1], cache_k.shape[2]

    pad = (-(bp + bs)) % SUBLANES
    c_all = jnp.concatenate([c_prompt, c_sample, jnp.zeros((pad, d), F32)], axis=0)
    ada_all = _ada(c_all, w_ada, b_ada)

    ck_t = jnp.transpose(cache_k, (0, 1, 3, 4, 2)).reshape(depth * n_phys, W_F, page)
    cv_t = jnp.transpose(cache_v, (0, 1, 3, 4, 2)).reshape(depth * n_phys, W_F, page)
    clf_t = jnp.transpose(cache_logf, (0, 1, 3, 2)).reshape(depth * n_phys, NH_F, page)

    chunk_p = SCAN_CHUNK if sp % SCAN_CHUNK == 0 else (CHUNK if sp % CHUNK == 0 else sp)
    chunk_s = SCAN_CHUNK if ss % SCAN_CHUNK == 0 else (CHUNK if ss % CHUNK == 0 else ss)
    yp, ys = x_prompt, x_sample
    res_p, res_s = [], []
    kv_all = None
    for l in range(depth):
        w_proj, w_kv, w_gates = _split_w_in(w_in[l])
        bias_lanes, alog_lanes = _gate_lanes(b_mgate[l], dt_bias[l], b_ffox[l], a_log[l])
        lw = dict(w_proj=w_proj, w_kv=w_kv, w_gates=w_gates, bias_lanes=bias_lanes, alog_lanes=alog_lanes,
                  g_pre_mix=g_pre_mix[l], g_post_mix=g_post_mix[l], g_mhead=g_mhead[l],
                  conv_w=conv_w[l], conv_b=conv_b[l], d_skip=d_skip[l], g_ssm=g_ssm[l],
                  w_br_m=w_br_m[l].astype(BF16), w_br_s=w_br_s[l].astype(BF16), w_br_f=w_br_f[l].astype(BF16),
                  w_out=w_out[l].astype(BF16), g_pre_mlp=g_pre_mlp[l], g_post_mlp=g_post_mlp[l],
                  w_up=w_up[l].astype(BF16), w_down=w_down[l].astype(BF16))
        ada_p = ada_all[l, :bp][:, None, :]
        ada_s = ada_all[l, bp:bp + bs][:, None, :]

        zero_state = (jnp.zeros((bp, NH_M, DQK_M, 2 * DV_M), F32),
                      jnp.zeros((bp, NH_M, 1, LANES), F32), jnp.zeros((bp, SUBLANES, CONV_CH), F32),
                      jnp.zeros((bp, NH_S, P_S, D_STATE), F32))
        yp, outs = _mixer(yp, ada_p, lw, zero_state,
                          lambda u, g, gt, l=l: _fox_prompt(u, g, gt, bp, sp, l), chunk_p, BF16, True,
                          l, depth, kv_all)
        kv_all = outs[:2]
        res_p.append(outs)

        state = (jnp.concatenate([jnp.swapaxes(state_mlstm_C[l], -1, -2),
                                  jnp.broadcast_to(state_mlstm_n[l][..., None], (bs, NH_M, DQK_M, DV_M))], axis=-1),
                 jnp.broadcast_to(state_mlstm_m[l][:, :, None, None], (bs, NH_M, 1, LANES)),
                 jnp.pad(state_conv[l], ((0, 0), (SUBLANES - (CONV_W - 1), 0), (0, 0))),
                 state_ssm[l])
        ys, outs = _mixer(ys, ada_s, lw, state,
                          lambda u, g, gt, l=l: _fox_sample(u, gt, page_table, ck_t, cv_t, clf_t, l * n_phys, bs, ss),
                          chunk_s, F32, False)
        res_s.append(outs)

    stack = lambda res, i: jnp.stack([r[i] for r in res])
    kv_prompt = tuple(jnp.transpose(a.reshape(depth, bp, NH_F, DH_F, sp), (0, 1, 4, 2, 3)) for a in kv_all)
    return ((yp, ys) + kv_prompt + tuple(stack(res_p, i) for i in range(2, 8))
            + tuple(stack(res_s, i) for i in range(8)))
```

```python
import functools

import jax
import jax.numpy as jnp
from jax import lax
from jax.experimental import pallas as pl
from jax.experimental.pallas import tpu as pltpu

F32 = jnp.float32
BF16 = jnp.bfloat16

NH_M, DQK_M, DV_M = 4, 64, 128
W_M = NH_M * DV_M
GATE_CAP = 15.0
NH_S, P_S, N_GROUPS, D_STATE, CONV_W = 8, 64, 2, 128, 4
D_INNER = NH_S * P_S
CONV_CH = D_INNER + 2 * N_GROUPS * D_STATE
NH_F, DH_F = 8, 64
W_F = NH_F * DH_F
CHUNK = 64
SCAN_CHUNK = 256
EPS = 1e-6
LANES = 128
SUBLANES = 8

LN_LOGI, LN_LOGF, LN_DT, LN_FF, LN_BM, LN_CUM, LN_FT = 0, 4, 8, 16, 24, 32, 40

VMEM_LIMIT = 56 * 1024 * 1024


def _cparams(sem):
    return pltpu.CompilerParams(dimension_semantics=sem, vmem_limit_bytes=VMEM_LIMIT)


def _sigmoid(x):
    return 1.0 / (1.0 + jnp.exp(-x))


def _softplus(x):
    return jnp.maximum(x, 0.0) + jnp.log(1.0 + jnp.exp(-jnp.abs(x)))


def _rms(x, g):
    return x * lax.rsqrt(jnp.mean(x * x, axis=-1, keepdims=True) + EPS) * g


def _dot(a, b):
    return jnp.dot(a, b, preferred_element_type=F32)


def _dot_nt(a, b):
    return lax.dot_general(a, b, (((1,), (1,)), ((), ())), preferred_element_type=F32)


def _dot_tn(a, b):
    return lax.dot_general(a, b, (((0,), (0,)), ((), ())), preferred_element_type=F32)


def _const_spec(shape):
    nd = len(shape)
    return pl.BlockSpec(shape, lambda *_: (0,) * nd)


def _ada_kernel(c_ref, w_ref, b_ref, o_ref):
    c = c_ref[...]
    s = (c * _sigmoid(c)).astype(BF16)
    o_ref[0] = _dot(s, w_ref[0].astype(BF16)) + b_ref[0]


def _ada(c_all, w_ada, b_ada):
    depth, d, n = w_ada.shape
    rows = c_all.shape[0]
    tn = 512
    return pl.pallas_call(
        _ada_kernel,
        grid=(depth, n // tn),
        in_specs=[pl.BlockSpec((rows, d), lambda l, j: (0, 0)),
                  pl.BlockSpec((1, d, tn), lambda l, j: (l, 0, j)),
                  pl.BlockSpec((1, 1, tn), lambda l, j: (l, 0, j))],
        out_specs=pl.BlockSpec((1, rows, tn), lambda l, j: (l, 0, j)),
        out_shape=jax.ShapeDtypeStruct((depth, rows, n), F32),
        compiler_params=_cparams(("arbitrary", "arbitrary")),
        name="ada",
    )(c_all, w_ada, b_ada.reshape(depth, 1, n))


SEGS = (("mq", NH_M * DQK_M, DQK_M ** -0.5, F32), ("mk", NH_M * DQK_M, 1.0, F32), ("mv", W_M, 1.0, F32),
        ("mo", W_M, 1.0, F32), ("sz", D_INNER, 1.0, F32), ("sxbc", CONV_CH, 1.0, F32),
        ("fq", W_F, DH_F ** -0.5, BF16), ("small", LANES, 1.0, F32))
W_PROJ = sum(sg[1] for sg in SEGS)


def _inproj_kernel(x_ref, sh_ref, sc_ref, g_ref, w_ref, wkv_ref, *rest, kv_t, n_alias, layer):
    outs = rest[n_alias:]
    x = x_ref[...]
    tb, ts, d = x.shape
    h = _rms(x, g_ref[...]) * (1.0 + sc_ref[...]) + sh_ref[...]
    hb = h.reshape(tb * ts, d).astype(BF16)
    off = 0
    for (_, wd, scale, _), o in zip(SEGS, outs):
        r = _dot(hb, w_ref[:, off:off + wd])
        if scale != 1.0:
            r = r * scale
        o[...] = r.astype(o.dtype)
        off += wd
    small_ref, smallt_ref, fk_ref, fv_ref = outs[len(SEGS) - 1:]
    smallt_ref[...] = small_ref[...].T
    if kv_t:
        kv = _dot_nt(wkv_ref[...], hb)
        own = layer if fk_ref.shape[0] > 1 else 0
        for o_ref, val in ((fk_ref, kv[:W_F]), (fv_ref, kv[W_F:])):
            for dl in range(o_ref.shape[0]):
                o_ref[dl, 0] = val if dl == own else jnp.zeros_like(val)
    else:
        kv = _dot(hb, wkv_ref[...])
        fk_ref[...] = kv[:, :W_F]
        fv_ref[...] = kv[:, W_F:]


def _x_tiles(b, s, ts_max):
    if s >= ts_max:
        return 1, ts_max
    tb = max(1, min(b, ts_max // s))
    return tb, s


def _ada_spec(tb, d, col):
    return pl.BlockSpec((tb, 1, d), lambda i, j: (i, 0, col))


def _inproj(x, ada, g_pre, w, wkv, kv_t, layer=0, depth=1, kv_all=None):
    b, s, d = x.shape
    tb, ts = _x_tiles(b, s, 512)
    tm = tb * ts
    nj = s // ts
    m = b * s
    row = lambda i, j: (i * nj + j, 0)
    out_shape = [jax.ShapeDtypeStruct((m, wd), dt) for _, wd, _, dt in SEGS]
    out_specs = [pl.BlockSpec((tm, wd), row) for _, wd, _, _ in SEGS]
    out_shape.append(jax.ShapeDtypeStruct((LANES, m), F32))
    out_specs.append(pl.BlockSpec((LANES, tm), lambda i, j: (0, i * nj + j)))
    in_specs = [pl.BlockSpec((tb, ts, d), lambda i, j: (i, j, 0)),
                _ada_spec(tb, d, 0), _ada_spec(tb, d, 1),
                _const_spec((1, d)),
                pl.BlockSpec((d, W_PROJ), lambda i, j: (0, 0), pipeline_mode=pl.Buffered(1)),
                pl.BlockSpec(wkv.shape, lambda i, j: (0, 0), pipeline_mode=pl.Buffered(1))]
    args = [x, ada, ada, g_pre.reshape(1, d), w, wkv]
    aliases = {}
    if kv_t:
        assert tb == 1
        out_shape += [jax.ShapeDtypeStruct((depth, b, W_F, s), F32)] * 2
        if kv_all is None:
            out_specs += [pl.BlockSpec((depth, 1, W_F, ts), lambda i, j: (0, i, 0, j))] * 2
        else:
            out_specs += [pl.BlockSpec((1, 1, W_F, ts), lambda i, j: (layer, i, 0, j))] * 2
        if kv_all is not None:
            aliases = {len(args): len(SEGS) + 1, len(args) + 1: len(SEGS) + 2}
            in_specs += [pl.BlockSpec(memory_space=pl.ANY)] * 2
            args += list(kv_all)
    else:
        out_shape += [jax.ShapeDtypeStruct((m, W_F), F32)] * 2
        out_specs += [pl.BlockSpec((tm, W_F), row)] * 2
    outs = pl.pallas_call(
        functools.partial(_inproj_kernel, kv_t=kv_t, n_alias=len(aliases), layer=layer),
        grid=(b // tb, nj),
        in_specs=in_specs,
        out_specs=out_specs,
        out_shape=out_shape,
        input_output_aliases=aliases,
        compiler_params=_cparams(("arbitrary", "arbitrary")),
        name="inproj",
    )(*args)
    names = [sg[0] for sg in SEGS] + ["small_t", "fk", "fv"]
    return dict(zip(names, outs))


def _prep_kernel(small_ref, bias_ref, alog_ref, g_ref, gt_ref, carry_ref, *, chunk, seg, tiles_per_seq):
    i = pl.program_id(0)

    @pl.when(i % tiles_per_seq == 0)
    def _():
        carry_ref[...] = jnp.zeros_like(carry_ref)

    v = small_ref[...] + bias_ref[...]
    ts = v.shape[0]
    lane = lax.broadcasted_iota(jnp.int32, v.shape, 1)
    row = lax.broadcasted_iota(jnp.int32, v.shape, 0)
    capped = GATE_CAP * jnp.tanh(v * (1.0 / GATE_CAP))
    is_m = (lane < LN_DT) | ((lane >= LN_BM) & (lane < LN_CUM))
    vv = jnp.where(is_m, capped, v)
    sp = _softplus(vv)
    lsig = vv - sp
    a = -jnp.exp(alog_ref[...])
    val = jnp.where(lane < LN_LOGF, vv,
          jnp.where(lane < LN_DT, lsig,
          jnp.where(lane < LN_FF, sp,
          jnp.where(lane < LN_CUM, lsig,
          jnp.where(lane < LN_FT, sp * a, lsig)))))
    val = jnp.where(lane < LN_FT + NH_F, val, 0.0)
    ridx = jnp.where(lane < LN_FT, row % chunk, row % seg)
    ridx = jnp.where(lane >= LN_BM, ridx, -1)
    x = val
    k = 1
    while k < min(ts, max(chunk, seg)):
        x = x + jnp.where(ridx >= k, pltpu.roll(x, k, 0), 0.0)
        k *= 2
    x = x + jnp.where(lane >= LN_FT, carry_ref[...], 0.0)
    carry_ref[...] = x[ts - 1:ts, :]
    g_ref[...] = x
    gt_ref[...] = x.T


def _prep(small, bias_lanes, alog_lanes, s, chunk):
    m = small.shape[0]
    ts = min(512, m)
    if s >= ts:
        seg, tiles_per_seq = ts, s // ts
    else:
        seg, tiles_per_seq = s, 1
    return pl.pallas_call(
        functools.partial(_prep_kernel, chunk=chunk, seg=seg, tiles_per_seq=tiles_per_seq),
        grid=(m // ts,),
        in_specs=[pl.BlockSpec((ts, LANES), lambda i: (i, 0)), _const_spec((1, LANES)), _const_spec((1, LANES))],
        out_specs=[pl.BlockSpec((ts, LANES), lambda i: (i, 0)), pl.BlockSpec((LANES, ts), lambda i: (0, i))],
        out_shape=[jax.ShapeDtypeStruct((m, LANES), F32), jax.ShapeDtypeStruct((LANES, m), F32)],
        scratch_shapes=[pltpu.VMEM((1, LANES), F32)],
        compiler_params=_cparams(("arbitrary",)),
        name="prep",
    )(small, bias_lanes, alog_lanes)


def _mlstm_kernel(q_ref, k_ref, v_ref, mo_ref, g_ref, *rest, nb, chunk, gt3d, mm):
    gt_refs = rest[:nb]
    gh_ref, c0_ref, m0_ref, y_ref, c_ref, m_ref = rest[nb:]
    j = pl.program_id(1)

    @pl.when(j == 0)
    def _():
        c_ref[...] = c0_ref[...]
        m_ref[...] = m0_ref[...]

    ts = q_ref.shape[1]
    ln = chunk
    gts = [r[0] if gt3d else r[...] for r in gt_refs]
    rr = lax.broadcasted_iota(jnp.int32, (ln, ln), 0)
    cc = lax.broadcasted_iota(jnp.int32, (ln, ln), 1)
    causal = cc <= rr
    ones = jnp.ones((ln, DV_M), F32)
    units = [(bi, h) for bi in range(nb) for h in range(NH_M)]
    for c in range(ts // ln):
        lo, hi = c * ln, (c + 1) * ln
        g = [g_ref[bi, lo:hi, :] for bi in range(nb)]
        qb = [q_ref[bi, lo:hi, h * DQK_M:(h + 1) * DQK_M].astype(mm) for bi, h in units]
        kb = [k_ref[bi, lo:hi, h * DQK_M:(h + 1) * DQK_M].astype(mm) for bi, h in units]
        vf = [v_ref[bi, lo:hi, h * DV_M:(h + 1) * DV_M] for bi, h in units]
        b_c = [g[bi][:, LN_BM + h:LN_BM + h + 1] for bi, h in units]
        logi_c = [g[bi][:, LN_LOGI + h:LN_LOGI + h + 1] for bi, h in units]
        m_st = [m_ref[bi, h][:, 0:1] for bi, h in units]
        st = [c_ref[bi, h] for bi, h in units]
        un = range(len(units))
        qk = [_dot_nt(qb[u], kb[u]) for u in un]
        qc = [_dot(qb[u], st[u].astype(mm)) for u in un]
        dm = [jnp.where(causal, b_c[u] - gts[bi][LN_BM + h:LN_BM + h + 1, lo:hi]
                        + gts[bi][LN_LOGI + h:LN_LOGI + h + 1, lo:hi], -jnp.inf) for u, (bi, h) in enumerate(units)]
        inter = [b_c[u] + m_st[u] for u in un]
        m_t = [jnp.maximum(inter[u], jnp.max(dm[u], axis=1, keepdims=True)) for u in un]
        s = [(qk[u] * jnp.exp(dm[u] - m_t[u])).astype(mm) for u in un]
        sv = [_dot(s[u], jnp.concatenate([vf[u], ones], axis=1).astype(mm)) for u in un]
        m_new = [m_t[u][ln - 1:ln, :] for u in un]
        wk = [jnp.exp(b_c[u][ln - 1:ln, :] - b_c[u] + logi_c[u] - m_new[u]) for u in un]
        wv = [jnp.concatenate([wk[u] * vf[u], jnp.broadcast_to(wk[u], (ln, DV_M))], axis=1).astype(mm) for u in un]
        upd = [_dot_tn(kb[u], wv[u]) for u in un]
        for u, (bi, h) in enumerate(units):
            decay = jnp.exp(b_c[u][ln - 1:ln, :] + m_st[u] - m_new[u])
            c_ref[bi, h] = decay * st[u] + upd[u]
            m_ref[bi, h] = jnp.broadcast_to(m_new[u], (1, LANES))
        for u, (bi, h) in enumerate(units):
            both = sv[u] + jnp.exp(inter[u] - m_t[u]) * qc[u]
            num, den = both[:, :DV_M], both[:, DV_M:]
            hv = num / jnp.maximum(jnp.abs(den), jnp.exp(-m_t[u]))
            hm = _rms(hv, gh_ref[:, h * DV_M:(h + 1) * DV_M])
            y = hm * _sigmoid(mo_ref[bi, lo:hi, h * DV_M:(h + 1) * DV_M])
            y_ref[bi, lo:hi, h * DV_M:(h + 1) * DV_M] = y.astype(y_ref.dtype)


def _gt_specs(s, ts, nj, gt3d, nb):
    if gt3d:
        return [pl.BlockSpec((1, LANES, s), lambda i, j, bi=bi: (i * nb + bi, 0, 0)) for bi in range(nb)]
    return [pl.BlockSpec((LANES, ts), lambda i, j, bi=bi: (0, (i * nb + bi) * nj + j)) for bi in range(nb)]


def _seqs_per_step(b, ts, long_tiles=2):
    nb = long_tiles if ts >= CHUNK else 4
    return nb if b % nb == 0 else 1


def _mlstm(u, g, gt, g_mhead, c0n, m0, b, s, chunk, mm):
    ts = min(256, s)
    nj = s // ts
    gt3d = gt.ndim == 3
    nb = _seqs_per_step(b, ts, long_tiles=4)
    tok = lambda w: pl.BlockSpec((nb, ts, w), lambda i, j: (i, j, 0))
    st = lambda *shape: pl.BlockSpec((nb,) + shape, lambda i, j: (i,) + (0,) * len(shape))
    seq = lambda a: a.reshape(b, s, a.shape[-1])
    y, c_n, m_new = pl.pallas_call(
        functools.partial(_mlstm_kernel, nb=nb, chunk=chunk, gt3d=gt3d, mm=mm),
        grid=(b // nb, nj),
        in_specs=[tok(NH_M * DQK_M), tok(NH_M * DQK_M), tok(W_M), tok(W_M), tok(LANES)]
                 + _gt_specs(s, ts, nj, gt3d, nb)
                 + [_const_spec((1, W_M)), st(NH_M, DQK_M, 2 * DV_M), st(NH_M, 1, LANES)],
        out_specs=[tok(W_M), st(NH_M, DQK_M, 2 * DV_M), st(NH_M, 1, LANES)],
        out_shape=[jax.ShapeDtypeStruct((b, s, W_M), BF16),
                   jax.ShapeDtypeStruct((b, NH_M, DQK_M, 2 * DV_M), F32),
                   jax.ShapeDtypeStruct((b, NH_M, 1, LANES), F32)],
        compiler_params=_cparams(("arbitrary", "arbitrary")),
        name="mlstm",
    )(seq(u["mq"]), seq(u["mk"]), seq(u["mv"]), seq(u["mo"]), seq(g), *([gt] * nb),
      g_mhead.reshape(1, W_M), c0n, m0)
    return y.reshape(b * s, W_M), c_n, m_new


def _ssd_expansion(ln):
    stride = max(ln, LANES)
    r = lax.broadcasted_iota(jnp.int32, (LANES, 2 * D_INNER + NH_S * stride), 0)
    c = lax.broadcasted_iota(jnp.int32, (LANES, 2 * D_INNER + NH_S * stride), 1)
    src = jnp.where(c < D_INNER, LN_DT + c // P_S,
                    jnp.where(c < 2 * D_INNER, LN_CUM + (c - D_INNER) // P_S, LN_CUM + (c - 2 * D_INNER) // stride))
    return jnp.where(r == src, 1.0, 0.0).astype(BF16)


def _ssd_kernel(xbc_ref, sz_ref, g_ref, *rest, nb, chunk, gt3d, mm):
    gt_refs = rest[:nb]
    ex_ref, cw_ref, cb_ref, dsk_ref, gs_ref, conv0_ref, h0_ref, y_ref, hst_ref, tail_ref = rest[nb:]
    j = pl.program_id(1)

    @pl.when(j == 0)
    def _():
        hst_ref[...] = h0_ref[...]
        tail_ref[:, 0:SUBLANES, :] = conv0_ref[...]

    ts = xbc_ref.shape[1]
    ln = chunk
    rr = lax.broadcasted_iota(jnp.int32, (ln, ln), 0)
    cc = lax.broadcasted_iota(jnp.int32, (ln, ln), 1)
    causal = cc <= rr
    for bi in range(nb):
        _ssd_sequence(xbc_ref.at[bi], sz_ref.at[bi], g_ref.at[bi], gt_refs[bi][0] if gt3d else gt_refs[bi][...],
                      ex_ref, cw_ref, cb_ref, dsk_ref, gs_ref, y_ref.at[bi], hst_ref.at[bi], tail_ref.at[bi],
                      causal, ts, ln, mm)


def _ssd_sequence(xbc_ref, sz_ref, g_ref, gt, ex_ref, cw_ref, cb_ref, dsk_ref, gs_ref, y_ref, hst_ref, tail_ref,
                  causal, ts, ln, mm):
    x = xbc_ref[...]
    tail_ref[SUBLANES:, :] = x
    acc = cb_ref[...] + cw_ref[CONV_W - 1:CONV_W, :] * x
    for k in range(1, CONV_W):
        acc = acc + cw_ref[CONV_W - 1 - k:CONV_W - k, :] * tail_ref[SUBLANES - k:SUBLANES - k + ts, :]
    tail_ref[0:SUBLANES, :] = x[ts - SUBLANES:, :]
    xa = acc * _sigmoid(acc)

    hpg = NH_S // N_GROUPS
    for c in range(ts // ln):
        lo, hi = c * ln, (c + 1) * ln
        g = g_ref[lo:hi, :]
        heads, pairs = range(NH_S), range(NH_S // 2)
        ex_dt = BF16 if ln % (2 * SUBLANES) == 0 else F32
        full = sum(_dot(part.astype(ex_dt), ex_ref[...].astype(ex_dt)) for part in _split3(g))
        dt_full, cum_full = full[:, :D_INNER], full[:, D_INNER:2 * D_INNER]
        stride = max(ln, LANES)
        cum_b = [full[:, 2 * D_INNER + h * stride:2 * D_INNER + h * stride + ln] for h in heads]
        xs = xa[lo:hi, :D_INNER]
        xdt = xs * dt_full
        wkx = jnp.exp(cum_full[ln - 1:ln, :] - cum_full) * xdt
        bmg = [xa[lo:hi, D_INNER + grp * D_STATE:D_INNER + (grp + 1) * D_STATE].astype(mm) for grp in range(N_GROUPS)]
        co = D_INNER + N_GROUPS * D_STATE
        cmg = [xa[lo:hi, co + grp * D_STATE:co + (grp + 1) * D_STATE].astype(mm) for grp in range(N_GROUPS)]
        cbm = [_dot_nt(cmg[grp], bmg[grp]) for grp in range(N_GROUPS)]
        low_l = lax.broadcasted_iota(jnp.int32, (ln, 2 * P_S), 1) < P_S
        low_r = lax.broadcasted_iota(jnp.int32, (2 * P_S, D_STATE), 0) < P_S
        ys = []
        for hp in pairs:
            psl = slice(hp * 2 * P_S, (hp + 1) * 2 * P_S)
            grp = 2 * hp // hpg
            hst = jnp.concatenate([hst_ref[2 * hp], hst_ref[2 * hp + 1]], axis=0)
            ych = _dot_nt(cmg[grp], hst.astype(mm))
            xdt_p = xdt[:, psl].astype(mm)
            ycb = []
            for h in (2 * hp, 2 * hp + 1):
                ldec = jnp.exp(jnp.where(causal, cum_b[h] - gt[LN_CUM + h:LN_CUM + h + 1, lo:hi], -jnp.inf))
                ycb.append(_dot((cbm[grp] * ldec).astype(mm), xdt_p))
            upd = _dot_tn(wkx[:, psl].astype(mm), bmg[grp])
            d0 = jnp.exp(g[ln - 1:ln, LN_CUM + 2 * hp:LN_CUM + 2 * hp + 1])
            d1 = jnp.exp(g[ln - 1:ln, LN_CUM + 2 * hp + 1:LN_CUM + 2 * hp + 2])
            new = jnp.where(low_r, d0, d1) * hst + upd
            hst_ref[2 * hp] = new[:P_S]
            hst_ref[2 * hp + 1] = new[P_S:]
            ys.append(jnp.where(low_l, ycb[0], ycb[1]) + jnp.exp(cum_full[:, psl]) * ych
                      + dsk_ref[:, psl] * xs[:, psl])
        yy = jnp.concatenate(ys, axis=1)
        z = sz_ref[lo:hi, :]
        y_ref[lo:hi, :] = _rms(yy * (z * _sigmoid(z)), gs_ref[...]).astype(y_ref.dtype)


def _ssd(u, g, gt, conv_w, conv_b, d_skip, g_ssm, conv0, h0, b, s, chunk, mm):
    ts = min(256, s)
    nj = s // ts
    gt3d = gt.ndim == 3
    nb = _seqs_per_step(b, ts)
    ex = _ssd_expansion(chunk)
    tok = lambda w: pl.BlockSpec((nb, ts, w), lambda i, j: (i, j, 0))
    st = lambda *shape: pl.BlockSpec((nb,) + shape, lambda i, j: (i,) + (0,) * len(shape))
    seq = lambda a: a.reshape(b, s, a.shape[-1])
    y, h_new = pl.pallas_call(
        functools.partial(_ssd_kernel, nb=nb, chunk=chunk, gt3d=gt3d, mm=mm),
        grid=(b // nb, nj),
        in_specs=[tok(CONV_CH), tok(D_INNER), tok(LANES)] + _gt_specs(s, ts, nj, gt3d, nb)
                 + [_const_spec(ex.shape),
                    _const_spec((CONV_W, CONV_CH)), _const_spec((1, CONV_CH)),
                    _const_spec((1, D_INNER)), _const_spec((1, D_INNER)),
                    st(SUBLANES, CONV_CH), st(NH_S, P_S, D_STATE)],
        out_specs=[tok(D_INNER), st(NH_S, P_S, D_STATE)],
        out_shape=[jax.ShapeDtypeStruct((b, s, D_INNER), BF16),
                   jax.ShapeDtypeStruct((b, NH_S, P_S, D_STATE), F32)],
        scratch_shapes=[pltpu.VMEM((nb, SUBLANES + ts, CONV_CH), F32)],
        compiler_params=_cparams(("arbitrary", "arbitrary")),
        name="ssd",
    )(seq(u["sxbc"]), seq(u["sz"]), seq(g), *([gt] * nb), ex, conv_w, conv_b.reshape(1, CONV_CH),
      jnp.repeat(d_skip, P_S).reshape(1, D_INNER), g_ssm.reshape(1, D_INNER), conv0, h0)
    return y.reshape(b * s, D_INNER), h_new


def _split3(x):
    hi = x.astype(BF16).astype(F32)
    r = x - hi
    mid = r.astype(BF16).astype(F32)
    return hi, mid, r - mid


FOX_EXT = 16
FOX_VROWS = 2 * DH_F + FOX_EXT
FOX_SUB = 256
FOX_WIDE = 4
FOXS_PAGES = 32
FOX_SKIP = 112.0


def _foxp_kernel(q_ref, kt_ref, vt_ref, g_ref, ftr_ref, o_ref, kaug_ref, vaug_ref, stat_ref, *, tq):
    hp = pl.program_id(1)
    i = pl.program_id(2)
    hw = 2 * DH_F
    s_len = kaug_ref.shape[1]

    wide = FOX_WIDE * tq
    lane_row = lax.broadcasted_iota(jnp.int32, (1, LANES), 1)

    @pl.when(i == 0)
    def _():
        kb = kt_ref[0, 0].astype(BF16)
        kaug_ref[0:hw, :] = kb
        vaug_ref[0:hw, :] = vt_ref[0, 0].astype(BF16)
        ksq = kb.astype(F32) * kb.astype(F32)
        r = lax.broadcasted_iota(jnp.int32, (FOX_EXT, s_len), 0)
        ext = jnp.where(r < 3, 1.0, 0.0)
        for hh in range(2):
            ft_row = ftr_ref[pl.ds(hp * 2 + hh, 1), :]
            parts = _split3(ft_row)
            for pi, part in enumerate(parts):
                ext = jnp.where(r == 3 + 3 * hh + pi, -part, ext)
            ftq = jnp.full((1, LANES), -jnp.inf, F32)
            for jb in range(s_len // tq):
                ftq = jnp.where(lane_row == jb, ft_row[:, tq * (jb + 1) - 1:tq * (jb + 1)], ftq)
            stat_ref[hh:hh + 1, :] = ftq
            kn2 = jnp.sum(ksq[hh * DH_F:(hh + 1) * DH_F, :], axis=0, keepdims=True)
            stat_ref[2 + hh:3 + hh, :] = jnp.broadcast_to(jnp.sqrt(jnp.max(kn2, axis=1, keepdims=True)), (1, LANES))
        kaug_ref[hw:hw + FOX_EXT, :] = ext.astype(BF16)
        kaug_ref[hw + FOX_EXT:, :] = jnp.zeros((kaug_ref.shape[0] - hw - FOX_EXT, s_len), BF16)
        vaug_ref[hw:, :] = jnp.where(r == 0, 1.0, 0.0).astype(BF16)

    g = g_ref[...]
    q = q_ref[...]
    lane = lax.broadcasted_iota(jnp.int32, (tq, LANES), 1)
    low = lane < DH_F
    qaug = []
    qsq = q.astype(F32) * q.astype(F32)
    skippable = None
    for hh in range(2):
        mine = low if hh == 0 else ~low
        ft_c = jnp.sum(jnp.where(lane == LN_FT + hp * 2 + hh, g, 0.0), axis=1, keepdims=True)
        hi, mid, lo = _split3(ft_c)
        ext = jnp.where(lane == 0, hi, jnp.where(lane == 1, mid, jnp.where(lane == 2, lo, 0.0)))
        ext = jnp.where((lane >= 3 + 3 * hh) & (lane < 6 + 3 * hh), 1.0, ext)
        qm = jnp.where(mine, q, jnp.zeros_like(q))
        qaug.append(jnp.concatenate([qm, ext.astype(BF16)], axis=1))
        qmax = jnp.sqrt(jnp.max(jnp.sum(jnp.where(mine, qsq, 0.0), axis=1, keepdims=True), axis=0, keepdims=True))
        thr = ft_c[0:1, :] + 2.0 * qmax * stat_ref[2 + hh:3 + hh, 0:1] + FOX_SKIP
        cond = stat_ref[hh:hh + 1, :] > thr
        skippable = cond if skippable is None else skippable & cond
    n_skip = jnp.sum(jnp.where(skippable, 1, 0).astype(jnp.int32))

    rr = lax.broadcasted_iota(jnp.int32, (tq, tq), 0)
    cc = lax.broadcasted_iota(jnp.int32, (tq, tq), 1)
    causal = cc <= rr

    def block(carry, start, width, diag):
        subs = [pl.ds(pl.multiple_of(start + c * FOX_SUB, FOX_SUB), FOX_SUB) for c in range(width // FOX_SUB)]
        ms, ls, acc = carry
        new_m, new_l, alphas, pvs = [], [], [], []
        scores = []
        for hh in range(2):
            ss = []
            for c, sub in enumerate(subs):
                s = _dot(qaug[hh], kaug_ref[:, sub])
                dc = c * FOX_SUB - (width - tq)
                if diag and dc >= 0:
                    s = jnp.where(causal[:, dc:dc + FOX_SUB], s, -jnp.inf)
                ss.append(s)
            scores.append(ss)
        for hh in range(2):
            m_blk = jnp.max(functools.reduce(jnp.maximum, scores[hh]), axis=1, keepdims=True)
            new_m.append(jnp.maximum(ms[hh], m_blk))
            alphas.append(jnp.exp(ms[hh] - new_m[hh]))
        pv = [None, None]
        for c in range(len(subs)):
            for hh in range(2):
                p = jnp.exp(scores[hh][c] - new_m[hh]).astype(BF16)
                d = _dot_nt(p, vaug_ref[:, subs[c]])
                pv[hh] = d if pv[hh] is None else pv[hh] + d
        for hh in range(2):
            new_l.append(alphas[hh] * ls[hh] + pv[hh][:, hw:hw + 1])
            pvs.append(pv[hh][:, :hw])
        acc = jnp.where(low, alphas[0], alphas[1]) * acc + jnp.where(low, pvs[0], pvs[1])
        return tuple(new_m), tuple(new_l), acc

    neg = jnp.full((tq, 1), -jnp.inf, F32)
    zero = jnp.zeros((tq, 1), F32)
    init = ((neg, neg), (zero, zero), jnp.zeros((tq, hw), F32))
    live = i - n_skip
    base = n_skip * tq
    carry = lax.fori_loop(0, live // FOX_WIDE, lambda jb, c: block(c, base + jb * wide, wide, False), init)
    tail_start = base + (live // FOX_WIDE) * wide

    def tail(r):
        return lambda c: block(c, tail_start, (r + 1) * tq, True)

    def pick(lo, hi):
        if hi - lo == 1:
            return tail(lo)
        mid = (lo + hi) // 2
        return lambda c: lax.cond(live % FOX_WIDE < mid, pick(lo, mid), pick(mid, hi), c)

    _, ls, acc = pick(0, FOX_WIDE)(carry)
    o_ref[...] = (acc / jnp.where(low, ls[0], ls[1])).astype(o_ref.dtype)


def _fox_prompt(u, g, gt, b, s, layer):
    tq = min(512, s)
    nq = s // tq
    m = b * s
    hw = 2 * DH_F
    return pl.pallas_call(
        functools.partial(_foxp_kernel, tq=tq),
        grid=(b, NH_F // 2, nq),
        in_specs=[pl.BlockSpec((tq, hw), lambda bb, hp, i: (bb * nq + i, hp)),
                  pl.BlockSpec((1, 1, hw, s), lambda bb, hp, i: (layer, bb, hp, 0)),
                  pl.BlockSpec((1, 1, hw, s), lambda bb, hp, i: (layer, bb, hp, 0)),
                  pl.BlockSpec((tq, LANES), lambda bb, hp, i: (bb * nq + i, 0)),
                  pl.BlockSpec((SUBLANES, s), lambda bb, hp, i: (LN_FT // SUBLANES, bb))],
        out_specs=pl.BlockSpec((tq, hw), lambda bb, hp, i: (bb * nq + i, hp)),
        out_shape=jax.ShapeDtypeStruct((m, W_F), BF16),
        scratch_shapes=[pltpu.VMEM((2 * hw, s), BF16), pltpu.VMEM((FOX_VROWS, s), BF16),
                        pltpu.VMEM((SUBLANES, LANES), F32)],
        compiler_params=_cparams(("arbitrary", "arbitrary", "arbitrary")),
        name="fox_prompt",
    )(u["fq"], u["fk"], u["fv"], g, gt)


def _foxs_kernel(pt_ref, q_ref, kn_ref, vn_ref, gt_ref, *rest, pages):
    del pt_ref
    k_refs, v_refs, lf_refs = rest[:pages], rest[pages:2 * pages], rest[2 * pages:3 * pages]
    o_ref, qbd_ref, m_ref, l_ref, acc_ref, carry_ref, kcat_ref, vcat_ref = rest[3 * pages:]
    gi = pl.program_id(1)
    t = q_ref.shape[0]
    ht = NH_F * t
    page = k_refs[0].shape[2]

    def expand_heads(x):
        return jnp.broadcast_to(x[:, None, :], (NH_F, t, x.shape[1])).reshape(ht, x.shape[1])

    @pl.when(gi == 0)
    def _():
        q = q_ref[...].astype(F32)
        qt = jnp.broadcast_to(q[None], (NH_F, t, W_F)).reshape(ht, W_F)
        rh = lax.broadcasted_iota(jnp.int32, (ht, W_F), 0) // t
        lh = lax.broadcasted_iota(jnp.int32, (ht, W_F), 1) // DH_F
        qbd_ref[...] = jnp.where(rh == lh, qt, 0.0)
        m_ref[...] = jnp.full_like(m_ref, -jnp.inf)
        l_ref[...] = jnp.zeros_like(l_ref)
        acc_ref[...] = jnp.zeros_like(acc_ref)
        carry_ref[...] = jnp.zeros_like(carry_ref)

    cn_rows = expand_heads(gt_ref[0][LN_FT:LN_FT + NH_F, :])
    tq_idx = lax.broadcasted_iota(jnp.int32, (ht, t), 0) % t
    tk_idx = lax.broadcasted_iota(jnp.int32, (ht, t), 1)
    cn_col = jnp.sum(jnp.where(tq_idx == tk_idx, cn_rows, 0.0), axis=1, keepdims=True)

    qbd = qbd_ref[...]
    qbd_b = qbd.astype(BF16)

    def update(s, pv):
        m = m_ref[...]
        m_new = jnp.maximum(m, jnp.max(s, axis=1, keepdims=True))
        p = jnp.exp(s - m_new)
        alpha = jnp.exp(m - m_new)
        l_ref[...] = alpha * l_ref[...] + jnp.sum(p, axis=1, keepdims=True)
        acc_ref[...] = alpha * acc_ref[...] + pv(p)
        m_ref[...] = m_new

    lf_all = jnp.concatenate([lf_refs[pi][0] for pi in range(pages)], axis=0)
    tri = jnp.where(lax.broadcasted_iota(jnp.int32, (page, page), 0)
                    <= lax.broadcasted_iota(jnp.int32, (page, page), 1), 1.0, 0.0).astype(BF16)
    pre_all = sum(_dot(part.astype(BF16), tri) for part in _split3(lf_all))
    carry = carry_ref[...]
    rs = [None] * pages
    for pi in reversed(range(pages)):
        pre = pre_all[pi * NH_F:(pi + 1) * NH_F, :]
        tot = pre[:, page - 1:page]
        rs[pi] = expand_heads(carry + tot - pre)
        carry = carry + tot
        kcat_ref[:, pi * page:(pi + 1) * page] = k_refs[pi][0].astype(BF16)
        vcat_ref[:, pi * page:(pi + 1) * page] = v_refs[pi][0].astype(BF16)
    carry_ref[...] = carry
    s_all = _dot(qbd_b, kcat_ref[...]) + jnp.concatenate(rs, axis=1) + cn_col
    update(s_all, lambda p: _dot_nt(p.astype(BF16), vcat_ref[...]))

    @pl.when(gi == pl.num_programs(1) - 1)
    def _():
        s_new = _dot_nt(qbd, kn_ref[...]) + (cn_col - cn_rows)
        s_new = jnp.where(tk_idx <= tq_idx, s_new, -jnp.inf)
        update(s_new, lambda p: _dot(p, vn_ref[...]))
        o = acc_ref[...] / l_ref[...]
        lh = lax.broadcasted_iota(jnp.int32, (t, W_F), 1) // DH_F
        y = jnp.zeros((t, W_F), F32)
        for h in range(NH_F):
            y = y + jnp.where(lh == h, o[h * t:(h + 1) * t, :], 0.0)
        o_ref[...] = y.astype(o_ref.dtype)


def _fox_sample(u, gt3, page_table, ck_t, cv_t, clf_t, page_base, b, t):
    n_pages = page_table.shape[1]
    page = ck_t.shape[2]
    pages = FOXS_PAGES
    while n_pages % pages:
        pages //= 2
    ng = n_pages // pages
    ht = NH_F * t

    def page_spec(width_shape, pi):
        return pl.BlockSpec((1,) + width_shape,
                            lambda bb, gi, pt: (page_base + pt[bb, (ng - 1 - gi) * pages + pi], 0, 0))

    tok = pl.BlockSpec((t, W_F), lambda bb, gi, pt: (bb, 0))
    in_specs = [tok, tok, tok, pl.BlockSpec((1, LANES, t), lambda bb, gi, pt: (bb, 0, 0))]
    in_specs += [page_spec((W_F, page), pi) for pi in range(pages)]
    in_specs += [page_spec((W_F, page), pi) for pi in range(pages)]
    in_specs += [page_spec((NH_F, page), pi) for pi in range(pages)]
    grid_spec = pltpu.PrefetchScalarGridSpec(
        num_scalar_prefetch=1,
        grid=(b, ng),
        in_specs=in_specs,
        out_specs=pl.BlockSpec((t, W_F), lambda bb, gi, pt: (bb, 0)),
        scratch_shapes=[pltpu.VMEM((ht, W_F), F32), pltpu.VMEM((ht, 1), F32), pltpu.VMEM((ht, 1), F32),
                        pltpu.VMEM((ht, W_F), F32), pltpu.VMEM((NH_F, 1), F32),
                        pltpu.VMEM((W_F, pages * page), BF16), pltpu.VMEM((W_F, pages * page), BF16)],
    )
    return pl.pallas_call(
        functools.partial(_foxs_kernel, pages=pages),
        grid_spec=grid_spec,
        out_shape=jax.ShapeDtypeStruct((b * t, W_F), BF16),
        compiler_params=_cparams(("arbitrary", "arbitrary")),
        name="fox_sample",
    )(page_table, u["fq"], u["fk"], u["fv"], gt3, *([ck_t] * pages), *([cv_t] * pages), *([clf_t] * pages))


def _merge_kernel(x_ref, sh_ref, sc_ref, gt_ref, gpre_ref, gpost_ref, ym_ref, ys_ref, yf_ref,
                  wg_ref, wm_ref, ws_ref, wf_ref, wo_ref, o_ref):
    x = x_ref[...]
    tb, ts, d = x.shape
    h = _rms(x, gpre_ref[...]) * (1.0 + sc_ref[...]) + sh_ref[...]
    hb = h.reshape(tb * ts, d).astype(BF16)
    merged = None
    for bi, (y_ref, w_ref) in enumerate(((ym_ref, wm_ref), (ys_ref, ws_ref), (yf_ref, wf_ref))):
        gate = _sigmoid(_dot(hb, wg_ref[:, bi * d:(bi + 1) * d]))
        term = gate * _dot(y_ref[...], w_ref[...])
        merged = term if merged is None else merged + term
    out = _dot(merged.astype(BF16), wo_ref[...]).reshape(tb, ts, d)
    o_ref[...] = x + gt_ref[...] * _rms(out, gpost_ref[...])


def _merge(x, ada, g_pre, g_post, ym, ys, yf, wg, wm, ws, wf, wo):
    b, s, d = x.shape
    tb, ts = _x_tiles(b, s, 512)
    tm = tb * ts
    nj = s // ts
    row = lambda i, j: (i * nj + j, 0)
    wspec = lambda shape: pl.BlockSpec(shape, lambda i, j: (0, 0), pipeline_mode=pl.Buffered(1))
    return pl.pallas_call(
        _merge_kernel,
        grid=(b // tb, nj),
        in_specs=[pl.BlockSpec((tb, ts, d), lambda i, j: (i, j, 0)),
                  _ada_spec(tb, d, 0), _ada_spec(tb, d, 1), _ada_spec(tb, d, 2),
                  _const_spec((1, d)), _const_spec((1, d)),
                  pl.BlockSpec((tm, W_M), row), pl.BlockSpec((tm, D_INNER), row), pl.BlockSpec((tm, W_F), row),
                  wspec((d, 3 * d)), wspec((W_M, d)), wspec((D_INNER, d)), wspec((W_F, d)), wspec((d, d))],
        out_specs=pl.BlockSpec((tb, ts, d), lambda i, j: (i, j, 0)),
        out_shape=jax.ShapeDtypeStruct((b, s, d), F32),
        compiler_params=_cparams(("arbitrary", "arbitrary")),
        name="merge",
    )(x, ada, ada, ada, g_pre.reshape(1, d), g_post.reshape(1, d), ym, ys, yf, wg, wm, ws, wf, wo)


def _mlp_kernel(x_ref, sh_ref, sc_ref, gt_ref, gpre_ref, gpost_ref, wu_ref, wd_ref, o_ref):
    x = x_ref[...]
    tb, ts, d = x.shape
    h = _rms(x, gpre_ref[...]) * (1.0 + sc_ref[...]) + sh_ref[...]
    hb = h.reshape(tb * ts, d).astype(BF16)
    up = jnp.maximum(_dot(hb, wu_ref[...]), 0.0)
    f = _dot((up * up).astype(BF16), wd_ref[...]).reshape(tb, ts, d)
    o_ref[...] = x + gt_ref[...] * _rms(f, gpost_ref[...])


def _mlp(x, ada, g_pre, g_post, wu, wd):
    b, s, d = x.shape
    tb, ts = _x_tiles(b, s, 512)
    nj = s // ts
    dff = wu.shape[1]
    wspec = lambda shape: pl.BlockSpec(shape, lambda i, j: (0, 0), pipeline_mode=pl.Buffered(1))
    return pl.pallas_call(
        _mlp_kernel,
        grid=(b // tb, nj),
        in_specs=[pl.BlockSpec((tb, ts, d), lambda i, j: (i, j, 0)),
                  _ada_spec(tb, d, 3), _ada_spec(tb, d, 4), _ada_spec(tb, d, 5),
                  _const_spec((1, d)), _const_spec((1, d)),
                  wspec((d, dff)), wspec((dff, d))],
        out_specs=pl.BlockSpec((tb, ts, d), lambda i, j: (i, j, 0)),
        out_shape=jax.ShapeDtypeStruct((b, s, d), F32),
        compiler_params=_cparams(("arbitrary", "arbitrary")),
        name="mlp",
    )(x, ada, ada, ada, g_pre.reshape(1, d), g_post.reshape(1, d), wu, wd)


def _split_w_in(w):
    sizes = (NH_M * DQK_M, NH_M * DQK_M, W_M, NH_M, NH_M, W_M, D_INNER, CONV_CH, NH_S, W_F, W_F, W_F, NH_F)
    names = ("mq", "mk", "mv", "mi", "mf", "mo", "sz", "sxbc", "sdt", "fq", "fk", "fv", "ff")
    cols, off = {}, 0
    for nm, sz in zip(names, sizes):
        cols[nm] = w[:, off:off + sz]
        off += sz
    gates = w[:, off:]
    d = w.shape[0]
    zeros = lambda n: jnp.zeros((d, n), w.dtype)
    small = jnp.concatenate(
        [cols["mi"], cols["mf"], cols["sdt"], cols["ff"], cols["mf"], zeros(LN_CUM - LN_BM - NH_M),
         cols["sdt"], cols["ff"], zeros(LANES - LN_FT - NH_F)], axis=1)
    proj = jnp.concatenate([cols[sg[0]] for sg in SEGS[:-1]] + [small], axis=1)
    kv = jnp.concatenate([cols["fk"], cols["fv"]], axis=1)
    return proj.astype(BF16), kv.astype(BF16), gates.astype(BF16)


def _gate_lanes(b_mgate, dt_bias, b_ffox, a_log):
    z = lambda n: jnp.zeros((n,), F32)
    bias = jnp.concatenate([b_mgate, dt_bias, b_ffox, b_mgate[NH_M:], z(LN_CUM - LN_BM - NH_M),
                            dt_bias, b_ffox, z(LANES - LN_FT - NH_F)])
    alog = jnp.concatenate([z(LN_CUM), a_log, z(LANES - LN_CUM - NH_S)])
    return bias.reshape(1, LANES), alog.reshape(1, LANES)


def _mixer(x, ada, lw, state, attend, chunk, mm, kv_t, layer=0, depth=1, kv_all=None):
    b, s, d = x.shape
    u = _inproj(x, ada, lw["g_pre_mix"], lw["w_proj"], lw["w_kv"].T if kv_t else lw["w_kv"], kv_t,
                layer, depth, kv_all)
    g, gt = _prep(u["small"], lw["bias_lanes"], lw["alog_lanes"], s, chunk)
    if s % LANES:
        gt = gt.reshape(LANES, b, s).transpose(1, 0, 2)
    c0n, m0, conv0, h0 = state
    ym, c_n, m_new = _mlstm(u, g, gt, lw["g_mhead"], c0n, m0, b, s, chunk, mm)
    ys, h_new = _ssd(u, g, gt, lw["conv_w"], lw["conv_b"], lw["d_skip"], lw["g_ssm"], conv0, h0, b, s, chunk, mm)
    yf = attend(u, g, gt)
    x1 = _merge(x, ada, lw["g_pre_mix"], lw["g_post_mix"], ym, ys, yf,
                lw["w_gates"], lw["w_br_m"], lw["w_br_s"], lw["w_br_f"], lw["w_out"])
    x2 = _mlp(x1, ada, lw["g_pre_mlp"], lw["g_post_mlp"], lw["w_up"], lw["w_down"])
    if kv_t:
        rows = lambda a: a
    else:
        rows = lambda a: a.reshape(b, s, NH_F, DH_F)
    outs = (rows(u["fk"]), rows(u["fv"]),
            g[:, LN_FF:LN_FF + NH_F].reshape(b, s, NH_F),
            jnp.swapaxes(c_n[..., :DV_M], -1, -2), c_n[..., DV_M], m_new[:, :, 0, 0],
            u["sxbc"].reshape(b, s, CONV_CH)[:, s - (CONV_W - 1):, :], h_new)
    return x2, outs


def kernel(x_prompt, x_sample, cache_k, cache_v, cache_logf, state_mlstm_C, state_mlstm_n, state_mlstm_m,
           state_conv, state_ssm, page_table, c_prompt, c_sample, w_ada, b_ada, g_pre_mix, g_post_mix, w_in,
           b_mgate, b_ffox, g_mhead, conv_w, conv_b, dt_bias, a_log, d_skip, g_ssm, w_br_m, w_br_s, w_br_f,
           w_out, g_pre_mlp, g_post_mlp, w_up, w_down):
    depth = w_in.shape[0]
    bp, sp, d = x_prompt.shape
    bs, ss, _ = x_sample.shape
    n_phys, page = cache_k.shape[1], cache_k.shape[2]

    pad = (-(bp + bs)) % SUBLANES
    c_all = jnp.concatenate([c_prompt, c_sample, jnp.zeros((pad, d), F32)], axis=0)
    ada_all = _ada(c_all, w_ada, b_ada)

    ck_t = jnp.transpose(cache_k, (0, 1, 3, 4, 2)).reshape(depth * n_phys, W_F, page)
    cv_t = jnp.transpose(cache_v, (0, 1, 3, 4, 2)).reshape(depth * n_phys, W_F, page)
    clf_t = jnp.transpose(cache_logf, (0, 1, 3, 2)).reshape(depth * n_phys, NH_F, page)

    chunk_p = SCAN_CHUNK if sp % SCAN_CHUNK == 0 else (CHUNK if sp % CHUNK == 0 else sp)
    chunk_s = SCAN_CHUNK if ss % SCAN_CHUNK == 0 else (CHUNK if ss % CHUNK == 0 else ss)
    yp, ys = x_prompt, x_sample
    res_p, res_s = [], []
    kv_all = None
    for l in range(depth):
        w_proj, w_kv, w_gates = _split_w_in(w_in[l])
        bias_lanes, alog_lanes = _gate_lanes(b_mgate[l], dt_bias[l], b_ffox[l], a_log[l])
        lw = dict(w_proj=w_proj, w_kv=w_kv, w_gates=w_gates, bias_lanes=bias_lanes, alog_lanes=alog_lanes,
                  g_pre_mix=g_pre_mix[l], g_post_mix=g_post_mix[l], g_mhead=g_mhead[l],
                  conv_w=conv_w[l], conv_b=conv_b[l], d_skip=d_skip[l], g_ssm=g_ssm[l],
                  w_br_m=w_br_m[l].astype(BF16), w_br_s=w_br_s[l].astype(BF16), w_br_f=w_br_f[l].astype(BF16),
                  w_out=w_out[l].astype(BF16), g_pre_mlp=g_pre_mlp[l], g_post_mlp=g_post_mlp[l],
                  w_up=w_up[l].astype(BF16), w_down=w_down[l].astype(BF16))
        ada_p = ada_all[l, :bp][:, None, :]
        ada_s = ada_all[l, bp:bp + bs][:, None, :]

        zero_state = (jnp.zeros((bp, NH_M, DQK_M, 2 * DV_M), F32),
                      jnp.zeros((bp, NH_M, 1, LANES), F32), jnp.zeros((bp, SUBLANES, CONV_CH), F32),
                      jnp.zeros((bp, NH_S, P_S, D_STATE), F32))
        yp, outs = _mixer(yp, ada_p, lw, zero_state,
                          lambda u, g, gt, l=l: _fox_prompt(u, g, gt, bp, sp, l), chunk_p, BF16, True,
                          l, depth, kv_all)
        kv_all = outs[:2]
        res_p.append(outs)

        state = (jnp.concatenate([jnp.swapaxes(state_mlstm_C[l], -1, -2),
                                  jnp.broadcast_to(state_mlstm_n[l][..., None], (bs, NH_M, DQK_M, DV_M))], axis=-1),
                 jnp.broadcast_to(state_mlstm_m[l][:, :, None, None], (bs, NH_M, 1, LANES)),
                 jnp.pad(state_conv[l], ((0, 0), (SUBLANES - (CONV_W - 1), 0), (0, 0))),
                 state_ssm[l])
        ys, outs = _mixer(ys, ada_s, lw, state,
                          lambda u, g, gt, l=l: _fox_sample(u, gt, page_table, ck_t, cv_t, clf_t, l * n_phys, bs, ss),
                          chunk_s, F32, False)
        res_s.append(outs)

    stack = lambda res, i: jnp.stack([r[i] for r in res])
    kv_prompt = tuple(jnp.transpose(a.reshape(depth, bp, NH_F, DH_F, sp), (0, 1, 4, 2, 3)) for a in kv_all)
    return ((yp, ys) + kv_prompt + tuple(stack(res_p, i) for i in range(2, 8))
            + tuple(stack(res_s, i) for i in range(8)))
```
